```python
import jax
import jax.numpy as jnp
from jax import lax
import numpy as np

D_MODEL = 1024
BATCH = 2
SEQ = 8192
DEPTH = 1

GRID_W = 64
CTX_LEN = 256
NA_HEADS = 8
NA_HEAD_DIM = 64
NA_WIDTH = NA_HEADS * NA_HEAD_DIM
NA_KR_MAX = 8
NA_KC = 16
HG_HEADS = 4
HG_DK = 128
HG_DV = 128
HG_K_WIDTH = HG_HEADS * HG_DK
HG_V_WIDTH = HG_HEADS * HG_DV
HG_CHUNK = 64
MIX_WIDTH = NA_WIDTH + HG_V_WIDTH
IN_SPLITS = (NA_WIDTH, NA_WIDTH, NA_WIDTH, HG_K_WIDTH, HG_K_WIDTH, HG_K_WIDTH, HG_V_WIDTH, HG_V_WIDTH)
IN_WIDTH = sum(IN_SPLITS)
D_FF = -(-8 * D_MODEL // (3 * 256)) * 256
N_MOD = 6
EPS = 1e-6

kernel_name = 'hybrid_natten_hgrn2_dit_layer'


def rmsnorm(x, w):
    xf = x.astype(jnp.float32)
    y = xf * lax.rsqrt(jnp.mean(xf * xf, axis=-1, keepdims=True) + EPS)
    return (y * w.astype(jnp.float32)).astype(x.dtype)


def split_proj(p):
    offsets = tuple(int(o) for o in np.cumsum(IN_SPLITS)[:-1])
    return jnp.split(p, offsets, axis=-1)


def to_heads(t, n_heads):
    return t.reshape(t.shape[0], t.shape[1], n_heads, -1)


def rev(t):
    return jnp.flip(t, axis=1)


def swiglu(h, w_ffn_in, w_ffn_out):
    gate, up = jnp.split(h @ w_ffn_in, 2, axis=-1)
    return (jax.nn.silu(gate) * up) @ w_ffn_out


def neighbourhood_attention(q, k, v, k_ctx, v_ctx, rpb):
    B, T = q.shape[0], q.shape[1]
    rows = T // GRID_W
    kr = min(NA_KR_MAX, rows)
    n_win = kr * NA_KC
    grid = lambda t: t.reshape(B, rows, GRID_W, NA_HEADS, NA_HEAD_DIM)
    qg = grid(q * NA_HEAD_DIM ** -0.5)
    kg, vg = grid(k), grid(v)
    cols = np.arange(GRID_W)
    col_idx = np.clip(cols - NA_KC // 2, 0, GRID_W - NA_KC)[:, None] + np.arange(NA_KC)[None, :]
    dc_idx = col_idx - cols[:, None] + (NA_KC - 1)

    def row_block(args):
        r, q_r = args
        r0 = jnp.clip(r - kr // 2, 0, rows - kr)
        k_win = lax.dynamic_slice_in_dim(kg, r0, kr, axis=1)[:, :, col_idx]
        v_win = lax.dynamic_slice_in_dim(vg, r0, kr, axis=1)[:, :, col_idx]
        dr_idx = r0 + jnp.arange(kr) - r + (NA_KR_MAX - 1)
        bias = rpb[:, dr_idx[None, :, None], dc_idx[:, None, :]].astype(jnp.float32)
        s_win = jnp.einsum('bwhd,brwkhd->bhwrk', q_r, k_win).astype(jnp.float32) + bias
        s_ctx = jnp.einsum('bwhd,blhd->bhwl', q_r, k_ctx).astype(jnp.float32)
        s = jnp.concatenate([s_win.reshape(B, NA_HEADS, GRID_W, n_win), s_ctx], axis=-1)
        p = jax.nn.softmax(s, axis=-1).astype(v.dtype)
        p_win = p[..., :n_win].reshape(B, NA_HEADS, GRID_W, kr, NA_KC)
        return (jnp.einsum('bhwrk,brwkhd->bwhd', p_win, v_win)
                + jnp.einsum('bhwl,blhd->bwhd', p[..., n_win:], v_ctx))

    o = lax.map(row_block, (jnp.arange(rows), jnp.moveaxis(qg, 1, 0)))
    return jnp.moveaxis(o, 0, 1).reshape(B, T, NA_WIDTH)


def context_attention(q, k, v):
    B, L = q.shape[0], q.shape[1]
    s = jnp.einsum('blhd,bmhd->bhlm', q * NA_HEAD_DIM ** -0.5, k).astype(jnp.float32)
    p = jax.nn.softmax(s, axis=-1).astype(v.dtype)
    return jnp.einsum('bhlm,bmhd->blhd', p, v).reshape(B, L, NA_WIDTH)


def hgrn2_gates(f_logits, lb):
    f = lb + (1.0 - lb) * jax.nn.sigmoid(f_logits.astype(jnp.float32))
    return to_heads(1.0 - f, HG_HEADS), to_heads(jnp.log(f), HG_HEADS)


def gla_chunk_scan(q, k, v, g, s0):
    B, T = q.shape[0], q.shape[1]
    n_chunks = T // HG_CHUNK
    to_chunks = lambda t: t.reshape(B, n_chunks, HG_CHUNK, HG_HEADS, t.shape[-1]).transpose(1, 0, 3, 2, 4)
    lower_tri = np.tril(np.ones((HG_CHUNK, HG_CHUNK), dtype=bool))

    def step(s, inp):
        qc, kc, vc, gc = inp
        b = jnp.cumsum(gc, axis=2)
        o_inter = jnp.einsum('bhtk,bhkv->bhtv', qc * jnp.exp(b), s)
        diff = jnp.where(lower_tri[:, :, None], b[:, :, :, None, :] - b[:, :, None, :, :], -jnp.inf)
        a = jnp.einsum('bhtk,bhsk,bhtsk->bhts', qc, kc, jnp.exp(diff))
        o = o_inter + jnp.einsum('bhts,bhsv->bhtv', a, vc)
        b_last = b[:, :, -1]
        s_new = (jnp.exp(b_last)[..., None] * s
                 + jnp.einsum('bhsk,bhsv->bhkv', kc * jnp.exp(b_last[:, :, None, :] - b), vc))
        return s_new, o

    _, o = lax.scan(step, s0, (to_chunks(q), to_chunks(k), to_chunks(v), to_chunks(g)))
    return o.transpose(1, 0, 3, 2, 4).reshape(B, T, HG_HEADS, HG_DV)


def gla_final_state(k, v, g):
    b = jnp.cumsum(g, axis=1)
    w = jnp.exp(b[:, -1:] - b)
    return jnp.einsum('blhk,blhv->bhkv', k * w, v)


def gated_group_norm(o, gate, w):
    o = o * lax.rsqrt(jnp.mean(o * o, axis=-1, keepdims=True) + EPS) * w.astype(jnp.float32)
    o = o * jax.nn.silu(to_heads(gate, HG_HEADS).astype(jnp.float32))
    return o.reshape(o.shape[0], o.shape[1], HG_V_WIDTH)


def hybrid_mixer(h, h_c, w_in, w_out, rpb, lb, hg_norm_w, with_ctx_out):
    f32 = jnp.float32
    B = h.shape[0]
    na_q, na_k, na_v, hg_q, hg_ff, hg_fb, hg_i, hg_g = split_proj(h @ w_in)
    na_qc, na_kc, na_vc, hg_qc, hg_ffc, hg_fbc, hg_ic, hg_gc = split_proj(h_c @ w_in)
    k_c_na, v_c_na = to_heads(na_kc, NA_HEADS), to_heads(na_vc, NA_HEADS)
    y_na = neighbourhood_attention(to_heads(na_q, NA_HEADS), to_heads(na_k, NA_HEADS),
                                   to_heads(na_v, NA_HEADS), k_c_na, v_c_na, rpb)
    q = to_heads(jax.nn.silu(hg_q), HG_HEADS).astype(f32)
    v = to_heads(hg_i, HG_HEADS).astype(f32)
    k_fw, g_fw = hgrn2_gates(hg_ff, lb[0])
    k_bw, g_bw = hgrn2_gates(hg_fb, lb[1])
    v_c = to_heads(hg_ic, HG_HEADS).astype(f32)
    k_cfw, g_cfw = hgrn2_gates(hg_ffc, lb[0])
    k_cbw, g_cbw = hgrn2_gates(hg_fbc, lb[1])
    s_fw = gla_final_state(k_cfw, v_c, g_cfw)
    s_bw = gla_final_state(rev(k_cbw), rev(v_c), rev(g_cbw))
    o_hg = (gla_chunk_scan(q, k_fw, v, g_fw, s_fw)
            + rev(gla_chunk_scan(rev(q), rev(k_bw), rev(v), rev(g_bw), s_bw)))
    y_hg = gated_group_norm(o_hg, hg_g, hg_norm_w).astype(h.dtype)
    y = jnp.concatenate([y_na, y_hg], axis=-1) @ w_out
    if not with_ctx_out:
        return y, None
    y_c_na = context_attention(to_heads(na_qc, NA_HEADS), k_c_na, v_c_na)
    q_c = to_heads(jax.nn.silu(hg_qc), HG_HEADS).astype(f32)
    zero_state = jnp.zeros((B, HG_HEADS, HG_DK, HG_DV), f32)
    o_c = (gla_chunk_scan(q_c, k_cfw, v_c, g_cfw, zero_state)
           + rev(gla_chunk_scan(rev(q_c), rev(k_cbw), rev(v_c), rev(g_cbw), zero_state)))
    y_c_hg = gated_group_norm(o_c, hg_gc, hg_norm_w).astype(h.dtype)
    y_c = jnp.concatenate([y_c_na, y_c_hg], axis=-1) @ w_out
    return y, y_c


def trunk_layer(x, ctx, c, c_ctx, w_ada, b_ada, n_mix_pre, n_mix_post, n_ffn_pre, n_ffn_post,
                w_in, rpb, lb, hg_norm_w, w_out, w_ffn_in, w_ffn_out, last):
    sh_m, sc_m, gt_m, sh_f, sc_f, gt_f = jnp.split((jax.nn.silu(c) @ w_ada + b_ada)[:, None, :], N_MOD, axis=-1)
    csh_m, csc_m, cgt_m, csh_f, csc_f, cgt_f = jnp.split(jax.nn.silu(c_ctx) @ w_ada + b_ada, N_MOD, axis=-1)
    h = rmsnorm(x, n_mix_pre) * (1.0 + sc_m) + sh_m
    h_c = rmsnorm(ctx, n_mix_pre) * (1.0 + csc_m) + csh_m
    y, y_c = hybrid_mixer(h, h_c, w_in, w_out, rpb, lb, hg_norm_w, not last)
    x = x + gt_m * rmsnorm(y, n_mix_post)
    h = rmsnorm(x, n_ffn_pre) * (1.0 + sc_f) + sh_f
    x = x + gt_f * rmsnorm(swiglu(h, w_ffn_in, w_ffn_out), n_ffn_post)
    if last:
        return x, ctx
    ctx = ctx + cgt_m * rmsnorm(y_c, n_mix_post)
    h_c = rmsnorm(ctx, n_ffn_pre) * (1.0 + csc_f) + csh_f
    ctx = ctx + cgt_f * rmsnorm(swiglu(h_c, w_ffn_in, w_ffn_out), n_ffn_post)
    return x, ctx


def setup_inputs(seed: int = 0) -> dict:
    key = jax.random.key(seed)
    ks = jax.random.split(key, 17)
    nrm = lambda k, shape, scale: scale * jax.random.normal(k, shape, jnp.float32)
    gain = lambda k, n: 1.0 + nrm(k, (DEPTH, n), 0.01)
    return {
        'x': nrm(ks[0], (BATCH, SEQ, D_MODEL), 1.0),
        'c': nrm(ks[1], (BATCH, D_MODEL), 1.0),
        'ctx': nrm(ks[2], (BATCH, CTX_LEN, D_MODEL), 1.0),
        'c_ctx': nrm(ks[3], (D_MODEL,), 1.0),
        'w_ada': nrm(ks[4], (DEPTH, D_MODEL, N_MOD * D_MODEL), 0.5 * D_MODEL ** -0.5),
        'b_ada': nrm(ks[5], (DEPTH, N_MOD * D_MODEL), 0.01),
        'norm_mix_pre': gain(ks[6], D_MODEL),
        'norm_mix_post': gain(ks[7], D_MODEL),
        'norm_ffn_pre': gain(ks[8], D_MODEL),
        'norm_ffn_post': gain(ks[9], D_MODEL),
        'w_in': nrm(ks[10], (DEPTH, D_MODEL, IN_WIDTH), D_MODEL ** -0.5),
        'na_rpb': nrm(ks[11], (DEPTH, NA_HEADS, 2 * NA_KR_MAX - 1, 2 * NA_KC - 1), 0.1),
        'hg_lb_logits': nrm(ks[12], (DEPTH + 1, 2, HG_K_WIDTH), 0.5),
        'hg_norm_w': gain(ks[13], HG_DV),
        'w_out': nrm(ks[14], (DEPTH, MIX_WIDTH, D_MODEL), MIX_WIDTH ** -0.5),
        'w_ffn_in': nrm(ks[15], (DEPTH, D_MODEL, 2 * D_FF), D_MODEL ** -0.5),
        'w_ffn_out': nrm(ks[16], (DEPTH, D_FF, D_MODEL), D_FF ** -0.5),
    }


def reference(x, c, ctx, c_ctx, w_ada, b_ada, norm_mix_pre, norm_mix_post, norm_ffn_pre, norm_ffn_post,
              w_in, na_rpb, hg_lb_logits, hg_norm_w, w_out, w_ffn_in, w_ffn_out):
    lower_bounds = jnp.cumsum(jax.nn.softmax(hg_lb_logits.astype(jnp.float32), axis=0), axis=0)
    for l in range(DEPTH):
        x, ctx = trunk_layer(x, ctx, c, c_ctx, w_ada[l], b_ada[l], norm_mix_pre[l], norm_mix_post[l],
                             norm_ffn_pre[l], norm_ffn_post[l], w_in[l], na_rpb[l], lower_bounds[l],
                             hg_norm_w[l], w_out[l], w_ffn_in[l], w_ffn_out[l], l == DEPTH - 1)
    return x
```

```python
import functools

import jax
import jax.numpy as jnp
import numpy as np
from jax import lax
from jax.experimental import pallas as pl
from jax.experimental.pallas import tpu as pltpu

D_MODEL = 1024
GRID_W = 64
NA_HEADS = 8
NA_HEAD_DIM = 64
NA_WIDTH = NA_HEADS * NA_HEAD_DIM
NA_KR = 8
NA_KC = 16
HG_HEADS = 4
HG_DK = 128
HG_WIDTH = HG_HEADS * HG_DK
HG_CHUNK = 64
N_LEVELS = 6
N_MOD = 6
EPS = 1e-6
NEG = -1e30

NA_GROUP = 4
NA_GW = NA_GROUP * NA_HEAD_DIM

F32 = jnp.float32
BF16 = jnp.bfloat16

VMEM_LIMIT = 56 * 1024 * 1024


def _silu(x):
    return x * jax.nn.sigmoid(x)


def _dot(a, b):
    return jnp.dot(a, b, preferred_element_type=F32)


def _dot_nt(a, b):
    return lax.dot_general(a, b, (((1,), (1,)), ((), ())), preferred_element_type=F32)


def _dot_tn(a, b):
    return lax.dot_general(a, b, (((0,), (0,)), ((), ())), preferred_element_type=F32)


def _split3(x):
    x1 = x.astype(BF16)
    r1 = x - x1.astype(F32)
    x2 = r1.astype(BF16)
    r2 = r1 - x2.astype(F32)
    return x1, x2, r2.astype(BF16)


def _dot_exact_lhs(t, x):
    x1, x2, x3 = _split3(x)
    return _dot(t, x1) + _dot(t, x2) + _dot(t, x3)


def _rms(x, w):
    return x * lax.rsqrt(jnp.mean(x * x, axis=-1, keepdims=True) + EPS) * w


def _ada_kernel(cv_ref, w_ref, b_ref, o_ref):
    s = _silu(cv_ref[...])
    s1, s2, s3 = _split3(s)
    w1, w2, w3 = _split3(w_ref[...])
    acc = _dot(s1, w1) + (_dot(s1, w2) + _dot(s2, w1)) + (_dot(s1, w3) + _dot(s2, w2) + _dot(s3, w1))
    o_ref[...] = acc + b_ref[...]


def _ada(cv, w_ada, b_ada):
    n = w_ada.shape[1]
    tn = 1536
    return pl.pallas_call(
        _ada_kernel,
        grid=(n // tn,),
        in_specs=[
            pl.BlockSpec((8, D_MODEL), lambda j: (0, 0)),
            pl.BlockSpec((D_MODEL, tn), lambda j: (0, j)),
            pl.BlockSpec((1, tn), lambda j: (0, j)),
        ],
        out_specs=pl.BlockSpec((8, tn), lambda j: (0, j)),
        out_shape=jax.ShapeDtypeStruct((8, n), F32),
        compiler_params=pltpu.CompilerParams(vmem_limit_bytes=VMEM_LIMIT),
        name="ada",
    )(cv, w_ada, b_ada)


def _in_proj_kernel(x_ref, sc_ref, sh_ref, nw_ref, w_ref, lbl_ref,
                    qna_ref, kna_ref, vna_ref, qhg_ref, gfw_ref, gbw_ref, vhg_ref, gate_ref):
    x = x_ref[0]
    h = _rms(x, nw_ref[...]) * (1.0 + sc_ref[0]) + sh_ref[0]
    hb = h.astype(BF16)

    def proj(i):
        return _dot(hb, w_ref[:, i * 512:(i + 1) * 512])

    lbl = lbl_ref[...]
    e = jnp.exp(lbl - jnp.max(lbl, axis=0, keepdims=True))
    lb = e[0:1] / jnp.sum(e, axis=0, keepdims=True)

    qna_ref[0] = (proj(0) * (NA_HEAD_DIM ** -0.5)).astype(BF16)
    kna_ref[0] = proj(1).astype(BF16)
    vna_ref[0] = proj(2).astype(BF16)
    qhg_ref[0] = _silu(proj(3)).astype(BF16)
    lb_f = lb[:, :HG_WIDTH]
    lb_b = lb[:, HG_WIDTH:]
    gfw_ref[0] = jnp.log(lb_f + (1.0 - lb_f) * jax.nn.sigmoid(proj(4)))
    gbw_ref[0] = jnp.log(lb_b + (1.0 - lb_b) * jax.nn.sigmoid(proj(5)))
    vhg_ref[0] = proj(6).astype(BF16)
    gate_ref[0] = _silu(proj(7)).astype(BF16)


def _in_proj(x, sc, sh, nw, w_bf, lbl, tm):
    B, T, _ = x.shape
    tok = lambda b, i: (b, i, 0)
    out_bf = jax.ShapeDtypeStruct((B, T, 512), BF16)
    out_f = jax.ShapeDtypeStruct((B, T, 512), F32)
    ospec = pl.BlockSpec((1, tm, 512), tok)
    return pl.pallas_call(
        _in_proj_kernel,
        grid=(B, T // tm),
        in_specs=[
            pl.BlockSpec((1, tm, D_MODEL), tok),
            pl.BlockSpec((1, 1, D_MODEL), lambda b, i: (b, 0, 0)),
            pl.BlockSpec((1, 1, D_MODEL), lambda b, i: (b, 0, 0)),
            pl.BlockSpec((1, D_MODEL), lambda b, i: (0, 0)),
            pl.BlockSpec(w_bf.shape, lambda b, i: (0, 0)),
            pl.BlockSpec(lbl.shape, lambda b, i: (0, 0)),
        ],
        out_specs=[ospec] * 8,
        out_shape=[out_bf, out_bf, out_bf, out_bf, out_f, out_f, out_bf, out_bf],
        compiler_params=pltpu.CompilerParams(
            dimension_semantics=("parallel", "parallel"), vmem_limit_bytes=VMEM_LIMIT),
        name="in_proj",
    )(x, sc, sh, nw, w_bf, lbl)


def _ctx_state_kernel(gfw_ref, gbw_ref, v_ref, sfw_ref, sbw_ref):
    L = gfw_ref.shape[1]
    r = lax.broadcasted_iota(jnp.int32, (L, L), 0)
    c = lax.broadcasted_iota(jnp.int32, (L, L), 1)
    upper = jnp.where(c > r, 1.0, 0.0).astype(BF16)
    lower = jnp.where(c < r, 1.0, 0.0).astype(BF16)
    v = v_ref[0]
    for g_ref, tri, s_ref in ((gfw_ref, upper, sfw_ref), (gbw_ref, lower, sbw_ref)):
        g = g_ref[0]
        kw = ((1.0 - jnp.exp(g)) * jnp.exp(_dot_exact_lhs(tri, g))).astype(BF16)
        for h in range(HG_HEADS):
            sl = slice(h * HG_DK, (h + 1) * HG_DK)
            s_ref[0, h] = _dot_tn(v[:, sl], kw[:, sl])


def _ctx_state(g_cfw, g_cbw, v_c):
    B, L, _ = g_cfw.shape
    tok = pl.BlockSpec((1, L, HG_WIDTH), lambda b: (b, 0, 0))
    st = pl.BlockSpec((1, HG_HEADS, HG_DK, HG_DK), lambda b: (b, 0, 0, 0))
    sshape = jax.ShapeDtypeStruct((B, HG_HEADS, HG_DK, HG_DK), F32)
    return pl.pallas_call(
        _ctx_state_kernel,
        grid=(B,),
        in_specs=[tok, tok, tok],
        out_specs=[st, st],
        out_shape=[sshape, sshape],
        compiler_params=pltpu.CompilerParams(vmem_limit_bytes=VMEM_LIMIT),
        name="ctx_state",
    )(g_cfw, g_cbw, v_c)


def _hg_masks():
    C = HG_CHUNK
    t = np.arange(C)[:, None]
    s = np.arange(C)[None, :]
    fw = []
    for j in range(N_LEVELS):
        hs = 1 << j
        same = (t // (2 * hs)) == (s // (2 * hs))
        fw.append(same & ((t % (2 * hs)) >= hs) & ((s % (2 * hs)) < hs))
    fw.append(t == s)
    fw = np.stack(fw).astype(np.float32)
    return np.stack([fw, fw.transpose(0, 2, 1)])


def _hg_ref_rows(bscr, n_rows, hs, rev):
    blk = 2 * hs
    off = hs if rev else hs - 1
    bcast = lambda m, n: jnp.broadcast_to(bscr[pl.ds(m, 1), :], (n, HG_DK))
    if blk >= 8:
        return jnp.concatenate([bcast(s + off, blk) for s in range(0, n_rows, blk)], axis=0)
    sub = lax.broadcasted_iota(jnp.int32, (8, HG_DK), 0)
    pieces = []
    for s in range(0, n_rows, 8):
        piece = bcast(s + off, 8)
        for a in range(1, 8 // blk):
            piece = jnp.where(sub >= a * blk, bcast(s + a * blk + off, 8), piece)
        pieces.append(piece)
    return jnp.concatenate(pieces, axis=0)


def _hg_direction(q_ref, v_ref, g_ref, st_ref, o_ref, bscr, mask_ref, rev):
    C = HG_CHUNK
    TB = q_ref.shape[1]
    nc = TB // C
    q = q_ref[0].astype(F32)
    v = v_ref[0]
    g = g_ref[0]
    k = 1.0 - jnp.exp(g)
    pos = lax.broadcasted_iota(jnp.int32, (TB, HG_DK), 0) % C

    b = g
    for j in range(N_LEVELS):
        s = 1 << j
        if rev:
            b = b + jnp.where(pos < C - s, pltpu.roll(b, TB - s, axis=0), 0.0)
        else:
            b = b + jnp.where(pos >= s, pltpu.roll(b, s, axis=0), 0.0)
    bscr[...] = b

    qk = [q.astype(BF16)]
    kk = [k.astype(BF16)]
    for j in range(N_LEVELS):
        e = jnp.exp(-jnp.abs(b - _hg_ref_rows(bscr, TB, 1 << j, rev)))
        qk.append((q * e).astype(BF16))
        kk.append((k * e).astype(BF16))
    level_mask = [mask_ref[1 if rev else 0, j] for j in range(N_LEVELS)] + [mask_ref[0, N_LEVELS]]
    level_mask = [level_mask[-1]] + level_mask[:-1]

    qb = (q * jnp.exp(b)).astype(BF16)
    chunks = range(nc - 1, -1, -1) if rev else range(nc)
    for c in chunks:
        rows = slice(c * C, (c + 1) * C)
        last = c * C if rev else c * C + C - 1
        bl = bscr[pl.ds(last, 1), :]
        a = jnp.zeros((C, C), F32)
        for qj, kj, mj in zip(qk, kk, level_mask):
            a = a + _dot_nt(qj[rows], kj[rows]) * mj
        st = st_ref[...]
        o_ref[0, rows, :] = _dot_nt(qb[rows], st.astype(BF16)) + _dot(a.astype(BF16), v[rows])
        kd = (k[rows] * jnp.exp(bl - b[rows])).astype(BF16)
        st_ref[...] = jnp.exp(bl) * st + _dot_tn(v[rows], kd)


def _hgrn_kernel(qf_ref, vf_ref, gf_ref, qb_ref, vb_ref, gb_ref, s0f_ref, s0b_ref, mask_ref,
                 of_ref, ob_ref, stf, stb, bscr):
    @pl.when(pl.program_id(2) == 0)
    def _():
        stf[...] = s0f_ref[0, 0]
        stb[...] = s0b_ref[0, 0]

    _hg_direction(qf_ref, vf_ref, gf_ref, stf, of_ref, bscr, mask_ref, False)
    _hg_direction(qb_ref, vb_ref, gb_ref, stb, ob_ref, bscr, mask_ref, True)


def _hgrn(q_hg, v_hg, g_fw, g_bw, s_fw, s_bw, tb):
    B, T, _ = q_hg.shape
    nb = T // tb
    fwd = pl.BlockSpec((1, tb, HG_DK), lambda b, h, i: (b, i, h))
    bwd = pl.BlockSpec((1, tb, HG_DK), lambda b, h, i: (b, nb - 1 - i, h))
    st = pl.BlockSpec((1, 1, HG_DK, HG_DK), lambda b, h, i: (b, h, 0, 0))
    masks = jnp.asarray(_hg_masks())
    oshape = jax.ShapeDtypeStruct((B, T, HG_WIDTH), F32)
    return pl.pallas_call(
        _hgrn_kernel,
        grid=(B, HG_HEADS, nb),
        in_specs=[fwd, fwd, fwd, bwd, bwd, bwd, st, st,
                  pl.BlockSpec(masks.shape, lambda b, h, i: (0, 0, 0, 0))],
        out_specs=[fwd, bwd],
        out_shape=[oshape, oshape],
        scratch_shapes=[pltpu.VMEM((HG_DK, HG_DK), F32), pltpu.VMEM((HG_DK, HG_DK), F32),
                        pltpu.VMEM((tb, HG_DK), F32)],
        compiler_params=pltpu.CompilerParams(
            dimension_semantics=("parallel", "parallel", "arbitrary"), vmem_limit_bytes=VMEM_LIMIT),
        name="hgrn",
    )(q_hg, v_hg, g_fw, q_hg, v_hg, g_bw, s_fw, s_bw, masks)


def _na_bias(rpb, rows):
    W = GRID_W
    kr = min(NA_KR, rows)
    cols = np.arange(W)
    c0 = np.clip(cols - NA_KC // 2, 0, W - NA_KC)
    wk = np.arange(W)[None, :]
    valid = (wk >= c0[:, None]) & (wk < c0[:, None] + NA_KC)
    dc = np.clip(wk - cols[:, None] + NA_KC - 1, 0, 2 * NA_KC - 2)
    o = np.arange(kr)[:, None]
    j = np.arange(kr)[None, :]
    dr = j - o + NA_KR - 1
    bias = rpb.astype(F32)[:, dr[:, :, None, None], dc[None, None, :, :]]
    bias = jnp.where(valid[None, None, None], bias, NEG)
    bias = bias.transpose(1, 0, 3, 2, 4)
    return bias.reshape(kr, NA_HEADS // NA_GROUP, NA_GROUP * W, kr * W)


def _natten_kernel(q_ref, k_ref, v_ref, kc_ref, vc_ref, bias_ref, o_ref):
    W = GRID_W
    rows = k_ref.shape[1] // W
    kr = min(NA_KR, rows)
    r = pl.program_id(1)
    r0 = jnp.clip(r - kr // 2, 0, rows - kr)
    start = pl.multiple_of(r0 * W, W)
    kw = k_ref[0, pl.ds(start, kr * W), :]
    vw = v_ref[0, pl.ds(start, kr * W), :]
    q = q_ref[0]
    rb = lax.broadcasted_iota(jnp.int32, (NA_GROUP * W, NA_GW), 0) // W
    cb = lax.broadcasted_iota(jnp.int32, (NA_GROUP * W, NA_GW), 1) // NA_HEAD_DIM
    diag = rb == cb
    for grp in range(NA_HEADS // NA_GROUP):
        gs = slice(grp * NA_GW, (grp + 1) * NA_GW)
        qg = q[:, gs]
        qbd = jnp.where(diag, jnp.concatenate([qg] * NA_GROUP, axis=0), jnp.zeros_like(qg[:1]))
        s_win = _dot_nt(qbd, kw[:, gs]) + bias_ref[0, grp]
        s_ctx = _dot_nt(qbd, kc_ref[0, :, gs])
        m = jnp.maximum(jnp.max(s_win, axis=-1, keepdims=True), jnp.max(s_ctx, axis=-1, keepdims=True))
        p_win = jnp.exp(s_win - m)
        p_ctx = jnp.exp(s_ctx - m)
        denom = jnp.sum(p_win, axis=-1, keepdims=True) + jnp.sum(p_ctx, axis=-1, keepdims=True)
        of = (_dot(p_win.astype(BF16), vw[:, gs]) + _dot(p_ctx.astype(BF16), vc_ref[0, :, gs])) / denom
        of = jnp.where(diag, of, 0.0)
        og = of[0:W]
        for h in range(1, NA_GROUP):
            og = og + of[h * W:(h + 1) * W]
        o_ref[0, :, gs] = og.astype(o_ref.dtype)


def _natten(q, k, v, kc, vc, bias):
    B, T, _ = q.shape
    rows = T // GRID_W
    kr = min(NA_KR, rows)
    L = kc.shape[1]
    full = lambda n: pl.BlockSpec((1, n, NA_WIDTH), lambda b, r: (b, 0, 0))
    row = pl.BlockSpec((1, GRID_W, NA_WIDTH), lambda b, r: (b, r, 0))

    def bias_map(b, r):
        return (r - jnp.clip(r - kr // 2, 0, rows - kr), 0, 0, 0)

    return pl.pallas_call(
        _natten_kernel,
        grid=(B, rows),
        in_specs=[row, full(T), full(T), full(L), full(L),
                  pl.BlockSpec((1,) + bias.shape[1:], bias_map)],
        out_specs=row,
        out_shape=jax.ShapeDtypeStruct((B, T, NA_WIDTH), BF16),
        compiler_params=pltpu.CompilerParams(
            dimension_semantics=("parallel", "arbitrary"), vmem_limit_bytes=VMEM_LIMIT),
        name="natten",
    )(q, k, v, kc, vc, bias)


def _out_proj_kernel(yna_ref, of_ref, ob_ref, gate_ref, hgw_ref, w_ref, x_ref, gt_ref, nw_ref, o_ref):
    o = of_ref[0] + ob_ref[0]
    gate = gate_ref[0].astype(F32)
    hgw = hgw_ref[...]
    parts = []
    for h in range(HG_HEADS):
        sl = slice(h * HG_DK, (h + 1) * HG_DK)
        parts.append((_rms(o[:, sl], hgw) * gate[:, sl]).astype(BF16))
    y_hg = jnp.concatenate(parts, axis=-1)
    y = _dot(yna_ref[0], w_ref[:NA_WIDTH, :]) + _dot(y_hg, w_ref[NA_WIDTH:, :])
    o_ref[0] = x_ref[0] + gt_ref[0] * _rms(y, nw_ref[...])


def _out_proj(y_na, o_fw, o_bw, gate, hgw, w_bf, x, gt, nw, tm):
    B, T, _ = x.shape
    tok = lambda n: pl.BlockSpec((1, tm, n), lambda b, i: (b, i, 0))
    const = lambda shape: pl.BlockSpec(shape, lambda b, i: (0,) * len(shape))
    return pl.pallas_call(
        _out_proj_kernel,
        grid=(B, T // tm),
        in_specs=[tok(512), tok(512), tok(512), tok(512), const((1, HG_DK)), const(w_bf.shape),
                  tok(D_MODEL), pl.BlockSpec((1, 1, D_MODEL), lambda b, i: (b, 0, 0)), const((1, D_MODEL))],
        out_specs=tok(D_MODEL),
        out_shape=jax.ShapeDtypeStruct(x.shape, F32),
        compiler_params=pltpu.CompilerParams(
            dimension_semantics=("parallel", "parallel"), vmem_limit_bytes=VMEM_LIMIT),
        name="out_proj",
    )(y_na, o_fw, o_bw, gate, hgw, w_bf, x, gt, nw)


def _ffn_kernel(x_ref, sc_ref, sh_ref, gt_ref, npre_ref, npost_ref, w1_ref, w2_ref, o_ref, act_ref):
    d_ff = w2_ref.shape[0]
    x = x_ref[0]
    hb = (_rms(x, npre_ref[...]) * (1.0 + sc_ref[0]) + sh_ref[0]).astype(BF16)
    step = 256
    for j in range(0, d_ff, step):
        gate = _dot(hb, w1_ref[:, j:j + step])
        up = _dot(hb, w1_ref[:, d_ff + j:d_ff + j + step])
        act_ref[:, j:j + step] = (_silu(gate) * up).astype(BF16)
    z = _dot(act_ref[...], w2_ref[...])
    o_ref[0] = x + gt_ref[0] * _rms(z, npost_ref[...])


def _ffn(x, sc, sh, gt, npre, npost, w1_bf, w2_bf, tm):
    B, T, _ = x.shape
    tok = pl.BlockSpec((1, tm, D_MODEL), lambda b, i: (b, i, 0))
    mod = pl.BlockSpec((1, 1, D_MODEL), lambda b, i: (b, 0, 0))
    const = lambda shape: pl.BlockSpec(shape, lambda b, i: (0,) * len(shape))
    return pl.pallas_call(
        _ffn_kernel,
        grid=(B, T // tm),
        in_specs=[tok, mod, mod, mod, const((1, D_MODEL)), const((1, D_MODEL)),
                  const(w1_bf.shape), const(w2_bf.shape)],
        out_specs=tok,
        out_shape=jax.ShapeDtypeStruct(x.shape, F32),
        scratch_shapes=[pltpu.VMEM((tm, w2_bf.shape[0]), BF16)],
        compiler_params=pltpu.CompilerParams(
            dimension_semantics=("parallel", "parallel"), vmem_limit_bytes=VMEM_LIMIT),
        name="ffn",
    )(x, sc, sh, gt, npre, npost, w1_bf, w2_bf)


def kernel(x, c, ctx, c_ctx, w_ada, b_ada, norm_mix_pre, norm_mix_post, norm_ffn_pre, norm_ffn_post,
           w_in, na_rpb, hg_lb_logits, hg_norm_w, w_out, w_ffn_in, w_ffn_out):
    B, T, D = x.shape
    assert w_ada.shape[0] == 1, "single-layer stack"
    rows = T // GRID_W

    cv = jnp.zeros((8, D), F32).at[:B].set(c).at[B].set(c_ctx)
    mod = _ada(cv, w_ada[0], b_ada[0][None, :])
    sh_m, sc_m, gt_m, sh_f, sc_f, gt_f = [mod[:, i * D:(i + 1) * D] for i in range(N_MOD)]
    lat = lambda m: m[:B, None, :]
    cx = lambda m: jnp.broadcast_to(m[B][None, None, :], (B, 1, D))

    w_in_bf = w_in[0].astype(BF16)
    lbl = hg_lb_logits.reshape(hg_lb_logits.shape[0], 2 * HG_WIDTH)
    nw_pre = norm_mix_pre[0][None, :]

    q_na, k_na, v_na, q_hg, g_fw, g_bw, v_hg, gate = _in_proj(x, lat(sc_m), lat(sh_m), nw_pre, w_in_bf, lbl, 512)
    _, k_c, v_c, _, g_cfw, g_cbw, vhg_c, _ = _in_proj(ctx, cx(sc_m), cx(sh_m), nw_pre, w_in_bf, lbl,
                                                      ctx.shape[1])

    s_fw, s_bw = _ctx_state(g_cfw, g_cbw, vhg_c)
    o_fw, o_bw = _hgrn(q_hg, v_hg, g_fw, g_bw, s_fw, s_bw, 256)

    y_na = _natten(q_na, k_na, v_na, k_c, v_c, _na_bias(na_rpb[0], rows))

    x1 = _out_proj(y_na, o_fw, o_bw, gate, hg_norm_w[0][None, :], w_out[0].astype(BF16), x, lat(gt_m),
                   norm_mix_post[0][None, :], 512)
    return _ffn(x1, lat(sc_f), lat(sh_f), lat(gt_f), norm_ffn_pre[0][None, :], norm_ffn_post[0][None, :],
                w_ffn_in[0].astype(BF16), w_ffn_out[0].astype(BF16), 512)
```

```python
import functools

import jax
import jax.numpy as jnp
import numpy as np
from jax import lax
from jax.experimental import pallas as pl
from jax.experimental.pallas import tpu as pltpu

D_MODEL = 1024
GRID_W = 64
NA_HEADS = 8
NA_HEAD_DIM = 64
NA_WIDTH = NA_HEADS * NA_HEAD_DIM
NA_KR = 8
NA_KC = 16
HG_HEADS = 4
HG_DK = 128
HG_WIDTH = HG_HEADS * HG_DK
HG_CHUNK = 64
N_LEVELS = 6
N_MOD = 6
EPS = 1e-6
NEG = -1e30

NA_GROUP = 4
NA_GW = NA_GROUP * NA_HEAD_DIM

F32 = jnp.float32
BF16 = jnp.bfloat16

VMEM_LIMIT = 56 * 1024 * 1024


def _silu(x):
    return x * jax.nn.sigmoid(x)


def _dot(a, b):
    return jnp.dot(a, b, preferred_element_type=F32)


def _dot_nt(a, b):
    return lax.dot_general(a, b, (((1,), (1,)), ((), ())), preferred_element_type=F32)


def _dot_tn(a, b):
    return lax.dot_general(a, b, (((0,), (0,)), ((), ())), preferred_element_type=F32)


def _split3(x):
    x1 = x.astype(BF16)
    r1 = x - x1.astype(F32)
    x2 = r1.astype(BF16)
    r2 = r1 - x2.astype(F32)
    return x1, x2, r2.astype(BF16)


def _dot_exact_lhs(t, x):
    x1, x2, x3 = _split3(x)
    return _dot(t, x1) + _dot(t, x2) + _dot(t, x3)


def _dot_exact_rhs(x, t):
    x1, x2, x3 = _split3(x)
    return _dot(x1, t) + _dot(x2, t) + _dot(x3, t)


def _rms(x, w):
    return x * lax.rsqrt(jnp.mean(x * x, axis=-1, keepdims=True) + EPS) * w


def _ada_kernel(cv_ref, w_ref, b_ref, o_ref):
    s = _silu(cv_ref[...])
    s1, s2, s3 = _split3(s)
    w1, w2, w3 = _split3(w_ref[...])
    acc = _dot(s1, w1) + (_dot(s1, w2) + _dot(s2, w1)) + (_dot(s1, w3) + _dot(s2, w2) + _dot(s3, w1))
    o_ref[...] = acc + b_ref[...]


def _ada(cv, w_ada, b_ada):
    n = w_ada.shape[1]
    tn = 1536
    return pl.pallas_call(
        _ada_kernel,
        grid=(n // tn,),
        in_specs=[
            pl.BlockSpec((8, D_MODEL), lambda j: (0, 0)),
            pl.BlockSpec((D_MODEL, tn), lambda j: (0, j)),
            pl.BlockSpec((1, tn), lambda j: (0, j)),
        ],
        out_specs=pl.BlockSpec((8, tn), lambda j: (0, j)),
        out_shape=jax.ShapeDtypeStruct((8, n), F32),
        compiler_params=pltpu.CompilerParams(vmem_limit_bytes=VMEM_LIMIT),
        name="ada",
    )(cv, w_ada, b_ada)


def _in_proj_kernel(x_ref, sc_ref, sh_ref, nw_ref, w_ref, lbl_ref,
                    qna_ref, kna_ref, vna_ref, qhg_ref, gfw_ref, gbw_ref, vhg_ref, gate_ref):
    x = x_ref[0]
    h = _rms(x, nw_ref[...]) * (1.0 + sc_ref[0]) + sh_ref[0]
    hb = h.astype(BF16)

    def proj(i):
        return _dot(hb, w_ref[:, i * 512:(i + 1) * 512])

    lbl = lbl_ref[...]
    e = jnp.exp(lbl - jnp.max(lbl, axis=0, keepdims=True))
    lb = e[0:1] / jnp.sum(e, axis=0, keepdims=True)

    qna_ref[0] = (proj(0) * (NA_HEAD_DIM ** -0.5)).astype(BF16)
    kna_ref[0] = proj(1).astype(BF16)
    vna_ref[0] = proj(2).astype(BF16)
    qhg_ref[0] = _silu(proj(3)).astype(BF16)
    lb_f = lb[:, :HG_WIDTH]
    lb_b = lb[:, HG_WIDTH:]
    gfw_ref[0] = jnp.log(lb_f + (1.0 - lb_f) * jax.nn.sigmoid(proj(4)))
    gbw_ref[0] = jnp.log(lb_b + (1.0 - lb_b) * jax.nn.sigmoid(proj(5)))
    vhg_ref[0] = proj(6).astype(BF16)
    gate_ref[0] = _silu(proj(7)).astype(BF16)


def _in_proj(x, sc, sh, nw, w_bf, lbl, tm):
    B, T, _ = x.shape
    tok = lambda b, i: (b, i, 0)
    out_bf = jax.ShapeDtypeStruct((B, T, 512), BF16)
    out_f = jax.ShapeDtypeStruct((B, T, 512), F32)
    ospec = pl.BlockSpec((1, tm, 512), tok)
    return pl.pallas_call(
        _in_proj_kernel,
        grid=(B, T // tm),
        in_specs=[
            pl.BlockSpec((1, tm, D_MODEL), tok),
            pl.BlockSpec((1, 1, D_MODEL), lambda b, i: (b, 0, 0)),
            pl.BlockSpec((1, 1, D_MODEL), lambda b, i: (b, 0, 0)),
            pl.BlockSpec((1, D_MODEL), lambda b, i: (0, 0)),
            pl.BlockSpec(w_bf.shape, lambda b, i: (0, 0)),
            pl.BlockSpec(lbl.shape, lambda b, i: (0, 0)),
        ],
        out_specs=[ospec] * 8,
        out_shape=[out_bf, out_bf, out_bf, out_bf, out_f, out_f, out_bf, out_bf],
        compiler_params=pltpu.CompilerParams(
            dimension_semantics=("parallel", "parallel"), vmem_limit_bytes=VMEM_LIMIT),
        name="in_proj",
    )(x, sc, sh, nw, w_bf, lbl)


def _ctx_state_kernel(gfw_ref, gbw_ref, v_ref, sfw_ref, sbw_ref):
    L = gfw_ref.shape[1]
    r = lax.broadcasted_iota(jnp.int32, (L, L), 0)
    c = lax.broadcasted_iota(jnp.int32, (L, L), 1)
    upper = jnp.where(c > r, 1.0, 0.0).astype(BF16)
    lower = jnp.where(c < r, 1.0, 0.0).astype(BF16)
    v = v_ref[0]
    for g_ref, tri, s_ref in ((gfw_ref, upper, sfw_ref), (gbw_ref, lower, sbw_ref)):
        g = g_ref[0]
        kw = ((1.0 - jnp.exp(g)) * jnp.exp(_dot_exact_lhs(tri, g))).astype(BF16)
        for h in range(HG_HEADS):
            sl = slice(h * HG_DK, (h + 1) * HG_DK)
            s_ref[0, h] = _dot_tn(v[:, sl], kw[:, sl])


def _ctx_state(g_cfw, g_cbw, v_c):
    B, L, _ = g_cfw.shape
    tok = pl.BlockSpec((1, L, HG_WIDTH), lambda b: (b, 0, 0))
    st = pl.BlockSpec((1, HG_HEADS, HG_DK, HG_DK), lambda b: (b, 0, 0, 0))
    sshape = jax.ShapeDtypeStruct((B, HG_HEADS, HG_DK, HG_DK), F32)
    return pl.pallas_call(
        _ctx_state_kernel,
        grid=(B,),
        in_specs=[tok, tok, tok],
        out_specs=[st, st],
        out_shape=[sshape, sshape],
        compiler_params=pltpu.CompilerParams(vmem_limit_bytes=VMEM_LIMIT),
        name="ctx_state",
    )(g_cfw, g_cbw, v_c)


def _hg_masks():
    C = HG_CHUNK
    t = np.arange(C)[:, None]
    s = np.arange(C)[None, :]
    fw = []
    for j in range(N_LEVELS):
        hs = 1 << j
        same = (t // (2 * hs)) == (s // (2 * hs))
        fw.append(same & ((t % (2 * hs)) >= hs) & ((s % (2 * hs)) < hs))
    fw.append(t == s)
    fw = np.stack(fw).astype(np.float32)
    return np.stack([fw, fw.transpose(0, 2, 1)])


def _hg_ref_rows(bscr, n_rows, hs, rev):
    blk = 2 * hs
    off = hs if rev else hs - 1
    bcast = lambda m, n: jnp.broadcast_to(bscr[pl.ds(m, 1), :], (n, HG_DK))
    if blk >= 8:
        return jnp.concatenate([bcast(s + off, blk) for s in range(0, n_rows, blk)], axis=0)
    sub = lax.broadcasted_iota(jnp.int32, (8, HG_DK), 0)
    pieces = []
    for s in range(0, n_rows, 8):
        piece = bcast(s + off, 8)
        for a in range(1, 8 // blk):
            piece = jnp.where(sub >= a * blk, bcast(s + a * blk + off, 8), piece)
        pieces.append(piece)
    return jnp.concatenate(pieces, axis=0)


def _hg_direction(q_ref, v_ref, g_ref, st_ref, o_ref, bscr, mask_ref, rev):
    C = HG_CHUNK
    TB = q_ref.shape[1]
    nc = TB // C
    q = q_ref[0].astype(F32)
    v = v_ref[0]
    g = g_ref[0]
    k = 1.0 - jnp.exp(g)
    pos = lax.broadcasted_iota(jnp.int32, (TB, HG_DK), 0) % C

    b = g
    for j in range(N_LEVELS):
        s = 1 << j
        if rev:
            b = b + jnp.where(pos < C - s, pltpu.roll(b, TB - s, axis=0), 0.0)
        else:
            b = b + jnp.where(pos >= s, pltpu.roll(b, s, axis=0), 0.0)
    bscr[...] = b

    qk = [q.astype(BF16)]
    kk = [k.astype(BF16)]
    for j in range(N_LEVELS):
        e = jnp.exp(-jnp.abs(b - _hg_ref_rows(bscr, TB, 1 << j, rev)))
        qk.append((q * e).astype(BF16))
        kk.append((k * e).astype(BF16))
    level_mask = [mask_ref[1 if rev else 0, j] for j in range(N_LEVELS)] + [mask_ref[0, N_LEVELS]]
    level_mask = [level_mask[-1]] + level_mask[:-1]

    qb = (q * jnp.exp(b)).astype(BF16)
    chunks = range(nc - 1, -1, -1) if rev else range(nc)
    for c in chunks:
        rows = slice(c * C, (c + 1) * C)
        last = c * C if rev else c * C + C - 1
        bl = bscr[pl.ds(last, 1), :]
        a = jnp.zeros((C, C), F32)
        for qj, kj, mj in zip(qk, kk, level_mask):
            a = a + _dot_nt(qj[rows], kj[rows]) * mj
        st = st_ref[...]
        o_ref[0, rows, :] = _dot_nt(qb[rows], st.astype(BF16)) + _dot(a.astype(BF16), v[rows])
        kd = (k[rows] * jnp.exp(bl - b[rows])).astype(BF16)
        st_ref[...] = jnp.exp(bl) * st + _dot_tn(v[rows], kd)


def _hgrn_kernel(qf_ref, vf_ref, gf_ref, qb_ref, vb_ref, gb_ref, s0f_ref, s0b_ref, mask_ref,
                 of_ref, ob_ref, stf, stb, bscr):
    @pl.when(pl.program_id(2) == 0)
    def _():
        stf[...] = s0f_ref[0, 0]
        stb[...] = s0b_ref[0, 0]

    _hg_direction(qf_ref, vf_ref, gf_ref, stf, of_ref, bscr, mask_ref, False)
    _hg_direction(qb_ref, vb_ref, gb_ref, stb, ob_ref, bscr, mask_ref, True)


def _hgrn(q_hg, v_hg, g_fw, g_bw, s_fw, s_bw, tb):
    B, T, _ = q_hg.shape
    nb = T // tb
    fwd = pl.BlockSpec((1, tb, HG_DK), lambda b, h, i: (b, i, h))
    bwd = pl.BlockSpec((1, tb, HG_DK), lambda b, h, i: (b, nb - 1 - i, h))
    st = pl.BlockSpec((1, 1, HG_DK, HG_DK), lambda b, h, i: (b, h, 0, 0))
    masks = jnp.asarray(_hg_masks())
    oshape = jax.ShapeDtypeStruct((B, T, HG_WIDTH), F32)
    return pl.pallas_call(
        _hgrn_kernel,
        grid=(B, HG_HEADS, nb),
        in_specs=[fwd, fwd, fwd, bwd, bwd, bwd, st, st,
                  pl.BlockSpec(masks.shape, lambda b, h, i: (0, 0, 0, 0))],
        out_specs=[fwd, bwd],
        out_shape=[oshape, oshape],
        scratch_shapes=[pltpu.VMEM((HG_DK, HG_DK), F32), pltpu.VMEM((HG_DK, HG_DK), F32),
                        pltpu.VMEM((tb, HG_DK), F32)],
        compiler_params=pltpu.CompilerParams(
            dimension_semantics=("parallel", "parallel", "arbitrary"), vmem_limit_bytes=VMEM_LIMIT),
        name="hgrn",
    )(q_hg, v_hg, g_fw, q_hg, v_hg, g_bw, s_fw, s_bw, masks)


def _na_bias_consts():
    W = GRID_W
    cols = np.arange(W)
    c0 = np.clip(cols - NA_KC // 2, 0, W - NA_KC)
    wk = np.arange(W)[None, :]
    valid = (wk >= c0[:, None]) & (wk < c0[:, None] + NA_KC)
    dc = wk - cols[:, None] + NA_KC - 1
    onehot = (dc[None] == np.arange(2 * NA_KC)[:, None, None]) & valid[None]
    neg = np.where(valid, 0.0, NEG).astype(np.float32).reshape(1, W * W)
    return onehot.reshape(2 * NA_KC, W * W).astype(np.float32), neg


def _na_bias_kernel(rpb_ref, onehot_ref, neg_ref, o_ref):
    o_ref[...] = _dot_exact_rhs(rpb_ref[...], onehot_ref[...]) + neg_ref[...]


def _na_bias(rpb, rows):
    W = GRID_W
    kr = min(NA_KR, rows)
    n_dr = 2 * NA_KR - 1
    onehot, neg = _na_bias_consts()
    rpb2 = jnp.zeros((128, 2 * NA_KC), F32).at[:NA_HEADS * n_dr, :2 * NA_KC - 1].set(
        rpb.astype(F32).reshape(NA_HEADS * n_dr, 2 * NA_KC - 1))
    toep = pl.pallas_call(
        _na_bias_kernel,
        out_shape=jax.ShapeDtypeStruct((128, W * W), F32),
        name="na_bias",
    )(rpb2, jnp.asarray(onehot, dtype=BF16), jnp.asarray(neg))
    toep = toep[:NA_HEADS * n_dr].reshape(NA_HEADS, n_dr, W, W)
    per_o = [toep[:, NA_KR - 1 - o:NA_KR - 1 - o + kr] for o in range(kr)]
    bias = jnp.stack(per_o).transpose(0, 1, 3, 2, 4)
    return bias.reshape(kr, NA_HEADS // NA_GROUP, NA_GROUP * W, kr * W)


def _natten_kernel(q_ref, k_ref, v_ref, kc_ref, vc_ref, bias_ref, o_ref):
    W = GRID_W
    rows = k_ref.shape[1] // W
    kr = min(NA_KR, rows)
    r = pl.program_id(1)
    r0 = jnp.clip(r - kr // 2, 0, rows - kr)
    start = pl.multiple_of(r0 * W, W)
    kw = k_ref[0, pl.ds(start, kr * W), :]
    vw = v_ref[0, pl.ds(start, kr * W), :]
    q = q_ref[0]
    rb = lax.broadcasted_iota(jnp.int32, (NA_GROUP * W, NA_GW), 0) // W
    cb = lax.broadcasted_iota(jnp.int32, (NA_GROUP * W, NA_GW), 1) // NA_HEAD_DIM
    diag = rb == cb
    for grp in range(NA_HEADS // NA_GROUP):
        gs = slice(grp * NA_GW, (grp + 1) * NA_GW)
        qg = q[:, gs]
        qbd = jnp.where(diag, jnp.concatenate([qg] * NA_GROUP, axis=0), jnp.zeros_like(qg[:1]))
        s_win = _dot_nt(qbd, kw[:, gs]) + bias_ref[0, grp]
        s_ctx = _dot_nt(qbd, kc_ref[0, :, gs])
        m = jnp.maximum(jnp.max(s_win, axis=-1, keepdims=True), jnp.max(s_ctx, axis=-1, keepdims=True))
        p_win = jnp.exp(s_win - m)
        p_ctx = jnp.exp(s_ctx - m)
        denom = jnp.sum(p_win, axis=-1, keepdims=True) + jnp.sum(p_ctx, axis=-1, keepdims=True)
        of = (_dot(p_win.astype(BF16), vw[:, gs]) + _dot(p_ctx.astype(BF16), vc_ref[0, :, gs])) / denom
        of = jnp.where(diag, of, 0.0)
        og = of[0:W]
        for h in range(1, NA_GROUP):
            og = og + of[h * W:(h + 1) * W]
        o_ref[0, :, gs] = og.astype(o_ref.dtype)


def _natten(q, k, v, kc, vc, bias):
    B, T, _ = q.shape
    rows = T // GRID_W
    kr = min(NA_KR, rows)
    L = kc.shape[1]
    full = lambda n: pl.BlockSpec((1, n, NA_WIDTH), lambda b, r: (b, 0, 0))
    row = pl.BlockSpec((1, GRID_W, NA_WIDTH), lambda b, r: (b, r, 0))

    def bias_map(b, r):
        return (r - jnp.clip(r - kr // 2, 0, rows - kr), 0, 0, 0)

    return pl.pallas_call(
        _natten_kernel,
        grid=(B, rows),
        in_specs=[row, full(T), full(T), full(L), full(L),
                  pl.BlockSpec((1,) + bias.shape[1:], bias_map)],
        out_specs=row,
        out_shape=jax.ShapeDtypeStruct((B, T, NA_WIDTH), BF16),
        compiler_params=pltpu.CompilerParams(
            dimension_semantics=("parallel", "arbitrary"), vmem_limit_bytes=VMEM_LIMIT),
        name="natten",
    )(q, k, v, kc, vc, bias)


def _out_proj_kernel(yna_ref, of_ref, ob_ref, gate_ref, hgw_ref, w_ref, x_ref, gt_ref, nw_ref, o_ref):
    o = of_ref[0] + ob_ref[0]
    gate = gate_ref[0].astype(F32)
    hgw = hgw_ref[...]
    parts = []
    for h in range(HG_HEADS):
        sl = slice(h * HG_DK, (h + 1) * HG_DK)
        parts.append((_rms(o[:, sl], hgw) * gate[:, sl]).astype(BF16))
    y_hg = jnp.concatenate(parts, axis=-1)
    y = _dot(yna_ref[0], w_ref[:NA_WIDTH, :]) + _dot(y_hg, w_ref[NA_WIDTH:, :])
    o_ref[0] = x_ref[0] + gt_ref[0] * _rms(y, nw_ref[...])


def _out_proj(y_na, o_fw, o_bw, gate, hgw, w_bf, x, gt, nw, tm):
    B, T, _ = x.shape
    tok = lambda n: pl.BlockSpec((1, tm, n), lambda b, i: (b, i, 0))
    const = lambda shape: pl.BlockSpec(shape, lambda b, i: (0,) * len(shape))
    return pl.pallas_call(
        _out_proj_kernel,
        grid=(B, T // tm),
        in_specs=[tok(512), tok(512), tok(512), tok(512), const((1, HG_DK)), const(w_bf.shape),
                  tok(D_MODEL), pl.BlockSpec((1, 1, D_MODEL), lambda b, i: (b, 0, 0)), const((1, D_MODEL))],
        out_specs=tok(D_MODEL),
        out_shape=jax.ShapeDtypeStruct(x.shape, F32),
        compiler_params=pltpu.CompilerParams(
            dimension_semantics=("parallel", "parallel"), vmem_limit_bytes=VMEM_LIMIT),
        name="out_proj",
    )(y_na, o_fw, o_bw, gate, hgw, w_bf, x, gt, nw)


def _ffn_kernel(x_ref, sc_ref, sh_ref, gt_ref, npre_ref, npost_ref, w1_ref, w2_ref, o_ref, act_ref):
    d_ff = w2_ref.shape[0]
    x = x_ref[0]
    hb = (_rms(x, npre_ref[...]) * (1.0 + sc_ref[0]) + sh_ref[0]).astype(BF16)
    step = 256
    for j in range(0, d_ff, step):
        gate = _dot(hb, w1_ref[:, j:j + step])
        up = _dot(hb, w1_ref[:, d_ff + j:d_ff + j + step])
        act_ref[:, j:j + step] = (_silu(gate) * up).astype(BF16)
    z = _dot(act_ref[...], w2_ref[...])
    o_ref[0] = x + gt_ref[0] * _rms(z, npost_ref[...])


def _ffn(x, sc, sh, gt, npre, npost, w1_bf, w2_bf, tm):
    B, T, _ = x.shape
    tok = pl.BlockSpec((1, tm, D_MODEL), lambda b, i: (b, i, 0))
    mod = pl.BlockSpec((1, 1, D_MODEL), lambda b, i: (b, 0, 0))
    const = lambda shape: pl.BlockSpec(shape, lambda b, i: (0,) * len(shape))
    return pl.pallas_call(
        _ffn_kernel,
        grid=(B, T // tm),
        in_specs=[tok, mod, mod, mod, const((1, D_MODEL)), const((1, D_MODEL)),
                  const(w1_bf.shape), const(w2_bf.shape)],
        out_specs=tok,
        out_shape=jax.ShapeDtypeStruct(x.shape, F32),
        scratch_shapes=[pltpu.VMEM((tm, w2_bf.shape[0]), BF16)],
        compiler_params=pltpu.CompilerParams(
            dimension_semantics=("parallel", "parallel"), vmem_limit_bytes=VMEM_LIMIT),
        name="ffn",
    )(x, sc, sh, gt, npre, npost, w1_bf, w2_bf)


def kernel(x, c, ctx, c_ctx, w_ada, b_ada, norm_mix_pre, norm_mix_post, norm_ffn_pre, norm_ffn_post,
           w_in, na_rpb, hg_lb_logits, hg_norm_w, w_out, w_ffn_in, w_ffn_out):
    B, T, D = x.shape
    assert w_ada.shape[0] == 1, "single-layer stack"
    rows = T // GRID_W

    cv = jnp.zeros((8, D), F32).at[:B].set(c).at[B].set(c_ctx)
    mod = _ada(cv, w_ada[0], b_ada[0][None, :])
    sh_m, sc_m, gt_m, sh_f, sc_f, gt_f = [mod[:, i * D:(i + 1) * D] for i in range(N_MOD)]
    lat = lambda m: m[:B, None, :]
    cx = lambda m: jnp.broadcast_to(m[B][None, None, :], (B, 1, D))

    w_in_bf = w_in[0].astype(BF16)
    lbl = hg_lb_logits.reshape(hg_lb_logits.shape[0], 2 * HG_WIDTH)
    nw_pre = norm_mix_pre[0][None, :]

    q_na, k_na, v_na, q_hg, g_fw, g_bw, v_hg, gate = _in_proj(x, lat(sc_m), lat(sh_m), nw_pre, w_in_bf, lbl, 512)
    _, k_c, v_c, _, g_cfw, g_cbw, vhg_c, _ = _in_proj(ctx, cx(sc_m), cx(sh_m), nw_pre, w_in_bf, lbl,
                                                      ctx.shape[1])

    s_fw, s_bw = _ctx_state(g_cfw, g_cbw, vhg_c)
    o_fw, o_bw = _hgrn(q_hg, v_hg, g_fw, g_bw, s_fw, s_bw, 256)

    y_na = _natten(q_na, k_na, v_na, k_c, v_c, _na_bias(na_rpb[0], rows))

    x1 = _out_proj(y_na, o_fw, o_bw, gate, hg_norm_w[0][None, :], w_out[0].astype(BF16), x, lat(gt_m),
                   norm_mix_post[0][None, :], 512)
    return _ffn(x1, lat(sc_f), lat(sh_f), lat(gt_f), norm_ffn_pre[0][None, :], norm_ffn_post[0][None, :],
                w_ffn_in[0].astype(BF16), w_ffn_out[0].astype(BF16), 512)
```

```python
import functools

import jax
import jax.numpy as jnp
import numpy as np
from jax import lax
from jax.experimental import pallas as pl
from jax.experimental.pallas import tpu as pltpu

D_MODEL = 1024
GRID_W = 64
NA_HEADS = 8
NA_HEAD_DIM = 64
NA_WIDTH = NA_HEADS * NA_HEAD_DIM
NA_KR = 8
NA_KC = 16
HG_HEADS = 4
HG_DK = 128
HG_WIDTH = HG_HEADS * HG_DK
HG_CHUNK = 64
N_LEVELS = 6
N_MOD = 6
EPS = 1e-6
NEG = -1e30

NA_GROUP = 4
NA_GW = NA_GROUP * NA_HEAD_DIM

F32 = jnp.float32
BF16 = jnp.bfloat16

VMEM_LIMIT = 56 * 1024 * 1024


def _silu(x):
    return x * jax.nn.sigmoid(x)


def _dot(a, b):
    return jnp.dot(a, b, preferred_element_type=F32)


def _dot_nt(a, b):
    return lax.dot_general(a, b, (((1,), (1,)), ((), ())), preferred_element_type=F32)


def _dot_tn(a, b):
    return lax.dot_general(a, b, (((0,), (0,)), ((), ())), preferred_element_type=F32)


def _split3(x):
    x1 = x.astype(BF16)
    r1 = x - x1.astype(F32)
    x2 = r1.astype(BF16)
    r2 = r1 - x2.astype(F32)
    return x1, x2, r2.astype(BF16)


def _dot_exact_lhs(t, x):
    x1, x2, x3 = _split3(x)
    return _dot(t, x1) + _dot(t, x2) + _dot(t, x3)


def _dot_exact_rhs(x, t):
    x1, x2, x3 = _split3(x)
    return _dot(x1, t) + _dot(x2, t) + _dot(x3, t)


def _rms(x, w):
    return x * lax.rsqrt(jnp.mean(x * x, axis=-1, keepdims=True) + EPS) * w


def _ada_kernel(cv_ref, w_ref, b_ref, o_ref):
    s = _silu(cv_ref[...])
    s1, s2, s3 = _split3(s)
    w1, w2, w3 = _split3(w_ref[...])
    acc = _dot(s1, w1) + (_dot(s1, w2) + _dot(s2, w1)) + (_dot(s1, w3) + _dot(s2, w2) + _dot(s3, w1))
    o_ref[...] = acc + b_ref[...]


def _ada(cv, w_ada, b_ada):
    n = w_ada.shape[1]
    tn = 1536
    return pl.pallas_call(
        _ada_kernel,
        grid=(n // tn,),
        in_specs=[
            pl.BlockSpec((8, D_MODEL), lambda j: (0, 0)),
            pl.BlockSpec((D_MODEL, tn), lambda j: (0, j)),
            pl.BlockSpec((1, tn), lambda j: (0, j)),
        ],
        out_specs=pl.BlockSpec((8, tn), lambda j: (0, j)),
        out_shape=jax.ShapeDtypeStruct((8, n), F32),
        compiler_params=pltpu.CompilerParams(vmem_limit_bytes=VMEM_LIMIT),
        name="ada",
    )(cv, w_ada, b_ada)


def _in_proj_kernel(x_ref, sc_ref, sh_ref, nw_ref, w_ref, lbl_ref,
                    qna_ref, kna_ref, vna_ref, qhg_ref, gfw_ref, gbw_ref, vhg_ref, gate_ref):
    x = x_ref[0]
    h = _rms(x, nw_ref[...]) * (1.0 + sc_ref[0]) + sh_ref[0]
    hb = h.astype(BF16)

    def proj(i):
        return _dot(hb, w_ref[:, i * 512:(i + 1) * 512])

    lbl = lbl_ref[...]
    e = jnp.exp(lbl - jnp.max(lbl, axis=0, keepdims=True))
    lb = e[0:1] / jnp.sum(e, axis=0, keepdims=True)

    qna_ref[0] = (proj(0) * (NA_HEAD_DIM ** -0.5)).astype(BF16)
    kna_ref[0] = proj(1).astype(BF16)
    vna_ref[0] = proj(2).astype(BF16)
    qhg_ref[0] = _silu(proj(3)).astype(BF16)
    lb_f = lb[:, :HG_WIDTH]
    lb_b = lb[:, HG_WIDTH:]
    gfw_ref[0] = jnp.log(lb_f + (1.0 - lb_f) * jax.nn.sigmoid(proj(4)))
    gbw_ref[0] = jnp.log(lb_b + (1.0 - lb_b) * jax.nn.sigmoid(proj(5)))
    vhg_ref[0] = proj(6).astype(BF16)
    gate_ref[0] = _silu(proj(7)).astype(BF16)


def _in_proj(x, sc, sh, nw, w_bf, lbl, tm):
    B, T, _ = x.shape
    tok = lambda b, i: (b, i, 0)
    out_bf = jax.ShapeDtypeStruct((B, T, 512), BF16)
    out_f = jax.ShapeDtypeStruct((B, T, 512), F32)
    ospec = pl.BlockSpec((1, tm, 512), tok)
    return pl.pallas_call(
        _in_proj_kernel,
        grid=(B, T // tm),
        in_specs=[
            pl.BlockSpec((1, tm, D_MODEL), tok),
            pl.BlockSpec((1, 1, D_MODEL), lambda b, i: (b, 0, 0)),
            pl.BlockSpec((1, 1, D_MODEL), lambda b, i: (b, 0, 0)),
            pl.BlockSpec((1, D_MODEL), lambda b, i: (0, 0)),
            pl.BlockSpec(w_bf.shape, lambda b, i: (0, 0)),
            pl.BlockSpec(lbl.shape, lambda b, i: (0, 0)),
        ],
        out_specs=[ospec] * 8,
        out_shape=[out_bf, out_bf, out_bf, out_bf, out_f, out_f, out_bf, out_bf],
        compiler_params=pltpu.CompilerParams(
            dimension_semantics=("parallel", "parallel"), vmem_limit_bytes=VMEM_LIMIT),
        name="in_proj",
    )(x, sc, sh, nw, w_bf, lbl)


def _ctx_state_kernel(gfw_ref, gbw_ref, v_ref, sfw_ref, sbw_ref):
    L = gfw_ref.shape[1]
    r = lax.broadcasted_iota(jnp.int32, (L, L), 0)
    c = lax.broadcasted_iota(jnp.int32, (L, L), 1)
    upper = jnp.where(c > r, 1.0, 0.0).astype(BF16)
    lower = jnp.where(c < r, 1.0, 0.0).astype(BF16)
    v = v_ref[0]
    for g_ref, tri, s_ref in ((gfw_ref, upper, sfw_ref), (gbw_ref, lower, sbw_ref)):
        g = g_ref[0]
        kw = ((1.0 - jnp.exp(g)) * jnp.exp(_dot_exact_lhs(tri, g))).astype(BF16)
        for h in range(HG_HEADS):
            sl = slice(h * HG_DK, (h + 1) * HG_DK)
            s_ref[0, h] = _dot_tn(kw[:, sl], v[:, sl])


def _ctx_state(g_cfw, g_cbw, v_c):
    B, L, _ = g_cfw.shape
    tok = pl.BlockSpec((1, L, HG_WIDTH), lambda b: (b, 0, 0))
    st = pl.BlockSpec((1, HG_HEADS, HG_DK, HG_DK), lambda b: (b, 0, 0, 0))
    sshape = jax.ShapeDtypeStruct((B, HG_HEADS, HG_DK, HG_DK), F32)
    return pl.pallas_call(
        _ctx_state_kernel,
        grid=(B,),
        in_specs=[tok, tok, tok],
        out_specs=[st, st],
        out_shape=[sshape, sshape],
        compiler_params=pltpu.CompilerParams(vmem_limit_bytes=VMEM_LIMIT),
        name="ctx_state",
    )(g_cfw, g_cbw, v_c)


SUBLANES = 8
MXU_LEVELS = [j for j in range(1, N_LEVELS) if (1 << j) < SUBLANES]
N_SEG = len(MXU_LEVELS) + 1


def _hg_consts():
    C = HG_CHUNK
    t = np.arange(C)[:, None]
    s = np.arange(C)[None, :]
    masks, sel = [], []
    for j in range(N_LEVELS):
        hs = 1 << j
        same = (t // (2 * hs)) == (s // (2 * hs))
        upper_t = (t % (2 * hs)) >= hs
        m = (t // (2 * hs)) * (2 * hs) + hs - 1
        masks.append(same & upper_t & ((s % (2 * hs)) < hs))
        if j in MXU_LEVELS:
            sel.append(np.where(upper_t, (s > m) & (s <= t), (s > t) & (s <= m)))
    masks.append(t == s)
    sel.append(s <= t)
    masks = np.stack(masks).astype(np.float32)
    sel = np.stack(sel).astype(np.float32)
    both = lambda a: np.stack([a, a[:, ::-1, ::-1]])
    sel = both(sel).reshape(2, N_SEG * C, C)
    return both(masks), np.tile(sel, (1, 1, 3))


class _HgDirection:
    def __init__(self, q_ref, v_ref, g_ref, st_ref, o_ref, sel_ref, mask_ref, rev):
        C = HG_CHUNK
        self.q_ref, self.v_ref, self.st_ref, self.o_ref, self.rev = q_ref, v_ref, st_ref, o_ref, rev
        d = 1 if rev else 0
        self.sel = sel_ref.at[d]
        g = g_ref[0]
        self.g_parts = _split3(g)
        self.f = jnp.exp(g)
        self.k = 1.0 - self.f
        self.level_mask = [mask_ref[d, j] for j in range(N_LEVELS + 1)]
        parity = lax.broadcasted_iota(jnp.int32, (C, HG_DK), 0) % 2
        self.pair_query = parity == (0 if rev else 1)
        n_pairs = q_ref.shape[1] // (2 * C)
        self.pairs = list(range(n_pairs - 1, -1, -1) if rev else range(n_pairs))
        self.st = st_ref[...]

    def decays(self, p):
        C = HG_CHUNK
        g_cat = jnp.concatenate(
            [jnp.concatenate([part[2 * p * C:(2 * p + 1) * C], part[(2 * p + 1) * C:(2 * p + 2) * C]], axis=1)
             for part in self.g_parts], axis=0)
        sums = _dot(self.sel[...], g_cat)
        parts = [sums[i * C:(i + 1) * C] for i in range(N_SEG - 1)]
        b = sums[(N_SEG - 1) * C:N_SEG * C]
        rows = {}

        def row(m):
            if m not in rows:
                rows[m] = jnp.broadcast_to(b[m:m + 1, :], (SUBLANES, 2 * HG_DK))
            return rows[m]

        for j in range(MXU_LEVELS[-1] + 1, N_LEVELS):
            hs = 1 << j
            groups = []
            for r in range(0, C, SUBLANES):
                start = r // (2 * hs) * (2 * hs)
                b_mid = row(start + (hs if self.rev else hs - 1))
                query_side = (r - start >= hs) != self.rev
                groups.append(b[r:r + SUBLANES] - b_mid if query_side else b_mid - b[r:r + SUBLANES])
            parts.append(jnp.concatenate(groups, axis=0))
        parts.append(b)
        parts.append(jnp.concatenate([row(0 if self.rev else C - 1)] * (C // SUBLANES), axis=0) - b)
        return [jnp.exp(x) for x in parts]

    def scores(self, p, e2):
        C = HG_CHUNK
        out = []
        for c in ((2 * p + 1, 2 * p) if self.rev else (2 * p, 2 * p + 1)):
            rows = slice(c * C, (c + 1) * C)
            lanes = slice((c - 2 * p) * HG_DK, (c - 2 * p + 1) * HG_DK)
            qc = self.q_ref[0, rows, :].astype(F32)
            kc = self.k[rows]
            e0 = jnp.where(self.pair_query, self.f[rows], 1.0)
            score = lambda qf, kf: _dot_nt(qf.astype(BF16), kf.astype(BF16))
            a = score(qc, kc) * self.level_mask[N_LEVELS]
            a = a + score(qc * e0, kc * e0) * self.level_mask[0]
            for j in range(1, N_LEVELS):
                ej = e2[j - 1][:, lanes]
                a = a + score(qc * ej, kc * ej) * self.level_mask[j]
            e_b = e2[N_LEVELS - 1][:, lanes]
            total = e_b[0:1] if self.rev else e_b[C - 1:C]
            q_dec = jnp.concatenate([(qc * e_b).astype(BF16), a.astype(BF16)], axis=1)
            out.append((rows, q_dec, (kc * e2[N_LEVELS][:, lanes]).astype(BF16), total))
        return out

    def outputs(self, chunk_terms):
        for rows, q_dec, k_dec, total in chunk_terms:
            vc = self.v_ref[0, rows, :]
            self.o_ref[0, rows, :] = _dot(q_dec, jnp.concatenate([self.st.astype(BF16), vc], axis=0))
            total_col = jnp.transpose(jnp.broadcast_to(total, (SUBLANES, HG_DK)))[:, 0:1]
            self.st = total_col * self.st + _dot_tn(k_dec, vc)

    def finish(self):
        self.st_ref[...] = self.st


def _hgrn_kernel(qf_ref, vf_ref, gf_ref, qb_ref, vb_ref, gb_ref, s0f_ref, s0b_ref, sel_ref, mask_ref,
                 of_ref, ob_ref, stf, stb):
    @pl.when(pl.program_id(2) == 0)
    def _():
        stf[...] = s0f_ref[0, 0]
        stb[...] = s0b_ref[0, 0]

    fw = _HgDirection(qf_ref, vf_ref, gf_ref, stf, of_ref, sel_ref, mask_ref, False)
    bw = _HgDirection(qb_ref, vb_ref, gb_ref, stb, ob_ref, sel_ref, mask_ref, True)
    units = [(dirn, p) for pf, pb in zip(fw.pairs, bw.pairs) for dirn, p in ((fw, pf), (bw, pb))]
    e_next = units[0][0].decays(units[0][1])
    pending = None
    for i, (dirn, p) in enumerate(units):
        e_cur = e_next
        if i + 1 < len(units):
            e_next = units[i + 1][0].decays(units[i + 1][1])
        terms = dirn.scores(p, e_cur)
        if pending is not None:
            pending[0].outputs(pending[1])
        pending = (dirn, terms)
    pending[0].outputs(pending[1])
    fw.finish()
    bw.finish()


def _hgrn(q_hg, v_hg, g_fw, g_bw, s_fw, s_bw, tb):
    B, T, _ = q_hg.shape
    nb = T // tb
    fwd = pl.BlockSpec((1, tb, HG_DK), lambda b, h, i: (b, i, h))
    bwd = pl.BlockSpec((1, tb, HG_DK), lambda b, h, i: (b, nb - 1 - i, h))
    st = pl.BlockSpec((1, 1, HG_DK, HG_DK), lambda b, h, i: (b, h, 0, 0))
    masks, sel = _hg_consts()
    masks, sel = jnp.asarray(masks), jnp.asarray(sel, dtype=BF16)
    oshape = jax.ShapeDtypeStruct((B, T, HG_WIDTH), F32)
    return pl.pallas_call(
        _hgrn_kernel,
        grid=(B, HG_HEADS, nb),
        in_specs=[fwd, fwd, fwd, bwd, bwd, bwd, st, st,
                  pl.BlockSpec(sel.shape, lambda b, h, i: (0, 0, 0)),
                  pl.BlockSpec(masks.shape, lambda b, h, i: (0, 0, 0, 0))],
        out_specs=[fwd, bwd],
        out_shape=[oshape, oshape],
        scratch_shapes=[pltpu.VMEM((HG_DK, HG_DK), F32), pltpu.VMEM((HG_DK, HG_DK), F32)],
        compiler_params=pltpu.CompilerParams(
            dimension_semantics=("parallel", "parallel", "arbitrary"), vmem_limit_bytes=VMEM_LIMIT),
        name="hgrn",
    )(q_hg, v_hg, g_fw, q_hg, v_hg, g_bw, s_fw, s_bw, sel, masks)


def _na_bias_consts():
    W = GRID_W
    cols = np.arange(W)
    c0 = np.clip(cols - NA_KC // 2, 0, W - NA_KC)
    wk = np.arange(W)[None, :]
    valid = (wk >= c0[:, None]) & (wk < c0[:, None] + NA_KC)
    dc = wk - cols[:, None] + NA_KC - 1
    onehot = (dc[None] == np.arange(2 * NA_KC)[:, None, None]) & valid[None]
    neg = np.where(valid, 0.0, NEG).astype(np.float32).reshape(1, W * W)
    return onehot.reshape(2 * NA_KC, W * W).astype(np.float32), neg


def _na_bias_kernel(rpb_ref, onehot_ref, neg_ref, o_ref):
    o_ref[...] = _dot_exact_rhs(rpb_ref[...], onehot_ref[...]) + neg_ref[...]


def _na_bias(rpb, rows):
    W = GRID_W
    kr = min(NA_KR, rows)
    n_dr = 2 * NA_KR - 1
    onehot, neg = _na_bias_consts()
    rpb2 = jnp.zeros((128, 2 * NA_KC), F32).at[:NA_HEADS * n_dr, :2 * NA_KC - 1].set(
        rpb.astype(F32).reshape(NA_HEADS * n_dr, 2 * NA_KC - 1))
    toep = pl.pallas_call(
        _na_bias_kernel,
        out_shape=jax.ShapeDtypeStruct((128, W * W), F32),
        name="na_bias",
    )(rpb2, jnp.asarray(onehot, dtype=BF16), jnp.asarray(neg))
    toep = toep[:NA_HEADS * n_dr].reshape(NA_HEADS, n_dr, W, W)
    per_o = [toep[:, NA_KR - 1 - o:NA_KR - 1 - o + kr] for o in range(kr)]
    bias = jnp.stack(per_o).transpose(0, 1, 3, 2, 4)
    return bias.reshape(kr, NA_HEADS // NA_GROUP, NA_GROUP * W, kr * W)


def _natten_kernel(q_ref, k_ref, v_ref, kc_ref, vc_ref, bias_ref, o_ref):
    W = GRID_W
    rows = k_ref.shape[1] // W
    kr = min(NA_KR, rows)
    r = pl.program_id(1)
    r0 = jnp.clip(r - kr // 2, 0, rows - kr)
    start = pl.multiple_of(r0 * W, W)
    kw = k_ref[0, pl.ds(start, kr * W), :]
    vw = v_ref[0, pl.ds(start, kr * W), :]
    q = q_ref[0]
    rb = lax.broadcasted_iota(jnp.int32, (NA_GROUP * W, NA_GW), 0) // W
    cb = lax.broadcasted_iota(jnp.int32, (NA_GROUP * W, NA_GW), 1) // NA_HEAD_DIM
    diag = rb == cb
    for grp in range(NA_HEADS // NA_GROUP):
        gs = slice(grp * NA_GW, (grp + 1) * NA_GW)
        qg = q[:, gs]
        qbd = jnp.where(diag, jnp.concatenate([qg] * NA_GROUP, axis=0), jnp.zeros_like(qg[:1]))
        s_win = _dot_nt(qbd, kw[:, gs]) + bias_ref[0, grp]
        s_ctx = _dot_nt(qbd, kc_ref[0, :, gs])
        m = jnp.maximum(jnp.max(s_win, axis=-1, keepdims=True), jnp.max(s_ctx, axis=-1, keepdims=True))
        p_win = jnp.exp(s_win - m)
        p_ctx = jnp.exp(s_ctx - m)
        denom = jnp.sum(p_win, axis=-1, keepdims=True) + jnp.sum(p_ctx, axis=-1, keepdims=True)
        of = (_dot(p_win.astype(BF16), vw[:, gs]) + _dot(p_ctx.astype(BF16), vc_ref[0, :, gs])) / denom
        of = jnp.where(diag, of, 0.0)
        og = of[0:W]
        for h in range(1, NA_GROUP):
            og = og + of[h * W:(h + 1) * W]
        o_ref[0, :, gs] = og.astype(o_ref.dtype)


def _natten(q, k, v, kc, vc, bias):
    B, T, _ = q.shape
    rows = T // GRID_W
    kr = min(NA_KR, rows)
    L = kc.shape[1]
    full = lambda n: pl.BlockSpec((1, n, NA_WIDTH), lambda b, r: (b, 0, 0))
    row = pl.BlockSpec((1, GRID_W, NA_WIDTH), lambda b, r: (b, r, 0))

    def bias_map(b, r):
        return (r - jnp.clip(r - kr // 2, 0, rows - kr), 0, 0, 0)

    return pl.pallas_call(
        _natten_kernel,
        grid=(B, rows),
        in_specs=[row, full(T), full(T), full(L), full(L),
                  pl.BlockSpec((1,) + bias.shape[1:], bias_map)],
        out_specs=row,
        out_shape=jax.ShapeDtypeStruct((B, T, NA_WIDTH), BF16),
        compiler_params=pltpu.CompilerParams(
            dimension_semantics=("parallel", "arbitrary"), vmem_limit_bytes=VMEM_LIMIT),
        name="natten",
    )(q, k, v, kc, vc, bias)


def _out_proj_kernel(yna_ref, of_ref, ob_ref, gate_ref, hgw_ref, w_ref, x_ref, gt_ref, nw_ref, o_ref):
    o = of_ref[0] + ob_ref[0]
    gate = gate_ref[0].astype(F32)
    hgw = hgw_ref[...]
    parts = []
    for h in range(HG_HEADS):
        sl = slice(h * HG_DK, (h + 1) * HG_DK)
        parts.append((_rms(o[:, sl], hgw) * gate[:, sl]).astype(BF16))
    y_hg = jnp.concatenate(parts, axis=-1)
    y = _dot(yna_ref[0], w_ref[:NA_WIDTH, :]) + _dot(y_hg, w_ref[NA_WIDTH:, :])
    o_ref[0] = x_ref[0] + gt_ref[0] * _rms(y, nw_ref[...])


def _out_proj(y_na, o_fw, o_bw, gate, hgw, w_bf, x, gt, nw, tm):
    B, T, _ = x.shape
    tok = lambda n: pl.BlockSpec((1, tm, n), lambda b, i: (b, i, 0))
    const = lambda shape: pl.BlockSpec(shape, lambda b, i: (0,) * len(shape))
    return pl.pallas_call(
        _out_proj_kernel,
        grid=(B, T // tm),
        in_specs=[tok(512), tok(512), tok(512), tok(512), const((1, HG_DK)), const(w_bf.shape),
                  tok(D_MODEL), pl.BlockSpec((1, 1, D_MODEL), lambda b, i: (b, 0, 0)), const((1, D_MODEL))],
        out_specs=tok(D_MODEL),
        out_shape=jax.ShapeDtypeStruct(x.shape, F32),
        compiler_params=pltpu.CompilerParams(
            dimension_semantics=("parallel", "parallel"), vmem_limit_bytes=VMEM_LIMIT),
        name="out_proj",
    )(y_na, o_fw, o_bw, gate, hgw, w_bf, x, gt, nw)


def _ffn_kernel(x_ref, sc_ref, sh_ref, gt_ref, npre_ref, npost_ref, w1_ref, w2_ref, o_ref, act_ref):
    d_ff = w2_ref.shape[0]
    x = x_ref[0]
    hb = (_rms(x, npre_ref[...]) * (1.0 + sc_ref[0]) + sh_ref[0]).astype(BF16)
    step = 256
    for j in range(0, d_ff, step):
        gate = _dot(hb, w1_ref[:, j:j + step])
        up = _dot(hb, w1_ref[:, d_ff + j:d_ff + j + step])
        act_ref[:, j:j + step] = (_silu(gate) * up).astype(BF16)
    z = _dot(act_ref[...], w2_ref[...])
    o_ref[0] = x + gt_ref[0] * _rms(z, npost_ref[...])


def _ffn(x, sc, sh, gt, npre, npost, w1_bf, w2_bf, tm):
    B, T, _ = x.shape
    tok = pl.BlockSpec((1, tm, D_MODEL), lambda b, i: (b, i, 0))
    mod = pl.BlockSpec((1, 1, D_MODEL), lambda b, i: (b, 0, 0))
    const = lambda shape: pl.BlockSpec(shape, lambda b, i: (0,) * len(shape))
    return pl.pallas_call(
        _ffn_kernel,
        grid=(B, T // tm),
        in_specs=[tok, mod, mod, mod, const((1, D_MODEL)), const((1, D_MODEL)),
                  const(w1_bf.shape), const(w2_bf.shape)],
        out_specs=tok,
        out_shape=jax.ShapeDtypeStruct(x.shape, F32),
        scratch_shapes=[pltpu.VMEM((tm, w2_bf.shape[0]), BF16)],
        compiler_params=pltpu.CompilerParams(
            dimension_semantics=("parallel", "parallel"), vmem_limit_bytes=VMEM_LIMIT),
        name="ffn",
    )(x, sc, sh, gt, npre, npost, w1_bf, w2_bf)


def kernel(x, c, ctx, c_ctx, w_ada, b_ada, norm_mix_pre, norm_mix_post, norm_ffn_pre, norm_ffn_post,
           w_in, na_rpb, hg_lb_logits, hg_norm_w, w_out, w_ffn_in, w_ffn_out):
    B, T, D = x.shape
    assert w_ada.shape[0] == 1, "single-layer stack"
    rows = T // GRID_W

    cv = jnp.zeros((8, D), F32).at[:B].set(c).at[B].set(c_ctx)
    mod = _ada(cv, w_ada[0], b_ada[0][None, :])
    sh_m, sc_m, gt_m, sh_f, sc_f, gt_f = [mod[:, i * D:(i + 1) * D] for i in range(N_MOD)]
    lat = lambda m: m[:B, None, :]
    cx = lambda m: jnp.broadcast_to(m[B][None, None, :], (B, 1, D))

    w_in_bf = w_in[0].astype(BF16)
    lbl = hg_lb_logits.reshape(hg_lb_logits.shape[0], 2 * HG_WIDTH)
    nw_pre = norm_mix_pre[0][None, :]

    q_na, k_na, v_na, q_hg, g_fw, g_bw, v_hg, gate = _in_proj(x, lat(sc_m), lat(sh_m), nw_pre, w_in_bf, lbl, 512)
    _, k_c, v_c, _, g_cfw, g_cbw, vhg_c, _ = _in_proj(ctx, cx(sc_m), cx(sh_m), nw_pre, w_in_bf, lbl,
                                                      ctx.shape[1])

    s_fw, s_bw = _ctx_state(g_cfw, g_cbw, vhg_c)
    o_fw, o_bw = _hgrn(q_hg, v_hg, g_fw, g_bw, s_fw, s_bw, 1024)

    y_na = _natten(q_na, k_na, v_na, k_c, v_c, _na_bias(na_rpb[0], rows))

    x1 = _out_proj(y_na, o_fw, o_bw, gate, hg_norm_w[0][None, :], w_out[0].astype(BF16), x, lat(gt_m),
                   norm_mix_post[0][None, :], 512)
    return _ffn(x1, lat(sc_f), lat(sh_f), lat(gt_f), norm_ffn_pre[0][None, :], norm_ffn_post[0][None, :],
                w_ffn_in[0].astype(BF16), w_ffn_out[0].astype(BF16), 512)
```

```python
import functools

import jax
import jax.numpy as jnp
import numpy as np
from jax import lax
from jax.experimental import pallas as pl
from jax.experimental.pallas import tpu as pltpu

D_MODEL = 1024
GRID_W = 64
NA_HEADS = 8
NA_HEAD_DIM = 64
NA_WIDTH = NA_HEADS * NA_HEAD_DIM
NA_KR = 8
NA_KC = 16
HG_HEADS = 4
HG_DK = 128
HG_WIDTH = HG_HEADS * HG_DK
HG_CHUNK = 64
N_LEVELS = 6
N_MOD = 6
EPS = 1e-6
NEG = -1e30

NA_GROUP = 4
NA_GW = NA_GROUP * NA_HEAD_DIM
NA_STEP_ROWS = 2
NA_TOK = NA_STEP_ROWS * GRID_W
NA_WIN_ROWS = NA_KR + NA_STEP_ROWS

F32 = jnp.float32
BF16 = jnp.bfloat16

VMEM_LIMIT = 56 * 1024 * 1024
FFN_COLS = 256


def _silu(x):
    return x * jax.nn.sigmoid(x)


def _dot(a, b):
    return jnp.dot(a, b, preferred_element_type=F32)


def _dot_nt(a, b):
    return lax.dot_general(a, b, (((1,), (1,)), ((), ())), preferred_element_type=F32)


def _dot_tn(a, b):
    return lax.dot_general(a, b, (((0,), (0,)), ((), ())), preferred_element_type=F32)


def _split3(x):
    x1 = x.astype(BF16)
    r1 = x - x1.astype(F32)
    x2 = r1.astype(BF16)
    r2 = r1 - x2.astype(F32)
    return x1, x2, r2.astype(BF16)


def _dot_exact_lhs(t, x):
    x1, x2, x3 = _split3(x)
    return _dot(t, x1) + _dot(t, x2) + _dot(t, x3)


def _dot_exact_rhs(x, t):
    x1, x2, x3 = _split3(x)
    return _dot(x1, t) + _dot(x2, t) + _dot(x3, t)


def _rms(x, w):
    return x * lax.rsqrt(jnp.mean(x * x, axis=-1, keepdims=True) + EPS) * w


def _ada_kernel(cv_ref, w_ref, b_ref, o_ref):
    s = _silu(cv_ref[...])
    s1, s2, s3 = _split3(s)
    w1, w2, w3 = _split3(w_ref[...])
    acc = _dot(s1, w1) + (_dot(s1, w2) + _dot(s2, w1)) + (_dot(s1, w3) + _dot(s2, w2) + _dot(s3, w1))
    o_ref[...] = acc + b_ref[...]


def _ada(cv, w_ada, b_ada):
    n = w_ada.shape[1]
    tn = 1536
    return pl.pallas_call(
        _ada_kernel,
        grid=(n // tn,),
        in_specs=[
            pl.BlockSpec((8, D_MODEL), lambda j: (0, 0)),
            pl.BlockSpec((D_MODEL, tn), lambda j: (0, j)),
            pl.BlockSpec((1, tn), lambda j: (0, j)),
        ],
        out_specs=pl.BlockSpec((8, tn), lambda j: (0, j)),
        out_shape=jax.ShapeDtypeStruct((8, n), F32),
        compiler_params=pltpu.CompilerParams(vmem_limit_bytes=VMEM_LIMIT),
        name="ada",
    )(cv, w_ada, b_ada)


def _in_proj_kernel(x_ref, sc_ref, sh_ref, nw_ref, w_ref, wkt_ref, lbl_ref,
                    qna_ref, knat_ref, vna_ref, qhg_ref, gfw_ref, gbw_ref, vhg_ref, gate_ref):
    x = x_ref[0]
    h = _rms(x, nw_ref[...]) * (1.0 + sc_ref[0]) + sh_ref[0]
    hb = h.astype(BF16)

    def proj(i):
        return _dot(hb, w_ref[:, i * 512:(i + 1) * 512])

    lbl = lbl_ref[...]
    e = jnp.exp(lbl - jnp.max(lbl, axis=0, keepdims=True))
    lb = e[0:1] / jnp.sum(e, axis=0, keepdims=True)

    qna_ref[0] = (proj(0) * (NA_HEAD_DIM ** -0.5)).astype(BF16)
    k_t = _dot_nt(wkt_ref[...], hb).astype(BF16)
    for j in range(knat_ref.shape[1]):
        knat_ref[0, j] = k_t[:, j * NA_TOK:(j + 1) * NA_TOK]
    vna_ref[0] = proj(2).astype(BF16)
    qhg_ref[0] = _silu(proj(3)).astype(BF16)
    lb_f = lb[:, :HG_WIDTH]
    lb_b = lb[:, HG_WIDTH:]
    gfw_ref[0] = jnp.log(lb_f + (1.0 - lb_f) * jax.nn.sigmoid(proj(4)))
    gbw_ref[0] = jnp.log(lb_b + (1.0 - lb_b) * jax.nn.sigmoid(proj(5)))
    vhg_ref[0] = proj(6).astype(BF16)
    gate_ref[0] = _silu(proj(7)).astype(BF16)


def _in_proj(x, sc, sh, nw, w_bf, wkt_bf, lbl, tm):
    B, T, _ = x.shape
    tok = lambda b, i: (b, i, 0)
    out_bf = jax.ShapeDtypeStruct((B, T, 512), BF16)
    out_f = jax.ShapeDtypeStruct((B, T, 512), F32)
    ospec = pl.BlockSpec((1, tm, 512), tok)
    return pl.pallas_call(
        _in_proj_kernel,
        grid=(B, T // tm),
        in_specs=[
            pl.BlockSpec((1, tm, D_MODEL), tok),
            pl.BlockSpec((1, 1, D_MODEL), lambda b, i: (b, 0, 0)),
            pl.BlockSpec((1, 1, D_MODEL), lambda b, i: (b, 0, 0)),
            pl.BlockSpec((1, D_MODEL), lambda b, i: (0, 0)),
            pl.BlockSpec(w_bf.shape, lambda b, i: (0, 0)),
            pl.BlockSpec(wkt_bf.shape, lambda b, i: (0, 0)),
            pl.BlockSpec(lbl.shape, lambda b, i: (0, 0)),
        ],
        out_specs=[ospec, pl.BlockSpec((1, tm // NA_TOK, NA_WIDTH, NA_TOK), lambda b, i: (b, i, 0, 0))]
        + [ospec] * 6,
        out_shape=[out_bf, jax.ShapeDtypeStruct((B, T // NA_TOK, NA_WIDTH, NA_TOK), BF16),
                   out_bf, out_bf, out_f, out_f, out_bf, out_bf],
        compiler_params=pltpu.CompilerParams(
            dimension_semantics=("parallel", "parallel"), vmem_limit_bytes=VMEM_LIMIT),
        name="in_proj",
    )(x, sc, sh, nw, w_bf, wkt_bf, lbl)


def _ctx_state_kernel(gfw_ref, gbw_ref, v_ref, sfw_ref, sbw_ref):
    L = gfw_ref.shape[1]
    r = lax.broadcasted_iota(jnp.int32, (L, L), 0)
    c = lax.broadcasted_iota(jnp.int32, (L, L), 1)
    upper = jnp.where(c > r, 1.0, 0.0).astype(BF16)
    lower = jnp.where(c < r, 1.0, 0.0).astype(BF16)
    v = v_ref[0]
    for g_ref, tri, s_ref in ((gfw_ref, upper, sfw_ref), (gbw_ref, lower, sbw_ref)):
        g = g_ref[0]
        kw = ((1.0 - jnp.exp(g)) * jnp.exp(_dot_exact_lhs(tri, g))).astype(BF16)
        for h in range(HG_HEADS):
            sl = slice(h * HG_DK, (h + 1) * HG_DK)
            s_ref[0, h] = _dot_tn(kw[:, sl], v[:, sl])


def _ctx_state(g_cfw, g_cbw, v_c):
    B, L, _ = g_cfw.shape
    tok = pl.BlockSpec((1, L, HG_WIDTH), lambda b: (b, 0, 0))
    st = pl.BlockSpec((1, HG_HEADS, HG_DK, HG_DK), lambda b: (b, 0, 0, 0))
    sshape = jax.ShapeDtypeStruct((B, HG_HEADS, HG_DK, HG_DK), F32)
    return pl.pallas_call(
        _ctx_state_kernel,
        grid=(B,),
        in_specs=[tok, tok, tok],
        out_specs=[st, st],
        out_shape=[sshape, sshape],
        compiler_params=pltpu.CompilerParams(vmem_limit_bytes=VMEM_LIMIT),
        name="ctx_state",
    )(g_cfw, g_cbw, v_c)


SUBLANES = 8
MXU_LEVELS = [j for j in range(1, N_LEVELS) if (1 << j) < SUBLANES]
N_SEG = len(MXU_LEVELS) + 1


def _hg_consts():
    C = HG_CHUNK
    t = np.arange(C)[:, None]
    s = np.arange(C)[None, :]
    masks, sel = [], []
    for j in range(N_LEVELS):
        hs = 1 << j
        same = (t // (2 * hs)) == (s // (2 * hs))
        upper_t = (t % (2 * hs)) >= hs
        m = (t // (2 * hs)) * (2 * hs) + hs - 1
        masks.append(same & upper_t & ((s % (2 * hs)) < hs))
        if j in MXU_LEVELS:
            sel.append(np.where(upper_t, (s > m) & (s <= t), (s > t) & (s <= m)))
    masks.append(t == s)
    sel.append(s <= t)
    masks = np.stack(masks).astype(np.float32)
    sel = np.stack(sel).astype(np.float32)
    both = lambda a: np.stack([a, a[:, ::-1, ::-1]])
    sel = both(sel).reshape(2, N_SEG * C, C)
    return both(masks), np.tile(sel, (1, 1, 3))


class _HgDirection:
    def __init__(self, q_ref, v_ref, g_ref, st_ref, o_ref, sel_ref, mask_ref, rev):
        C = HG_CHUNK
        self.q_ref, self.v_ref, self.st_ref, self.o_ref, self.rev = q_ref, v_ref, st_ref, o_ref, rev
        d = 1 if rev else 0
        self.sel = sel_ref.at[d]
        g = g_ref[0]
        self.g_parts = _split3(g)
        self.f = jnp.exp(g)
        self.k = 1.0 - self.f
        self.level_mask = [mask_ref[d, j] for j in range(N_LEVELS + 1)]
        parity = lax.broadcasted_iota(jnp.int32, (C, HG_DK), 0) % 2
        self.pair_query = parity == (0 if rev else 1)
        n_pairs = q_ref.shape[1] // (2 * C)
        self.pairs = list(range(n_pairs - 1, -1, -1) if rev else range(n_pairs))
        self.st = st_ref[...]

    def decays(self, p):
        C = HG_CHUNK
        g_cat = jnp.concatenate(
            [jnp.concatenate([part[2 * p * C:(2 * p + 1) * C], part[(2 * p + 1) * C:(2 * p + 2) * C]], axis=1)
             for part in self.g_parts], axis=0)
        sums = _dot(self.sel[...], g_cat)
        parts = [sums[i * C:(i + 1) * C] for i in range(N_SEG - 1)]
        b = sums[(N_SEG - 1) * C:N_SEG * C]
        rows = {}

        def row(m):
            if m not in rows:
                rows[m] = jnp.broadcast_to(b[m:m + 1, :], (SUBLANES, 2 * HG_DK))
            return rows[m]

        for j in range(MXU_LEVELS[-1] + 1, N_LEVELS):
            hs = 1 << j
            groups = []
            for r in range(0, C, SUBLANES):
                start = r // (2 * hs) * (2 * hs)
                b_mid = row(start + (hs if self.rev else hs - 1))
                query_side = (r - start >= hs) != self.rev
                groups.append(b[r:r + SUBLANES] - b_mid if query_side else b_mid - b[r:r + SUBLANES])
            parts.append(jnp.concatenate(groups, axis=0))
        parts.append(b)
        parts.append(jnp.concatenate([row(0 if self.rev else C - 1)] * (C // SUBLANES), axis=0) - b)
        return [jnp.exp(x) for x in parts]

    def scores(self, p, e2):
        C = HG_CHUNK
        out = []
        for c in ((2 * p + 1, 2 * p) if self.rev else (2 * p, 2 * p + 1)):
            rows = slice(c * C, (c + 1) * C)
            lanes = slice((c - 2 * p) * HG_DK, (c - 2 * p + 1) * HG_DK)
            qc = self.q_ref[0, rows, :].astype(F32)
            kc = self.k[rows]
            e0 = jnp.where(self.pair_query, self.f[rows], 1.0)
            score = lambda qf, kf: _dot_nt(qf.astype(BF16), kf.astype(BF16))
            a = score(qc, kc) * self.level_mask[N_LEVELS]
            a = a + score(qc * e0, kc * e0) * self.level_mask[0]
            for j in range(1, N_LEVELS):
                ej = e2[j - 1][:, lanes]
                a = a + score(qc * ej, kc * ej) * self.level_mask[j]
            e_b = e2[N_LEVELS - 1][:, lanes]
            total = e_b[0:1] if self.rev else e_b[C - 1:C]
            q_dec = jnp.concatenate([(qc * e_b).astype(BF16), a.astype(BF16)], axis=1)
            out.append((rows, q_dec, (kc * e2[N_LEVELS][:, lanes]).astype(BF16), total))
        return out

    def outputs(self, chunk_terms):
        for rows, q_dec, k_dec, total in chunk_terms:
            vc = self.v_ref[0, rows, :]
            self.o_ref[0, rows, :] = _dot(q_dec, jnp.concatenate([self.st.astype(BF16), vc], axis=0))
            total_col = jnp.transpose(jnp.broadcast_to(total, (SUBLANES, HG_DK)))[:, 0:1]
            self.st = total_col * self.st + _dot_tn(k_dec, vc)

    def finish(self):
        self.st_ref[...] = self.st


def _hgrn_kernel(qf_ref, vf_ref, gf_ref, qb_ref, vb_ref, gb_ref, s0f_ref, s0b_ref, sel_ref, mask_ref,
                 of_ref, ob_ref, stf, stb):
    @pl.when(pl.program_id(2) == 0)
    def _():
        stf[...] = s0f_ref[0, 0]
        stb[...] = s0b_ref[0, 0]

    fw = _HgDirection(qf_ref, vf_ref, gf_ref, stf, of_ref, sel_ref, mask_ref, False)
    bw = _HgDirection(qb_ref, vb_ref, gb_ref, stb, ob_ref, sel_ref, mask_ref, True)
    units = [(dirn, p) for pf, pb in zip(fw.pairs, bw.pairs) for dirn, p in ((fw, pf), (bw, pb))]
    e_next = units[0][0].decays(units[0][1])
    pending = None
    for i, (dirn, p) in enumerate(units):
        e_cur = e_next
        if i + 1 < len(units):
            e_next = units[i + 1][0].decays(units[i + 1][1])
        terms = dirn.scores(p, e_cur)
        if pending is not None:
            pending[0].outputs(pending[1])
        pending = (dirn, terms)
    pending[0].outputs(pending[1])
    fw.finish()
    bw.finish()


def _hgrn(q_hg, v_hg, g_fw, g_bw, s_fw, s_bw, tb):
    B, T, _ = q_hg.shape
    nb = T // tb
    fwd = pl.BlockSpec((1, tb, HG_DK), lambda b, h, i: (b, i, h))
    bwd = pl.BlockSpec((1, tb, HG_DK), lambda b, h, i: (b, nb - 1 - i, h))
    st = pl.BlockSpec((1, 1, HG_DK, HG_DK), lambda b, h, i: (b, h, 0, 0))
    masks, sel = _hg_consts()
    masks, sel = jnp.asarray(masks), jnp.asarray(sel, dtype=BF16)
    oshape = jax.ShapeDtypeStruct((B, T, HG_WIDTH), F32)
    return pl.pallas_call(
        _hgrn_kernel,
        grid=(B, HG_HEADS, nb),
        in_specs=[fwd, fwd, fwd, bwd, bwd, bwd, st, st,
                  pl.BlockSpec(sel.shape, lambda b, h, i: (0, 0, 0)),
                  pl.BlockSpec(masks.shape, lambda b, h, i: (0, 0, 0, 0))],
        out_specs=[fwd, bwd],
        out_shape=[oshape, oshape],
        scratch_shapes=[pltpu.VMEM((HG_DK, HG_DK), F32), pltpu.VMEM((HG_DK, HG_DK), F32)],
        compiler_params=pltpu.CompilerParams(
            dimension_semantics=("parallel", "parallel", "arbitrary"), vmem_limit_bytes=VMEM_LIMIT),
        name="hgrn",
    )(q_hg, v_hg, g_fw, q_hg, v_hg, g_bw, s_fw, s_bw, sel, masks)


def _na_bias_consts():
    W = GRID_W
    cols = np.arange(W)
    c0 = np.clip(cols - NA_KC // 2, 0, W - NA_KC)
    wk = np.arange(W)[None, :]
    valid = (wk >= c0[:, None]) & (wk < c0[:, None] + NA_KC)
    dc = wk - cols[:, None] + NA_KC - 1
    onehot = (dc[None] == np.arange(2 * NA_KC)[:, None, None]) & valid[None]
    neg = np.where(valid, 0.0, NEG).astype(np.float32).reshape(1, W * W)
    return onehot.reshape(2 * NA_KC, W * W).astype(np.float32), neg


def _na_bias_kernel(rpb_ref, onehot_ref, neg_ref, o_ref):
    o_ref[...] = _dot_exact_rhs(rpb_ref[...], onehot_ref[...]) + neg_ref[...]


def _na_plan(rows):
    assert rows % NA_STEP_ROWS == 0 and rows >= NA_WIN_ROWS and (rows - NA_WIN_ROWS) % NA_STEP_ROWS == 0
    variants, starts = [], []
    for i in range(rows // NA_STEP_ROWS):
        w0 = int(np.clip(NA_STEP_ROWS * i - NA_KR // 2, 0, rows - NA_WIN_ROWS))
        geom = tuple((int(np.clip(r - NA_KR // 2, 0, rows - NA_KR)) - w0, r - w0)
                     for r in range(NA_STEP_ROWS * i, NA_STEP_ROWS * (i + 1)))
        if not variants or variants[-1] != geom:
            assert geom not in variants
            variants.append(geom)
            starts.append(i)
    return variants, starts


def _na_bias(rpb, rows):
    W = GRID_W
    n_dr = 2 * NA_KR - 1
    onehot, neg = _na_bias_consts()
    rpb2 = jnp.zeros((128, 2 * NA_KC), F32).at[:NA_HEADS * n_dr, :2 * NA_KC - 1].set(
        rpb.astype(F32).reshape(NA_HEADS * n_dr, 2 * NA_KC - 1))
    toep = pl.pallas_call(
        _na_bias_kernel,
        out_shape=jax.ShapeDtypeStruct((128, W * W), F32),
        name="na_bias",
    )(rpb2, jnp.asarray(onehot, dtype=BF16), jnp.asarray(neg))
    toep = toep[:NA_HEADS * n_dr].reshape(NA_HEADS, n_dr, W, W)
    masked = jnp.full((NA_HEADS, W, W), NEG, F32)
    variants, _ = _na_plan(rows)
    tables = []
    for geom in variants:
        per_row = []
        for first, qrow in geom:
            blocks = [toep[:, j - qrow + NA_KR - 1] if first <= j < first + NA_KR else masked
                      for j in range(NA_WIN_ROWS)]
            per_row.append(jnp.stack(blocks, axis=2))
        tables.append(jnp.stack(per_row, axis=1))
    bias = jnp.stack(tables)
    return bias.reshape(len(variants), NA_HEADS // NA_GROUP, NA_GROUP * NA_TOK, NA_WIN_ROWS * W)


def _natten_kernel(q_ref, kt_ref, v_ref, kct_ref, vc_ref, bias_ref, o_ref):
    W = GRID_W
    rows = v_ref.shape[1] // W
    i = pl.program_id(1)
    w0 = jnp.clip(NA_STEP_ROWS * i - NA_KR // 2, 0, rows - NA_WIN_ROWS)
    blk0 = w0 // NA_STEP_ROWS
    kt = jnp.concatenate([kt_ref[0, blk0 + j] for j in range(NA_WIN_ROWS // NA_STEP_ROWS)], axis=1)
    kct = jnp.concatenate([kct_ref[0, j] for j in range(kct_ref.shape[1])], axis=1)
    vw = v_ref[0, pl.ds(pl.multiple_of(w0 * W, NA_TOK), NA_WIN_ROWS * W), :]
    q = q_ref[0]
    rb = lax.broadcasted_iota(jnp.int32, (NA_GROUP * NA_TOK, NA_GW), 0) // NA_TOK
    cb = lax.broadcasted_iota(jnp.int32, (NA_GROUP * NA_TOK, NA_GW), 1) // NA_HEAD_DIM
    diag = rb == cb
    groups = [slice(grp * NA_GW, (grp + 1) * NA_GW) for grp in range(NA_HEADS // NA_GROUP)]
    logits = []
    for grp, gs in enumerate(groups):
        qg = q[:, gs]
        qbd = jnp.where(diag, jnp.concatenate([qg] * NA_GROUP, axis=0), jnp.zeros_like(qg[:1]))
        logits.append((_dot(qbd, kt[gs, :]) + bias_ref[0, grp], _dot(qbd, kct[gs, :])))
    for gs, (s_win, s_ctx) in zip(groups, logits):
        m = jnp.maximum(jnp.max(s_win, axis=-1, keepdims=True), jnp.max(s_ctx, axis=-1, keepdims=True))
        p_win = jnp.exp(s_win - m)
        p_ctx = jnp.exp(s_ctx - m)
        denom = jnp.sum(p_win, axis=-1, keepdims=True) + jnp.sum(p_ctx, axis=-1, keepdims=True)
        of = (_dot(p_win.astype(BF16), vw[:, gs]) + _dot(p_ctx.astype(BF16), vc_ref[0, :, gs])) / denom
        of = jnp.where(diag, of, 0.0)
        og = of[0:NA_TOK]
        for h in range(1, NA_GROUP):
            og = og + of[h * NA_TOK:(h + 1) * NA_TOK]
        o_ref[0, :, gs] = og.astype(o_ref.dtype)


def _natten(q, kt, v, kct, vc, bias):
    B, T, _ = q.shape
    rows = T // GRID_W
    _, starts = _na_plan(rows)
    tok = pl.BlockSpec((1, NA_TOK, NA_WIDTH), lambda b, i: (b, i, 0))
    whole = lambda a: pl.BlockSpec((1,) + a.shape[1:], lambda b, i: (b,) + (0,) * (a.ndim - 1))

    def bias_map(b, i):
        return (sum((i >= s).astype(jnp.int32) for s in starts[1:]), 0, 0, 0)

    return pl.pallas_call(
        _natten_kernel,
        grid=(B, rows // NA_STEP_ROWS),
        in_specs=[tok, whole(kt), whole(v), whole(kct), whole(vc),
                  pl.BlockSpec((1,) + bias.shape[1:], bias_map)],
        out_specs=tok,
        out_shape=jax.ShapeDtypeStruct((B, T, NA_WIDTH), BF16),
        compiler_params=pltpu.CompilerParams(
            dimension_semantics=("parallel", "arbitrary"), vmem_limit_bytes=VMEM_LIMIT),
        name="natten",
    )(q, kt, v, kct, vc, bias)


def _post_mix_kernel(yna_ref, of_ref, ob_ref, gate_ref, hgw_ref, wo_ref, x_ref, gtm_ref, nmix_ref,
                     scf_ref, shf_ref, gtf_ref, npre_ref, npost_ref, w1_ref, w2_ref, o_ref, act_ref):
    o = of_ref[0] + ob_ref[0]
    gate = gate_ref[0].astype(F32)
    hgw = hgw_ref[...]
    parts = []
    for h in range(HG_HEADS):
        sl = slice(h * HG_DK, (h + 1) * HG_DK)
        parts.append((_rms(o[:, sl], hgw) * gate[:, sl]).astype(BF16))
    y_hg = jnp.concatenate(parts, axis=-1)
    y = _dot(yna_ref[0], wo_ref[:NA_WIDTH, :]) + _dot(y_hg, wo_ref[NA_WIDTH:, :])
    x1 = x_ref[0] + gtm_ref[0] * _rms(y, nmix_ref[...])
    d_ff = w2_ref.shape[0]
    hb = (_rms(x1, npre_ref[...]) * (1.0 + scf_ref[0]) + shf_ref[0]).astype(BF16)
    for j in range(0, d_ff, FFN_COLS):
        gate_j = _dot(hb, w1_ref[:, j:j + FFN_COLS])
        up_j = _dot(hb, w1_ref[:, d_ff + j:d_ff + j + FFN_COLS])
        act_ref[:, j:j + FFN_COLS] = (_silu(gate_j) * up_j).astype(BF16)
    z = _dot(act_ref[...], w2_ref[...])
    o_ref[0] = x1 + gtf_ref[0] * _rms(z, npost_ref[...])


def _post_mix(y_na, o_fw, o_bw, gate, hgw, wo_bf, x, gt_m, nmix, sc_f, sh_f, gt_f, npre, npost, w1_bf, w2_bf, tm):
    B, T, _ = x.shape
    d_ff = w2_bf.shape[0]
    assert d_ff % FFN_COLS == 0
    tok = lambda n: pl.BlockSpec((1, tm, n), lambda b, i: (b, i, 0))
    mod = pl.BlockSpec((1, 1, D_MODEL), lambda b, i: (b, 0, 0))
    const = lambda shape: pl.BlockSpec(shape, lambda b, i: (0,) * len(shape), pipeline_mode=pl.Buffered(1))
    return pl.pallas_call(
        _post_mix_kernel,
        grid=(B, T // tm),
        in_specs=[tok(NA_WIDTH), tok(HG_WIDTH), tok(HG_WIDTH), tok(HG_WIDTH), const((1, HG_DK)),
                  const(wo_bf.shape), tok(D_MODEL), mod, const((1, D_MODEL)),
                  mod, mod, mod, const((1, D_MODEL)), const((1, D_MODEL)),
                  const(w1_bf.shape), const(w2_bf.shape)],
        out_specs=tok(D_MODEL),
        out_shape=jax.ShapeDtypeStruct(x.shape, F32),
        scratch_shapes=[pltpu.VMEM((tm, d_ff), BF16)],
        compiler_params=pltpu.CompilerParams(
            dimension_semantics=("parallel", "parallel"), vmem_limit_bytes=VMEM_LIMIT),
        name="post_mix",
    )(y_na, o_fw, o_bw, gate, hgw, wo_bf, x, gt_m, nmix, sc_f, sh_f, gt_f, npre, npost, w1_bf, w2_bf)


def kernel(x, c, ctx, c_ctx, w_ada, b_ada, norm_mix_pre, norm_mix_post, norm_ffn_pre, norm_ffn_post,
           w_in, na_rpb, hg_lb_logits, hg_norm_w, w_out, w_ffn_in, w_ffn_out):
    B, T, D = x.shape
    assert w_ada.shape[0] == 1, "single-layer stack"
    rows = T // GRID_W

    cv = jnp.zeros((8, D), F32).at[:B].set(c).at[B].set(c_ctx)
    mod = _ada(cv, w_ada[0], b_ada[0][None, :])
    sh_m, sc_m, gt_m, sh_f, sc_f, gt_f = [mod[:, i * D:(i + 1) * D] for i in range(N_MOD)]
    lat = lambda m: m[:B, None, :]
    cx = lambda m: jnp.broadcast_to(m[B][None, None, :], (B, 1, D))

    w_in_bf = w_in[0].astype(BF16)
    wkt_bf = w_in[0][:, NA_WIDTH:2 * NA_WIDTH].T.astype(BF16)
    lbl = hg_lb_logits.reshape(hg_lb_logits.shape[0], 2 * HG_WIDTH)
    nw_pre = norm_mix_pre[0][None, :]

    q_na, kt_na, v_na, q_hg, g_fw, g_bw, v_hg, gate = _in_proj(x, lat(sc_m), lat(sh_m), nw_pre, w_in_bf, wkt_bf,
                                                                lbl, 512)
    _, kt_c, v_c, _, g_cfw, g_cbw, vhg_c, _ = _in_proj(ctx, cx(sc_m), cx(sh_m), nw_pre, w_in_bf, wkt_bf, lbl,
                                                       ctx.shape[1])

    s_fw, s_bw = _ctx_state(g_cfw, g_cbw, vhg_c)
    o_fw, o_bw = _hgrn(q_hg, v_hg, g_fw, g_bw, s_fw, s_bw, 1024)

    y_na = _natten(q_na, kt_na, v_na, kt_c, v_c, _na_bias(na_rpb[0], rows))

    return _post_mix(y_na, o_fw, o_bw, gate, hg_norm_w[0][None, :], w_out[0].astype(BF16), x, lat(gt_m),
                     norm_mix_post[0][None, :], lat(sc_f), lat(sh_f), lat(gt_f), norm_ffn_pre[0][None, :],
                     norm_ffn_post[0][None, :], w_ffn_in[0].astype(BF16), w_ffn_out[0].astype(BF16), 512)
```

```python
import functools

import jax
import jax.numpy as jnp
import numpy as np
from jax import lax
from jax.experimental import pallas as pl
from jax.experimental.pallas import tpu as pltpu

D_MODEL = 1024
GRID_W = 64
NA_HEADS = 8
NA_HEAD_DIM = 64
NA_WIDTH = NA_HEADS * NA_HEAD_DIM
NA_KR = 8
NA_KC = 16
HG_HEADS = 4
HG_DK = 128
HG_WIDTH = HG_HEADS * HG_DK
HG_CHUNK = 64
N_LEVELS = 6
N_MOD = 6
EPS = 1e-6
NEG = -1e30

NA_GROUP = 4
NA_GW = NA_GROUP * NA_HEAD_DIM
NA_STEP_ROWS = 2
NA_TOK = NA_STEP_ROWS * GRID_W
NA_WIN_ROWS = NA_KR + NA_STEP_ROWS

F32 = jnp.float32
BF16 = jnp.bfloat16

VMEM_LIMIT = 56 * 1024 * 1024
FFN_COLS = 256


def _silu(x):
    return x * jax.nn.sigmoid(x)


def _dot(a, b):
    return jnp.dot(a, b, preferred_element_type=F32)


def _dot_nt(a, b):
    return lax.dot_general(a, b, (((1,), (1,)), ((), ())), preferred_element_type=F32)


def _dot_tn(a, b):
    return lax.dot_general(a, b, (((0,), (0,)), ((), ())), preferred_element_type=F32)


def _split3(x):
    x1 = x.astype(BF16)
    r1 = x - x1.astype(F32)
    x2 = r1.astype(BF16)
    r2 = r1 - x2.astype(F32)
    return x1, x2, r2.astype(BF16)


def _dot_exact_lhs(t, x):
    x1, x2, x3 = _split3(x)
    return _dot(t, x1) + _dot(t, x2) + _dot(t, x3)


def _dot_exact_rhs(x, t):
    x1, x2, x3 = _split3(x)
    return _dot(x1, t) + _dot(x2, t) + _dot(x3, t)


def _rms(x, w):
    return x * lax.rsqrt(jnp.mean(x * x, axis=-1, keepdims=True) + EPS) * w


def _ada_kernel(cv_ref, w_ref, b_ref, o_ref):
    s = _silu(cv_ref[...])
    s1, s2, s3 = _split3(s)
    w1, w2, w3 = _split3(w_ref[...])
    acc = _dot(s1, w1) + (_dot(s1, w2) + _dot(s2, w1)) + (_dot(s1, w3) + _dot(s2, w2) + _dot(s3, w1))
    o_ref[...] = acc + b_ref[...]


def _ada(cv, w_ada, b_ada):
    n = w_ada.shape[1]
    tn = 1536
    return pl.pallas_call(
        _ada_kernel,
        grid=(n // tn,),
        in_specs=[
            pl.BlockSpec((8, D_MODEL), lambda j: (0, 0)),
            pl.BlockSpec((D_MODEL, tn), lambda j: (0, j)),
            pl.BlockSpec((1, tn), lambda j: (0, j)),
        ],
        out_specs=pl.BlockSpec((8, tn), lambda j: (0, j)),
        out_shape=jax.ShapeDtypeStruct((8, n), F32),
        compiler_params=pltpu.CompilerParams(vmem_limit_bytes=VMEM_LIMIT),
        name="ada",
    )(cv, w_ada, b_ada)


def _in_proj_kernel(x_ref, sc_ref, sh_ref, nw_ref, w_ref, wkt_ref, lbl_ref,
                    qna_ref, knat_ref, vna_ref, qhg_ref, gfw_ref, gbw_ref, vhg_ref, gate_ref):
    x = x_ref[0]
    h = _rms(x, nw_ref[...]) * (1.0 + sc_ref[0]) + sh_ref[0]
    hb = h.astype(BF16)

    def proj(i):
        return _dot(hb, w_ref[:, i * 512:(i + 1) * 512])

    lbl = lbl_ref[...]
    e = jnp.exp(lbl - jnp.max(lbl, axis=0, keepdims=True))
    lb = e[0:1] / jnp.sum(e, axis=0, keepdims=True)

    qna_ref[0] = (proj(0) * (NA_HEAD_DIM ** -0.5)).astype(BF16)
    k_t = _dot_nt(wkt_ref[...], hb).astype(BF16)
    for j in range(knat_ref.shape[1]):
        knat_ref[0, j] = k_t[:, j * NA_TOK:(j + 1) * NA_TOK]
    vna_ref[0] = proj(2).astype(BF16)
    qhg_ref[0] = _silu(proj(3)).astype(BF16)
    lb_f = lb[:, :HG_WIDTH]
    lb_b = lb[:, HG_WIDTH:]
    gfw_ref[0] = jnp.log(lb_f + (1.0 - lb_f) * jax.nn.sigmoid(proj(4)))
    gbw_ref[0] = jnp.log(lb_b + (1.0 - lb_b) * jax.nn.sigmoid(proj(5)))
    vhg_ref[0] = proj(6).astype(BF16)
    gate_ref[0] = _silu(proj(7)).astype(BF16)


def _in_proj(x, sc, sh, nw, w_bf, wkt_bf, lbl, tm):
    B, T, _ = x.shape
    tok = lambda b, i: (b, i, 0)
    out_bf = jax.ShapeDtypeStruct((B, T, 512), BF16)
    out_f = jax.ShapeDtypeStruct((B, T, 512), F32)
    ospec = pl.BlockSpec((1, tm, 512), tok)
    return pl.pallas_call(
        _in_proj_kernel,
        grid=(B, T // tm),
        in_specs=[
            pl.BlockSpec((1, tm, D_MODEL), tok),
            pl.BlockSpec((1, 1, D_MODEL), lambda b, i: (b, 0, 0)),
            pl.BlockSpec((1, 1, D_MODEL), lambda b, i: (b, 0, 0)),
            pl.BlockSpec((1, D_MODEL), lambda b, i: (0, 0)),
            pl.BlockSpec(w_bf.shape, lambda b, i: (0, 0)),
            pl.BlockSpec(wkt_bf.shape, lambda b, i: (0, 0)),
            pl.BlockSpec(lbl.shape, lambda b, i: (0, 0)),
        ],
        out_specs=[ospec, pl.BlockSpec((1, tm // NA_TOK, NA_WIDTH, NA_TOK), lambda b, i: (b, i, 0, 0))]
        + [ospec] * 6,
        out_shape=[out_bf, jax.ShapeDtypeStruct((B, T // NA_TOK, NA_WIDTH, NA_TOK), BF16),
                   out_bf, out_bf, out_f, out_f, out_bf, out_bf],
        compiler_params=pltpu.CompilerParams(
            dimension_semantics=("parallel", "parallel"), vmem_limit_bytes=VMEM_LIMIT),
        name="in_proj",
    )(x, sc, sh, nw, w_bf, wkt_bf, lbl)


def _ctx_state_kernel(gfw_ref, gbw_ref, v_ref, sfw_ref, sbw_ref):
    L = gfw_ref.shape[1]
    r = lax.broadcasted_iota(jnp.int32, (L, L), 0)
    c = lax.broadcasted_iota(jnp.int32, (L, L), 1)
    upper = jnp.where(c > r, 1.0, 0.0).astype(BF16)
    lower = jnp.where(c < r, 1.0, 0.0).astype(BF16)
    v = v_ref[0]
    for g_ref, tri, s_ref in ((gfw_ref, upper, sfw_ref), (gbw_ref, lower, sbw_ref)):
        g = g_ref[0]
        kw = ((1.0 - jnp.exp(g)) * jnp.exp(_dot_exact_lhs(tri, g))).astype(BF16)
        for h in range(HG_HEADS):
            sl = slice(h * HG_DK, (h + 1) * HG_DK)
            s_ref[0, h] = _dot_tn(kw[:, sl], v[:, sl])


def _ctx_state(g_cfw, g_cbw, v_c):
    B, L, _ = g_cfw.shape
    tok = pl.BlockSpec((1, L, HG_WIDTH), lambda b: (b, 0, 0))
    st = pl.BlockSpec((1, HG_HEADS, HG_DK, HG_DK), lambda b: (b, 0, 0, 0))
    sshape = jax.ShapeDtypeStruct((B, HG_HEADS, HG_DK, HG_DK), F32)
    return pl.pallas_call(
        _ctx_state_kernel,
        grid=(B,),
        in_specs=[tok, tok, tok],
        out_specs=[st, st],
        out_shape=[sshape, sshape],
        compiler_params=pltpu.CompilerParams(vmem_limit_bytes=VMEM_LIMIT),
        name="ctx_state",
    )(g_cfw, g_cbw, v_c)


SUBLANES = 8
MXU_LEVELS = [j for j in range(1, N_LEVELS) if (1 << j) < SUBLANES]
N_SEG = len(MXU_LEVELS) + 1
HG_OUTPUT_LAG = 2


def _hg_consts():
    C = HG_CHUNK
    t = np.arange(C)[:, None]
    s = np.arange(C)[None, :]
    masks, sel = [], []
    for j in range(N_LEVELS):
        hs = 1 << j
        same = (t // (2 * hs)) == (s // (2 * hs))
        upper_t = (t % (2 * hs)) >= hs
        m = (t // (2 * hs)) * (2 * hs) + hs - 1
        masks.append(same & upper_t & ((s % (2 * hs)) < hs))
        if j in MXU_LEVELS:
            sel.append(np.where(upper_t, (s > m) & (s <= t), (s > t) & (s <= m)))
    masks.append(t == s)
    sel.append(s <= t)
    masks = np.stack(masks).astype(np.float32)
    sel = np.stack(sel).astype(np.float32)
    both = lambda a: np.stack([a, a[:, ::-1, ::-1]])
    sel = both(sel).reshape(2, N_SEG * C, C)
    return both(masks), np.tile(sel, (1, 1, 3))


class _HgDirection:
    def __init__(self, q_ref, v_ref, g_ref, st_ref, o_ref, sel_ref, mask_ref, rev):
        C = HG_CHUNK
        self.q_ref, self.v_ref, self.st_ref, self.o_ref, self.rev = q_ref, v_ref, st_ref, o_ref, rev
        d = 1 if rev else 0
        self.sel = sel_ref.at[d]
        g = g_ref[0]
        self.g_parts = _split3(g)
        self.f = jnp.exp(g)
        self.k = 1.0 - self.f
        self.level_mask = [mask_ref[d, j] for j in range(N_LEVELS + 1)]
        n_pairs = q_ref.shape[1] // (2 * C)
        self.pairs = list(range(n_pairs - 1, -1, -1) if rev else range(n_pairs))
        self.st = st_ref[...]

    def decays(self, p):
        C = HG_CHUNK
        g_cat = jnp.concatenate(
            [jnp.concatenate([part[2 * p * C:(2 * p + 1) * C], part[(2 * p + 1) * C:(2 * p + 2) * C]], axis=1)
             for part in self.g_parts], axis=0)
        sums = _dot(self.sel[...], g_cat)
        parts = [sums[i * C:(i + 1) * C] for i in range(N_SEG - 1)]
        b = sums[(N_SEG - 1) * C:N_SEG * C]
        rows = {}

        def row(m):
            if m not in rows:
                rows[m] = jnp.broadcast_to(b[m:m + 1, :], (SUBLANES, 2 * HG_DK))
            return rows[m]

        for j in range(MXU_LEVELS[-1] + 1, N_LEVELS):
            hs = 1 << j
            groups = []
            for r in range(0, C, SUBLANES):
                start = r // (2 * hs) * (2 * hs)
                b_mid = row(start + (hs if self.rev else hs - 1))
                query_side = (r - start >= hs) != self.rev
                groups.append(b[r:r + SUBLANES] - b_mid if query_side else b_mid - b[r:r + SUBLANES])
            parts.append(jnp.concatenate(groups, axis=0))
        parts.append(b)
        parts.append(jnp.concatenate([row(0 if self.rev else C - 1)] * (C // SUBLANES), axis=0) - b)
        return [jnp.exp(x) for x in parts]

    def scores(self, p, e2):
        C = HG_CHUNK
        out = []
        for c in ((2 * p + 1, 2 * p) if self.rev else (2 * p, 2 * p + 1)):
            rows = slice(c * C, (c + 1) * C)
            lanes = slice((c - 2 * p) * HG_DK, (c - 2 * p + 1) * HG_DK)
            qc = self.q_ref[0, rows, :].astype(F32)
            kc = self.k[rows]
            score = lambda qf, kf: _dot_nt(qf.astype(BF16), kf.astype(BF16))
            k_next = pltpu.roll(kc, C - 1 if self.rev else 1, axis=0)
            a = (jnp.sum(qc * kc, axis=1, keepdims=True) * self.level_mask[N_LEVELS]
                 + jnp.sum(qc * self.f[rows] * k_next, axis=1, keepdims=True) * self.level_mask[0])
            for j in range(1, N_LEVELS):
                ej = e2[j - 1][:, lanes]
                a = a + score(qc * ej, kc * ej) * self.level_mask[j]
            e_b = e2[N_LEVELS - 1][:, lanes]
            total = e_b[0:1] if self.rev else e_b[C - 1:C]
            q_dec = jnp.concatenate([(qc * e_b).astype(BF16), a.astype(BF16)], axis=1)
            out.append((rows, q_dec, (kc * e2[N_LEVELS][:, lanes]).astype(BF16), total))
        return out

    def outputs(self, chunk_terms):
        for rows, q_dec, k_dec, total in chunk_terms:
            vc = self.v_ref[0, rows, :]
            self.o_ref[0, rows, :] = _dot(q_dec, jnp.concatenate([self.st.astype(BF16), vc], axis=0))
            total_col = jnp.transpose(jnp.broadcast_to(total, (SUBLANES, HG_DK)))[:, 0:1]
            self.st = total_col * self.st + _dot_tn(k_dec, vc)

    def finish(self):
        self.st_ref[...] = self.st


def _hgrn_kernel(qf_ref, vf_ref, gf_ref, qb_ref, vb_ref, gb_ref, s0f_ref, s0b_ref, sel_ref, mask_ref,
                 of_ref, ob_ref, stf, stb):
    @pl.when(pl.program_id(2) == 0)
    def _():
        stf[...] = s0f_ref[0, 0]
        stb[...] = s0b_ref[0, 0]

    fw = _HgDirection(qf_ref, vf_ref, gf_ref, stf, of_ref, sel_ref, mask_ref, False)
    bw = _HgDirection(qb_ref, vb_ref, gb_ref, stb, ob_ref, sel_ref, mask_ref, True)
    units = [(dirn, p) for pf, pb in zip(fw.pairs, bw.pairs) for dirn, p in ((fw, pf), (bw, pb))]
    e_next = units[0][0].decays(units[0][1])
    pending = []
    for i, (dirn, p) in enumerate(units):
        e_cur = e_next
        if i + 1 < len(units):
            e_next = units[i + 1][0].decays(units[i + 1][1])
        pending.append((dirn, dirn.scores(p, e_cur)))
        if len(pending) > HG_OUTPUT_LAG:
            done, terms = pending.pop(0)
            done.outputs(terms)
    for done, terms in pending:
        done.outputs(terms)
    fw.finish()
    bw.finish()


def _hgrn(q_hg, v_hg, g_fw, g_bw, s_fw, s_bw, tb):
    B, T, _ = q_hg.shape
    nb = T // tb
    fwd = pl.BlockSpec((1, tb, HG_DK), lambda b, h, i: (b, i, h))
    bwd = pl.BlockSpec((1, tb, HG_DK), lambda b, h, i: (b, nb - 1 - i, h))
    st = pl.BlockSpec((1, 1, HG_DK, HG_DK), lambda b, h, i: (b, h, 0, 0))
    masks, sel = _hg_consts()
    masks, sel = jnp.asarray(masks), jnp.asarray(sel, dtype=BF16)
    oshape = jax.ShapeDtypeStruct((B, T, HG_WIDTH), F32)
    return pl.pallas_call(
        _hgrn_kernel,
        grid=(B, HG_HEADS, nb),
        in_specs=[fwd, fwd, fwd, bwd, bwd, bwd, st, st,
                  pl.BlockSpec(sel.shape, lambda b, h, i: (0, 0, 0)),
                  pl.BlockSpec(masks.shape, lambda b, h, i: (0, 0, 0, 0))],
        out_specs=[fwd, bwd],
        out_shape=[oshape, oshape],
        scratch_shapes=[pltpu.VMEM((HG_DK, HG_DK), F32), pltpu.VMEM((HG_DK, HG_DK), F32)],
        compiler_params=pltpu.CompilerParams(
            dimension_semantics=("parallel", "parallel", "arbitrary"), vmem_limit_bytes=VMEM_LIMIT),
        name="hgrn",
    )(q_hg, v_hg, g_fw, q_hg, v_hg, g_bw, s_fw, s_bw, sel, masks)


def _na_bias_consts():
    W = GRID_W
    cols = np.arange(W)
    c0 = np.clip(cols - NA_KC // 2, 0, W - NA_KC)
    wk = np.arange(W)[None, :]
    valid = (wk >= c0[:, None]) & (wk < c0[:, None] + NA_KC)
    dc = wk - cols[:, None] + NA_KC - 1
    onehot = (dc[None] == np.arange(2 * NA_KC)[:, None, None]) & valid[None]
    onehot[2 * NA_KC - 1] = True
    neg = np.where(valid, 0.0, NEG).astype(np.float32).reshape(1, W * W)
    return onehot.reshape(2 * NA_KC, W * W).astype(np.float32), neg


def _na_bias_kernel(rpb_ref, onehot_ref, neg_ref, o_ref):
    o_ref[...] = _dot_exact_rhs(rpb_ref[...], onehot_ref[...]) + neg_ref[...]


def _na_plan(rows):
    assert rows % NA_STEP_ROWS == 0 and rows >= NA_WIN_ROWS and (rows - NA_WIN_ROWS) % NA_STEP_ROWS == 0
    variants, starts = [], []
    for i in range(rows // NA_STEP_ROWS):
        w0 = int(np.clip(NA_STEP_ROWS * i - NA_KR // 2, 0, rows - NA_WIN_ROWS))
        geom = tuple((int(np.clip(r - NA_KR // 2, 0, rows - NA_KR)) - w0, r - w0)
                     for r in range(NA_STEP_ROWS * i, NA_STEP_ROWS * (i + 1)))
        if not variants or variants[-1] != geom:
            assert geom not in variants
            variants.append(geom)
            starts.append(i)
    return variants, starts


def _na_bias(rpb, rows):
    W = GRID_W
    n_dr = 2 * NA_KR - 1
    n_dc = 2 * NA_KC
    onehot, neg = _na_bias_consts()
    variants, _ = _na_plan(rows)
    dr = [j - qrow + NA_KR - 1 if first <= j < first + NA_KR else n_dr
          for geom in variants for first, qrow in geom for j in range(NA_WIN_ROWS)]
    rpb_ext = jnp.zeros((NA_HEADS, n_dr + 1, n_dc), F32)
    rpb_ext = rpb_ext.at[:, :n_dr, :n_dc - 1].set(rpb.astype(F32)).at[:, n_dr, n_dc - 1].set(NEG)
    picked = jnp.take(rpb_ext, jnp.asarray(dr, dtype=jnp.int32), axis=1)
    picked = picked.transpose(1, 0, 2).reshape(len(dr) * NA_HEADS, n_dc)
    n_steps = 4
    assert picked.shape[0] % (n_steps * SUBLANES) == 0
    tile = picked.shape[0] // n_steps
    table = pl.pallas_call(
        _na_bias_kernel,
        grid=(n_steps,),
        in_specs=[pl.BlockSpec((tile, n_dc), lambda i: (i, 0)),
                  pl.BlockSpec(onehot.shape, lambda i: (0, 0)),
                  pl.BlockSpec(neg.shape, lambda i: (0, 0))],
        out_specs=pl.BlockSpec((tile, W * W), lambda i: (i, 0)),
        out_shape=jax.ShapeDtypeStruct((picked.shape[0], W * W), F32),
        name="na_bias",
    )(picked, jnp.asarray(onehot, dtype=BF16), jnp.asarray(neg))
    table = table.reshape(len(variants), NA_STEP_ROWS, NA_WIN_ROWS, NA_HEADS // NA_GROUP, NA_GROUP, W, W)
    table = table.transpose(0, 3, 4, 1, 5, 2, 6)
    return table.reshape(len(variants), NA_HEADS // NA_GROUP, NA_GROUP * NA_TOK, NA_WIN_ROWS * W)


def _natten_kernel(q_ref, kt_ref, v_ref, kct_ref, vc_ref, bias_ref, o_ref):
    W = GRID_W
    rows = v_ref.shape[1] // W
    i = pl.program_id(1)
    w0 = jnp.clip(NA_STEP_ROWS * i - NA_KR // 2, 0, rows - NA_WIN_ROWS)
    blk0 = w0 // NA_STEP_ROWS
    kt = jnp.concatenate([kt_ref[0, blk0 + j] for j in range(NA_WIN_ROWS // NA_STEP_ROWS)], axis=1)
    kct = jnp.concatenate([kct_ref[0, j] for j in range(kct_ref.shape[1])], axis=1)
    vw = v_ref[0, pl.ds(pl.multiple_of(w0 * W, NA_TOK), NA_WIN_ROWS * W), :]
    q = q_ref[0]
    rb = lax.broadcasted_iota(jnp.int32, (NA_GROUP * NA_TOK, NA_GW), 0) // NA_TOK
    cb = lax.broadcasted_iota(jnp.int32, (NA_GROUP * NA_TOK, NA_GW), 1) // NA_HEAD_DIM
    diag = rb == cb
    groups = [slice(grp * NA_GW, (grp + 1) * NA_GW) for grp in range(NA_HEADS // NA_GROUP)]
    logits = []
    for grp, gs in enumerate(groups):
        qg = q[:, gs]
        qbd = jnp.where(diag, jnp.concatenate([qg] * NA_GROUP, axis=0), jnp.zeros_like(qg[:1]))
        logits.append((_dot(qbd, kt[gs, :]) + bias_ref[0, grp], _dot(qbd, kct[gs, :])))
    for gs, (s_win, s_ctx) in zip(groups, logits):
        m = jnp.maximum(jnp.max(s_win, axis=-1, keepdims=True), jnp.max(s_ctx, axis=-1, keepdims=True))
        p_win = jnp.exp(s_win - m)
        p_ctx = jnp.exp(s_ctx - m)
        denom = jnp.sum(p_win, axis=-1, keepdims=True) + jnp.sum(p_ctx, axis=-1, keepdims=True)
        of = (_dot(p_win.astype(BF16), vw[:, gs]) + _dot(p_ctx.astype(BF16), vc_ref[0, :, gs])) / denom
        of = jnp.where(diag, of, 0.0)
        og = of[0:NA_TOK]
        for h in range(1, NA_GROUP):
            og = og + of[h * NA_TOK:(h + 1) * NA_TOK]
        o_ref[0, :, gs] = og.astype(o_ref.dtype)


def _natten(q, kt, v, kct, vc, bias):
    B, T, _ = q.shape
    rows = T // GRID_W
    _, starts = _na_plan(rows)
    tok = pl.BlockSpec((1, NA_TOK, NA_WIDTH), lambda b, i: (b, i, 0))
    whole = lambda a: pl.BlockSpec((1,) + a.shape[1:], lambda b, i: (b,) + (0,) * (a.ndim - 1))

    def bias_map(b, i):
        return (sum((i >= s).astype(jnp.int32) for s in starts[1:]), 0, 0, 0)

    return pl.pallas_call(
        _natten_kernel,
        grid=(B, rows // NA_STEP_ROWS),
        in_specs=[tok, whole(kt), whole(v), whole(kct), whole(vc),
                  pl.BlockSpec((1,) + bias.shape[1:], bias_map)],
        out_specs=tok,
        out_shape=jax.ShapeDtypeStruct((B, T, NA_WIDTH), BF16),
        compiler_params=pltpu.CompilerParams(
            dimension_semantics=("parallel", "arbitrary"), vmem_limit_bytes=VMEM_LIMIT),
        name="natten",
    )(q, kt, v, kct, vc, bias)


def _post_mix_kernel(yna_ref, of_ref, ob_ref, gate_ref, hgw_ref, wo_ref, x_ref, gtm_ref, nmix_ref,
                     scf_ref, shf_ref, gtf_ref, npre_ref, npost_ref, w1_ref, w2_ref, o_ref, act_ref):
    o = of_ref[0] + ob_ref[0]
    gate = gate_ref[0].astype(F32)
    hgw = hgw_ref[...]
    parts = []
    for h in range(HG_HEADS):
        sl = slice(h * HG_DK, (h + 1) * HG_DK)
        parts.append((_rms(o[:, sl], hgw) * gate[:, sl]).astype(BF16))
    y_hg = jnp.concatenate(parts, axis=-1)
    y = _dot(yna_ref[0], wo_ref[:NA_WIDTH, :]) + _dot(y_hg, wo_ref[NA_WIDTH:, :])
    x1 = x_ref[0] + gtm_ref[0] * _rms(y, nmix_ref[...])
    d_ff = w2_ref.shape[0]
    hb = (_rms(x1, npre_ref[...]) * (1.0 + scf_ref[0]) + shf_ref[0]).astype(BF16)
    for j in range(0, d_ff, FFN_COLS):
        gate_j = _dot(hb, w1_ref[:, j:j + FFN_COLS])
        up_j = _dot(hb, w1_ref[:, d_ff + j:d_ff + j + FFN_COLS])
        act_ref[:, j:j + FFN_COLS] = (_silu(gate_j) * up_j).astype(BF16)
    z = _dot(act_ref[...], w2_ref[...])
    o_ref[0] = x1 + gtf_ref[0] * _rms(z, npost_ref[...])


def _post_mix(y_na, o_fw, o_bw, gate, hgw, wo_bf, x, gt_m, nmix, sc_f, sh_f, gt_f, npre, npost, w1_bf, w2_bf, tm):
    B, T, _ = x.shape
    d_ff = w2_bf.shape[0]
    assert d_ff % FFN_COLS == 0
    tok = lambda n: pl.BlockSpec((1, tm, n), lambda b, i: (b, i, 0))
    mod = pl.BlockSpec((1, 1, D_MODEL), lambda b, i: (b, 0, 0))
    const = lambda shape: pl.BlockSpec(shape, lambda b, i: (0,) * len(shape), pipeline_mode=pl.Buffered(1))
    return pl.pallas_call(
        _post_mix_kernel,
        grid=(B, T // tm),
        in_specs=[tok(NA_WIDTH), tok(HG_WIDTH), tok(HG_WIDTH), tok(HG_WIDTH), const((1, HG_DK)),
                  const(wo_bf.shape), tok(D_MODEL), mod, const((1, D_MODEL)),
                  mod, mod, mod, const((1, D_MODEL)), const((1, D_MODEL)),
                  const(w1_bf.shape), const(w2_bf.shape)],
        out_specs=tok(D_MODEL),
        out_shape=jax.ShapeDtypeStruct(x.shape, F32),
        scratch_shapes=[pltpu.VMEM((tm, d_ff), BF16)],
        compiler_params=pltpu.CompilerParams(
            dimension_semantics=("parallel", "parallel"), vmem_limit_bytes=VMEM_LIMIT),
        name="post_mix",
    )(y_na, o_fw, o_bw, gate, hgw, wo_bf, x, gt_m, nmix, sc_f, sh_f, gt_f, npre, npost, w1_bf, w2_bf)


def kernel(x, c, ctx, c_ctx, w_ada, b_ada, norm_mix_pre, norm_mix_post, norm_ffn_pre, norm_ffn_post,
           w_in, na_rpb, hg_lb_logits, hg_norm_w, w_out, w_ffn_in, w_ffn_out):
    B, T, D = x.shape
    assert w_ada.shape[0] == 1, "single-layer stack"
    rows = T // GRID_W

    cv = jnp.zeros((8, D), F32).at[:B].set(c).at[B].set(c_ctx)
    mod = _ada(cv, w_ada[0], b_ada[0][None, :])
    sh_m, sc_m, gt_m, sh_f, sc_f, gt_f = [mod[:, i * D:(i + 1) * D] for i in range(N_MOD)]
    lat = lambda m: m[:B, None, :]
    cx = lambda m: jnp.broadcast_to(m[B][None, None, :], (B, 1, D))

    w_in_bf = w_in[0].astype(BF16)
    wkt_bf = w_in[0][:, NA_WIDTH:2 * NA_WIDTH].T.astype(BF16)
    lbl = hg_lb_logits.reshape(hg_lb_logits.shape[0], 2 * HG_WIDTH)
    nw_pre = norm_mix_pre[0][None, :]

    q_na, kt_na, v_na, q_hg, g_fw, g_bw, v_hg, gate = _in_proj(x, lat(sc_m), lat(sh_m), nw_pre, w_in_bf, wkt_bf,
                                                                lbl, 512)
    _, kt_c, v_c, _, g_cfw, g_cbw, vhg_c, _ = _in_proj(ctx, cx(sc_m), cx(sh_m), nw_pre, w_in_bf, wkt_bf, lbl,
                                                       ctx.shape[1])

    s_fw, s_bw = _ctx_state(g_cfw, g_cbw, vhg_c)
    o_fw, o_bw = _hgrn(q_hg, v_hg, g_fw, g_bw, s_fw, s_bw, 1024)

    y_na = _natten(q_na, kt_na, v_na, kt_c, v_c, _na_bias(na_rpb[0], rows))

    return _post_mix(y_na, o_fw, o_bw, gate, hg_norm_w[0][None, :], w_out[0].astype(BF16), x, lat(gt_m),
                     norm_mix_post[0][None, :], lat(sc_f), lat(sh_f), lat(gt_f), norm_ffn_pre[0][None, :],
                     norm_ffn_post[0][None, :], w_ffn_in[0].astype(BF16), w_ffn_out[0].astype(BF16), 512)
```

```python
import functools

import jax
import jax.numpy as jnp
import numpy as np
from jax import lax
from jax.experimental import pallas as pl
from jax.experimental.pallas import tpu as pltpu

D_MODEL = 1024
GRID_W = 64
NA_HEADS = 8
NA_HEAD_DIM = 64
NA_WIDTH = NA_HEADS * NA_HEAD_DIM
NA_KR = 8
NA_KC = 16
HG_HEADS = 4
HG_DK = 128
HG_WIDTH = HG_HEADS * HG_DK
HG_CHUNK = 64
N_LEVELS = 6
N_MOD = 6
EPS = 1e-6
NEG = -1e30

NA_GROUP = 4
NA_GW = NA_GROUP * NA_HEAD_DIM
NA_STEP_ROWS = 2
NA_TOK = NA_STEP_ROWS * GRID_W
NA_WIN_ROWS = NA_KR + NA_STEP_ROWS

F32 = jnp.float32
BF16 = jnp.bfloat16

VMEM_LIMIT = 56 * 1024 * 1024
FFN_COLS = 256


def _silu(x):
    return x * jax.nn.sigmoid(x)


def _dot(a, b):
    return jnp.dot(a, b, preferred_element_type=F32)


def _dot_nt(a, b):
    return lax.dot_general(a, b, (((1,), (1,)), ((), ())), preferred_element_type=F32)


def _dot_tn(a, b):
    return lax.dot_general(a, b, (((0,), (0,)), ((), ())), preferred_element_type=F32)


def _split3(x):
    x1 = x.astype(BF16)
    r1 = x - x1.astype(F32)
    x2 = r1.astype(BF16)
    r2 = r1 - x2.astype(F32)
    return x1, x2, r2.astype(BF16)


def _dot_exact_lhs(t, x):
    x1, x2, x3 = _split3(x)
    return _dot(t, x1) + _dot(t, x2) + _dot(t, x3)


def _rms(x, w):
    return x * lax.rsqrt(jnp.mean(x * x, axis=-1, keepdims=True) + EPS) * w


def _ada_kernel(cv_ref, w_ref, b_ref, o_ref):
    s = _silu(cv_ref[...])
    s1, s2, s3 = _split3(s)
    w1, w2, w3 = _split3(w_ref[...])
    acc = _dot(s1, w1) + (_dot(s1, w2) + _dot(s2, w1)) + (_dot(s1, w3) + _dot(s2, w2) + _dot(s3, w1))
    o_ref[...] = acc + b_ref[...]


def _ada(cv, w_ada, b_ada):
    n = w_ada.shape[1]
    tn = 1536
    return pl.pallas_call(
        _ada_kernel,
        grid=(n // tn,),
        in_specs=[
            pl.BlockSpec((8, D_MODEL), lambda j: (0, 0)),
            pl.BlockSpec((D_MODEL, tn), lambda j: (0, j)),
            pl.BlockSpec((1, tn), lambda j: (0, j)),
        ],
        out_specs=pl.BlockSpec((8, tn), lambda j: (0, j)),
        out_shape=jax.ShapeDtypeStruct((8, n), F32),
        compiler_params=pltpu.CompilerParams(vmem_limit_bytes=VMEM_LIMIT),
        name="ada",
    )(cv, w_ada, b_ada)


def _in_proj_kernel(x_ref, sc_ref, sh_ref, nw_ref, w_ref, lbl_ref,
                    qna_ref, knat_ref, vna_ref, qhg_ref, gfw_ref, gbw_ref, vhg_ref, gate_ref):
    x = x_ref[0]
    h = _rms(x, nw_ref[...]) * (1.0 + sc_ref[0]) + sh_ref[0]
    hb = h.astype(BF16)

    def proj(i):
        return _dot(hb, w_ref[:, i * 512:(i + 1) * 512])

    lbl = lbl_ref[...]
    e = jnp.exp(lbl - jnp.max(lbl, axis=0, keepdims=True))
    lb = e[0:1] / jnp.sum(e, axis=0, keepdims=True)

    qna_ref[0] = (proj(0) * (NA_HEAD_DIM ** -0.5)).astype(BF16)
    k_t = proj(1).astype(BF16).T
    for j in range(knat_ref.shape[1]):
        knat_ref[0, j] = k_t[:, j * NA_TOK:(j + 1) * NA_TOK]
    vna_ref[0] = proj(2).astype(BF16)
    qhg_ref[0] = _silu(proj(3)).astype(BF16)
    lb_f = lb[:, :HG_WIDTH]
    lb_b = lb[:, HG_WIDTH:]
    gfw_ref[0] = jnp.log(lb_f + (1.0 - lb_f) * jax.nn.sigmoid(proj(4)))
    gbw_ref[0] = jnp.log(lb_b + (1.0 - lb_b) * jax.nn.sigmoid(proj(5)))
    vhg_ref[0] = proj(6).astype(BF16)
    gate_ref[0] = _silu(proj(7)).astype(BF16)


def _in_proj(x, sc, sh, nw, w_bf, lbl, tm):
    B, T, _ = x.shape
    tok = lambda b, i: (b, i, 0)
    out_bf = jax.ShapeDtypeStruct((B, T, 512), BF16)
    out_f = jax.ShapeDtypeStruct((B, T, 512), F32)
    ospec = pl.BlockSpec((1, tm, 512), tok)
    return pl.pallas_call(
        _in_proj_kernel,
        grid=(B, T // tm),
        in_specs=[
            pl.BlockSpec((1, tm, D_MODEL), tok),
            pl.BlockSpec((1, 1, D_MODEL), lambda b, i: (b, 0, 0)),
            pl.BlockSpec((1, 1, D_MODEL), lambda b, i: (b, 0, 0)),
            pl.BlockSpec((1, D_MODEL), lambda b, i: (0, 0)),
            pl.BlockSpec(w_bf.shape, lambda b, i: (0, 0)),
            pl.BlockSpec(lbl.shape, lambda b, i: (0, 0)),
        ],
        out_specs=[ospec, pl.BlockSpec((1, tm // NA_TOK, NA_WIDTH, NA_TOK), lambda b, i: (b, i, 0, 0))]
        + [ospec] * 6,
        out_shape=[out_bf, jax.ShapeDtypeStruct((B, T // NA_TOK, NA_WIDTH, NA_TOK), BF16),
                   out_bf, out_bf, out_f, out_f, out_bf, out_bf],
        compiler_params=pltpu.CompilerParams(
            dimension_semantics=("parallel", "parallel"), vmem_limit_bytes=VMEM_LIMIT),
        name="in_proj",
    )(x, sc, sh, nw, w_bf, lbl)


def _ctx_state_kernel(gfw_ref, gbw_ref, v_ref, sfw_ref, sbw_ref):
    L = gfw_ref.shape[1]
    r = lax.broadcasted_iota(jnp.int32, (L, L), 0)
    c = lax.broadcasted_iota(jnp.int32, (L, L), 1)
    upper = jnp.where(c > r, 1.0, 0.0).astype(BF16)
    lower = jnp.where(c < r, 1.0, 0.0).astype(BF16)
    v = v_ref[0]
    for g_ref, tri, s_ref in ((gfw_ref, upper, sfw_ref), (gbw_ref, lower, sbw_ref)):
        g = g_ref[0]
        kw = ((1.0 - jnp.exp(g)) * jnp.exp(_dot_exact_lhs(tri, g))).astype(BF16)
        for h in range(HG_HEADS):
            sl = slice(h * HG_DK, (h + 1) * HG_DK)
            s_ref[0, h] = _dot_tn(kw[:, sl], v[:, sl])


def _ctx_state(g_cfw, g_cbw, v_c):
    B, L, _ = g_cfw.shape
    tok = pl.BlockSpec((1, L, HG_WIDTH), lambda b: (b, 0, 0))
    st = pl.BlockSpec((1, HG_HEADS, HG_DK, HG_DK), lambda b: (b, 0, 0, 0))
    sshape = jax.ShapeDtypeStruct((B, HG_HEADS, HG_DK, HG_DK), F32)
    return pl.pallas_call(
        _ctx_state_kernel,
        grid=(B,),
        in_specs=[tok, tok, tok],
        out_specs=[st, st],
        out_shape=[sshape, sshape],
        compiler_params=pltpu.CompilerParams(vmem_limit_bytes=VMEM_LIMIT),
        name="ctx_state",
    )(g_cfw, g_cbw, v_c)


SUBLANES = 8
MXU_LEVELS = [j for j in range(1, N_LEVELS) if (1 << j) < SUBLANES]
N_SEG = len(MXU_LEVELS) + 1
HG_OUTPUT_LAG = 2


def _hg_consts():
    C = HG_CHUNK
    t = np.arange(C)[:, None]
    s = np.arange(C)[None, :]
    masks, sel = [], []
    for j in range(N_LEVELS):
        hs = 1 << j
        same = (t // (2 * hs)) == (s // (2 * hs))
        upper_t = (t % (2 * hs)) >= hs
        m = (t // (2 * hs)) * (2 * hs) + hs - 1
        masks.append(same & upper_t & ((s % (2 * hs)) < hs))
        if j in MXU_LEVELS:
            sel.append(np.where(upper_t, (s > m) & (s <= t), (s > t) & (s <= m)))
    masks.append(t == s)
    sel.append(s <= t)
    masks = np.stack(masks).astype(np.float32)
    sel = np.stack(sel).astype(np.float32)
    both = lambda a: np.stack([a, a[:, ::-1, ::-1]])
    sel = both(sel).reshape(2, N_SEG * C, C)
    return both(masks), np.tile(sel, (1, 1, 3))


class _HgDirection:
    def __init__(self, q_ref, v_ref, g_ref, st_ref, o_ref, sel_ref, mask_ref, rev):
        C = HG_CHUNK
        self.q_ref, self.v_ref, self.st_ref, self.o_ref, self.rev = q_ref, v_ref, st_ref, o_ref, rev
        d = 1 if rev else 0
        self.sel = sel_ref.at[d]
        g = g_ref[0]
        self.g_parts = _split3(g)
        self.f = jnp.exp(g)
        self.k = 1.0 - self.f
        self.level_mask = [mask_ref[d, j] for j in range(N_LEVELS + 1)]
        n_pairs = q_ref.shape[1] // (2 * C)
        self.pairs = list(range(n_pairs - 1, -1, -1) if rev else range(n_pairs))
        self.st = st_ref[...]

    def decays(self, p):
        C = HG_CHUNK
        g_cat = jnp.concatenate(
            [jnp.concatenate([part[2 * p * C:(2 * p + 1) * C], part[(2 * p + 1) * C:(2 * p + 2) * C]], axis=1)
             for part in self.g_parts], axis=0)
        sums = _dot(self.sel[...], g_cat)
        parts = [sums[i * C:(i + 1) * C] for i in range(N_SEG - 1)]
        b = sums[(N_SEG - 1) * C:N_SEG * C]
        rows = {}

        def row(m):
            if m not in rows:
                rows[m] = jnp.broadcast_to(b[m:m + 1, :], (SUBLANES, 2 * HG_DK))
            return rows[m]

        for j in range(MXU_LEVELS[-1] + 1, N_LEVELS):
            hs = 1 << j
            groups = []
            for r in range(0, C, SUBLANES):
                start = r // (2 * hs) * (2 * hs)
                b_mid = row(start + (hs if self.rev else hs - 1))
                query_side = (r - start >= hs) != self.rev
                groups.append(b[r:r + SUBLANES] - b_mid if query_side else b_mid - b[r:r + SUBLANES])
            parts.append(jnp.concatenate(groups, axis=0))
        parts.append(b)
        parts.append(jnp.concatenate([row(0 if self.rev else C - 1)] * (C // SUBLANES), axis=0) - b)
        return [jnp.exp(x) for x in parts]

    def scores(self, p, e2):
        C = HG_CHUNK
        out = []
        for c in ((2 * p + 1, 2 * p) if self.rev else (2 * p, 2 * p + 1)):
            rows = slice(c * C, (c + 1) * C)
            lanes = slice((c - 2 * p) * HG_DK, (c - 2 * p + 1) * HG_DK)
            qc = self.q_ref[0, rows, :].astype(F32)
            kc = self.k[rows]
            score = lambda qf, kf: _dot_nt(qf.astype(BF16), kf.astype(BF16))
            k_next = pltpu.roll(kc, C - 1 if self.rev else 1, axis=0)
            a = (jnp.sum(qc * kc, axis=1, keepdims=True) * self.level_mask[N_LEVELS]
                 + jnp.sum(qc * self.f[rows] * k_next, axis=1, keepdims=True) * self.level_mask[0])
            for j in range(1, N_LEVELS):
                ej = e2[j - 1][:, lanes]
                a = a + score(qc * ej, kc * ej) * self.level_mask[j]
            e_b = e2[N_LEVELS - 1][:, lanes]
            total = e_b[0:1] if self.rev else e_b[C - 1:C]
            q_dec = jnp.concatenate([(qc * e_b).astype(BF16), a.astype(BF16)], axis=1)
            out.append((rows, q_dec, (kc * e2[N_LEVELS][:, lanes]).astype(BF16), total))
        return out

    def outputs(self, chunk_terms):
        for rows, q_dec, k_dec, total in chunk_terms:
            vc = self.v_ref[0, rows, :]
            self.o_ref[0, rows, :] = _dot(q_dec, jnp.concatenate([self.st.astype(BF16), vc], axis=0))
            total_col = jnp.transpose(jnp.broadcast_to(total, (SUBLANES, HG_DK)))[:, 0:1]
            self.st = total_col * self.st + _dot_tn(k_dec, vc)

    def finish(self):
        self.st_ref[...] = self.st


def _hgrn_kernel(qf_ref, vf_ref, gf_ref, qb_ref, vb_ref, gb_ref, s0f_ref, s0b_ref, sel_ref, mask_ref,
                 of_ref, ob_ref, stf, stb):
    @pl.when(pl.program_id(2) == 0)
    def _():
        stf[...] = s0f_ref[0, 0]
        stb[...] = s0b_ref[0, 0]

    fw = _HgDirection(qf_ref, vf_ref, gf_ref, stf, of_ref, sel_ref, mask_ref, False)
    bw = _HgDirection(qb_ref, vb_ref, gb_ref, stb, ob_ref, sel_ref, mask_ref, True)
    units = [(dirn, p) for pf, pb in zip(fw.pairs, bw.pairs) for dirn, p in ((fw, pf), (bw, pb))]
    e_next = units[0][0].decays(units[0][1])
    pending = []
    for i, (dirn, p) in enumerate(units):
        e_cur = e_next
        if i + 1 < len(units):
            e_next = units[i + 1][0].decays(units[i + 1][1])
        pending.append((dirn, dirn.scores(p, e_cur)))
        if len(pending) > HG_OUTPUT_LAG:
            done, terms = pending.pop(0)
            done.outputs(terms)
    for done, terms in pending:
        done.outputs(terms)
    fw.finish()
    bw.finish()


def _hgrn(q_hg, v_hg, g_fw, g_bw, s_fw, s_bw, tb):
    B, T, _ = q_hg.shape
    nb = T // tb
    fwd = pl.BlockSpec((1, tb, HG_DK), lambda b, h, i: (b, i, h))
    bwd = pl.BlockSpec((1, tb, HG_DK), lambda b, h, i: (b, nb - 1 - i, h))
    st = pl.BlockSpec((1, 1, HG_DK, HG_DK), lambda b, h, i: (b, h, 0, 0))
    masks, sel = _hg_consts()
    masks, sel = jnp.asarray(masks), jnp.asarray(sel, dtype=BF16)
    oshape = jax.ShapeDtypeStruct((B, T, HG_WIDTH), F32)
    return pl.pallas_call(
        _hgrn_kernel,
        grid=(B, HG_HEADS, nb),
        in_specs=[fwd, fwd, fwd, bwd, bwd, bwd, st, st,
                  pl.BlockSpec(sel.shape, lambda b, h, i: (0, 0, 0)),
                  pl.BlockSpec(masks.shape, lambda b, h, i: (0, 0, 0, 0))],
        out_specs=[fwd, bwd],
        out_shape=[oshape, oshape],
        scratch_shapes=[pltpu.VMEM((HG_DK, HG_DK), F32), pltpu.VMEM((HG_DK, HG_DK), F32)],
        compiler_params=pltpu.CompilerParams(
            dimension_semantics=("parallel", "parallel", "arbitrary"), vmem_limit_bytes=VMEM_LIMIT),
        name="hgrn",
    )(q_hg, v_hg, g_fw, q_hg, v_hg, g_bw, s_fw, s_bw, sel, masks)


def _na_bias_kernel(taps_ref, valid_ref, o_ref):
    W = GRID_W
    for i in range(taps_ref.shape[2]):
        taps = jnp.broadcast_to(taps_ref[0, 0, i:i + 1, :], (W, taps_ref.shape[3]))
        shifted = pltpu.roll(taps, 0, axis=1, stride=1, stride_axis=0)
        o_ref[0, 0, i * W:(i + 1) * W, :] = jnp.where(valid_ref[0, i % NA_STEP_ROWS] > 0.0, shifted, NEG)


def _na_plan(rows):
    assert rows % NA_STEP_ROWS == 0 and rows >= NA_WIN_ROWS and (rows - NA_WIN_ROWS) % NA_STEP_ROWS == 0
    variants, starts = [], []
    for i in range(rows // NA_STEP_ROWS):
        w0 = int(np.clip(NA_STEP_ROWS * i - NA_KR // 2, 0, rows - NA_WIN_ROWS))
        geom = tuple((int(np.clip(r - NA_KR // 2, 0, rows - NA_KR)) - w0, r - w0)
                     for r in range(NA_STEP_ROWS * i, NA_STEP_ROWS * (i + 1)))
        if not variants or variants[-1] != geom:
            assert geom not in variants
            variants.append(geom)
            starts.append(i)
    return variants, starts


def _na_bias(rpb, rows):
    W = GRID_W
    n_dr = 2 * NA_KR - 1
    variants, _ = _na_plan(rows)
    dr = np.array([[[j - qrow + NA_KR - 1 if first <= j < first + NA_KR else n_dr for j in range(NA_WIN_ROWS)]
                    for first, qrow in geom] for geom in variants])
    cols = np.arange(W)
    c0 = np.clip(cols - NA_KC // 2, 0, W - NA_KC)
    col_ok = (cols[None, :] >= c0[:, None]) & (cols[None, :] < c0[:, None] + NA_KC)
    valid = (dr < n_dr)[:, :, None, :, None] & col_ok[None, None, :, None, :]
    valid = valid.reshape(len(variants), NA_STEP_ROWS, W, NA_WIN_ROWS * W).astype(np.float32)

    rpb_ext = jnp.zeros((NA_HEADS, n_dr + 1, 2 * NA_KC - 1), F32).at[:, :n_dr].set(rpb.astype(F32))
    picked = jnp.take(rpb_ext, jnp.asarray(dr.reshape(-1), dtype=jnp.int32), axis=1)
    picked = picked.reshape((NA_HEADS,) + dr.shape + (2 * NA_KC - 1,))
    ahead = picked[..., NA_KC - 1:]
    behind = jnp.roll(picked, -1, axis=3)[..., :NA_KC - 1]
    gap = jnp.zeros(picked.shape[:-1] + (W - 2 * NA_KC + 1,), F32)
    taps = jnp.concatenate([ahead, gap, behind], axis=-1)
    taps = taps.reshape(NA_HEADS // NA_GROUP, NA_GROUP, len(variants), NA_STEP_ROWS, NA_WIN_ROWS * W)
    taps = taps.transpose(2, 0, 1, 3, 4).reshape(len(variants), NA_HEADS // NA_GROUP, NA_GROUP * NA_STEP_ROWS,
                                                 NA_WIN_ROWS * W)
    out_block = (1, 1, NA_GROUP * NA_TOK, NA_WIN_ROWS * W)
    return pl.pallas_call(
        _na_bias_kernel,
        grid=(len(variants), NA_HEADS // NA_GROUP),
        in_specs=[pl.BlockSpec((1, 1) + taps.shape[2:], lambda v, g: (v, g, 0, 0)),
                  pl.BlockSpec((1,) + valid.shape[1:], lambda v, g: (v, 0, 0, 0))],
        out_specs=pl.BlockSpec(out_block, lambda v, g: (v, g, 0, 0)),
        out_shape=jax.ShapeDtypeStruct((len(variants), NA_HEADS // NA_GROUP) + out_block[2:], F32),
        name="na_bias",
    )(taps, jnp.asarray(valid))


def _natten_kernel(q_ref, kt_ref, v_ref, kct_ref, vc_ref, bias_ref, o_ref):
    W = GRID_W
    rows = v_ref.shape[1] // W
    i = pl.program_id(1)
    w0 = jnp.clip(NA_STEP_ROWS * i - NA_KR // 2, 0, rows - NA_WIN_ROWS)
    blk0 = w0 // NA_STEP_ROWS
    kt = jnp.concatenate([kt_ref[0, blk0 + j] for j in range(NA_WIN_ROWS // NA_STEP_ROWS)], axis=1)
    kct = jnp.concatenate([kct_ref[0, j] for j in range(kct_ref.shape[1])], axis=1)
    vw = v_ref[0, pl.ds(pl.multiple_of(w0 * W, NA_TOK), NA_WIN_ROWS * W), :]
    q = q_ref[0]
    rb = lax.broadcasted_iota(jnp.int32, (NA_GROUP * NA_TOK, NA_GW), 0) // NA_TOK
    cb = lax.broadcasted_iota(jnp.int32, (NA_GROUP * NA_TOK, NA_GW), 1) // NA_HEAD_DIM
    diag = rb == cb
    groups = [slice(grp * NA_GW, (grp + 1) * NA_GW) for grp in range(NA_HEADS // NA_GROUP)]
    logits = []
    for grp, gs in enumerate(groups):
        qg = q[:, gs]
        qbd = jnp.where(diag, jnp.concatenate([qg] * NA_GROUP, axis=0), jnp.zeros_like(qg[:1]))
        logits.append((_dot(qbd, kt[gs, :]) + bias_ref[0, grp], _dot(qbd, kct[gs, :])))
    for gs, (s_win, s_ctx) in zip(groups, logits):
        m = jnp.maximum(jnp.max(s_win, axis=-1, keepdims=True), jnp.max(s_ctx, axis=-1, keepdims=True))
        p_win = jnp.exp(s_win - m)
        p_ctx = jnp.exp(s_ctx - m)
        denom = jnp.sum(p_win, axis=-1, keepdims=True) + jnp.sum(p_ctx, axis=-1, keepdims=True)
        of = (_dot(p_win.astype(BF16), vw[:, gs]) + _dot(p_ctx.astype(BF16), vc_ref[0, :, gs])) / denom
        of = jnp.where(diag, of, 0.0)
        og = of[0:NA_TOK]
        for h in range(1, NA_GROUP):
            og = og + of[h * NA_TOK:(h + 1) * NA_TOK]
        o_ref[0, :, gs] = og.astype(o_ref.dtype)


def _natten(q, kt, v, kct, vc, bias):
    B, T, _ = q.shape
    rows = T // GRID_W
    _, starts = _na_plan(rows)
    tok = pl.BlockSpec((1, NA_TOK, NA_WIDTH), lambda b, i: (b, i, 0))
    whole = lambda a: pl.BlockSpec((1,) + a.shape[1:], lambda b, i: (b,) + (0,) * (a.ndim - 1))

    def bias_map(b, i):
        return (sum((i >= s).astype(jnp.int32) for s in starts[1:]), 0, 0, 0)

    return pl.pallas_call(
        _natten_kernel,
        grid=(B, rows // NA_STEP_ROWS),
        in_specs=[tok, whole(kt), whole(v), whole(kct), whole(vc),
                  pl.BlockSpec((1,) + bias.shape[1:], bias_map)],
        out_specs=tok,
        out_shape=jax.ShapeDtypeStruct((B, T, NA_WIDTH), BF16),
        compiler_params=pltpu.CompilerParams(
            dimension_semantics=("parallel", "arbitrary"), vmem_limit_bytes=VMEM_LIMIT),
        name="natten",
    )(q, kt, v, kct, vc, bias)


def _post_mix_kernel(yna_ref, of_ref, ob_ref, gate_ref, hgw_ref, wo_ref, x_ref, gtm_ref, nmix_ref,
                     scf_ref, shf_ref, gtf_ref, npre_ref, npost_ref, w1_ref, w2_ref, o_ref, act_ref):
    o = of_ref[0] + ob_ref[0]
    gate = gate_ref[0].astype(F32)
    hgw = hgw_ref[...]
    parts = []
    for h in range(HG_HEADS):
        sl = slice(h * HG_DK, (h + 1) * HG_DK)
        parts.append((_rms(o[:, sl], hgw) * gate[:, sl]).astype(BF16))
    y_hg = jnp.concatenate(parts, axis=-1)
    y = _dot(yna_ref[0], wo_ref[:NA_WIDTH, :]) + _dot(y_hg, wo_ref[NA_WIDTH:, :])
    x1 = x_ref[0] + gtm_ref[0] * _rms(y, nmix_ref[...])
    d_ff = w2_ref.shape[0]
    hb = (_rms(x1, npre_ref[...]) * (1.0 + scf_ref[0]) + shf_ref[0]).astype(BF16)
    for j in range(0, d_ff, FFN_COLS):
        gate_j = _dot(hb, w1_ref[:, j:j + FFN_COLS])
        up_j = _dot(hb, w1_ref[:, d_ff + j:d_ff + j + FFN_COLS])
        act_ref[:, j:j + FFN_COLS] = (_silu(gate_j) * up_j).astype(BF16)
    z = _dot(act_ref[...], w2_ref[...])
    o_ref[0] = x1 + gtf_ref[0] * _rms(z, npost_ref[...])


def _post_mix(y_na, o_fw, o_bw, gate, hgw, wo_bf, x, gt_m, nmix, sc_f, sh_f, gt_f, npre, npost, w1_bf, w2_bf, tm):
    B, T, _ = x.shape
    d_ff = w2_bf.shape[0]
    assert d_ff % FFN_COLS == 0
    tok = lambda n: pl.BlockSpec((1, tm, n), lambda b, i: (b, i, 0))
    mod = pl.BlockSpec((1, 1, D_MODEL), lambda b, i: (b, 0, 0))
    const = lambda shape: pl.BlockSpec(shape, lambda b, i: (0,) * len(shape), pipeline_mode=pl.Buffered(1))
    return pl.pallas_call(
        _post_mix_kernel,
        grid=(B, T // tm),
        in_specs=[tok(NA_WIDTH), tok(HG_WIDTH), tok(HG_WIDTH), tok(HG_WIDTH), const((1, HG_DK)),
                  const(wo_bf.shape), tok(D_MODEL), mod, const((1, D_MODEL)),
                  mod, mod, mod, const((1, D_MODEL)), const((1, D_MODEL)),
                  const(w1_bf.shape), const(w2_bf.shape)],
        out_specs=tok(D_MODEL),
        out_shape=jax.ShapeDtypeStruct(x.shape, F32),
        scratch_shapes=[pltpu.VMEM((tm, d_ff), BF16)],
        compiler_params=pltpu.CompilerParams(
            dimension_semantics=("parallel", "parallel"), vmem_limit_bytes=VMEM_LIMIT),
        name="post_mix",
    )(y_na, o_fw, o_bw, gate, hgw, wo_bf, x, gt_m, nmix, sc_f, sh_f, gt_f, npre, npost, w1_bf, w2_bf)


def kernel(x, c, ctx, c_ctx, w_ada, b_ada, norm_mix_pre, norm_mix_post, norm_ffn_pre, norm_ffn_post,
           w_in, na_rpb, hg_lb_logits, hg_norm_w, w_out, w_ffn_in, w_ffn_out):
    B, T, D = x.shape
    assert w_ada.shape[0] == 1, "single-layer stack"
    rows = T // GRID_W

    cv = jnp.zeros((8, D), F32).at[:B].set(c).at[B].set(c_ctx)
    mod = _ada(cv, w_ada[0], b_ada[0][None, :])
    sh_m, sc_m, gt_m, sh_f, sc_f, gt_f = [mod[:, i * D:(i + 1) * D] for i in range(N_MOD)]
    lat = lambda m: m[:B, None, :]
    cx = lambda m: jnp.broadcast_to(m[B][None, None, :], (B, 1, D))

    w_in_bf = w_in[0].astype(BF16)
    lbl = hg_lb_logits.reshape(hg_lb_logits.shape[0], 2 * HG_WIDTH)
    nw_pre = norm_mix_pre[0][None, :]

    q_na, kt_na, v_na, q_hg, g_fw, g_bw, v_hg, gate = _in_proj(x, lat(sc_m), lat(sh_m), nw_pre, w_in_bf, lbl, 512)
    _, kt_c, v_c, _, g_cfw, g_cbw, vhg_c, _ = _in_proj(ctx, cx(sc_m), cx(sh_m), nw_pre, w_in_bf, lbl,
                                                       ctx.shape[1])

    s_fw, s_bw = _ctx_state(g_cfw, g_cbw, vhg_c)
    o_fw, o_bw = _hgrn(q_hg, v_hg, g_fw, g_bw, s_fw, s_bw, 1024)

    y_na = _natten(q_na, kt_na, v_na, kt_c, v_c, _na_bias(na_rpb[0], rows))

    return _post_mix(y_na, o_fw, o_bw, gate, hg_norm_w[0][None, :], w_out[0].astype(BF16), x, lat(gt_m),
                     norm_mix_post[0][None, :], lat(sc_f), lat(sh_f), lat(gt_f), norm_ffn_pre[0][None, :],
                     norm_ffn_post[0][None, :], w_ffn_in[0].astype(BF16), w_ffn_out[0].astype(BF16), 512)
```

```python
import functools

import jax
import jax.numpy as jnp
import numpy as np
from jax import lax
from jax.experimental import pallas as pl
from jax.experimental.pallas import tpu as pltpu

D_MODEL = 1024
GRID_W = 64
NA_HEADS = 8
NA_HEAD_DIM = 64
NA_WIDTH = NA_HEADS * NA_HEAD_DIM
NA_KR = 8
NA_KC = 16
HG_HEADS = 4
HG_DK = 128
HG_WIDTH = HG_HEADS * HG_DK
HG_CHUNK = 64
N_LEVELS = 6
N_MOD = 6
EPS = 1e-6
NEG = -1e30

NA_GROUP = 4
NA_GW = NA_GROUP * NA_HEAD_DIM
NA_STEP_ROWS = 2
NA_TOK = NA_STEP_ROWS * GRID_W
NA_WIN_ROWS = NA_KR + NA_STEP_ROWS
NA_STEP_PAIRS = 2

F32 = jnp.float32
BF16 = jnp.bfloat16

VMEM_LIMIT = 56 * 1024 * 1024
FFN_COLS = 256


def _silu(x):
    return x * jax.nn.sigmoid(x)


def _dot(a, b):
    return jnp.dot(a, b, preferred_element_type=F32)


def _dot_nt(a, b):
    return lax.dot_general(a, b, (((1,), (1,)), ((), ())), preferred_element_type=F32)


def _dot_tn(a, b):
    return lax.dot_general(a, b, (((0,), (0,)), ((), ())), preferred_element_type=F32)


def _split3(x):
    x1 = x.astype(BF16)
    r1 = x - x1.astype(F32)
    x2 = r1.astype(BF16)
    r2 = r1 - x2.astype(F32)
    return x1, x2, r2.astype(BF16)


def _dot_exact_lhs(t, x):
    x1, x2, x3 = _split3(x)
    return _dot(t, x1) + _dot(t, x2) + _dot(t, x3)


def _rms(x, w):
    return x * lax.rsqrt(jnp.mean(x * x, axis=-1, keepdims=True) + EPS) * w


def _ada_kernel(cv_ref, w_ref, b_ref, o_ref):
    s = _silu(cv_ref[...])
    s1, s2, s3 = _split3(s)
    w1, w2, w3 = _split3(w_ref[...])
    acc = _dot(s1, w1) + (_dot(s1, w2) + _dot(s2, w1)) + (_dot(s1, w3) + _dot(s2, w2) + _dot(s3, w1))
    o_ref[...] = acc + b_ref[...]


def _ada(cv, w_ada, b_ada):
    n = w_ada.shape[1]
    tn = 1536
    return pl.pallas_call(
        _ada_kernel,
        grid=(n // tn,),
        in_specs=[
            pl.BlockSpec((8, D_MODEL), lambda j: (0, 0)),
            pl.BlockSpec((D_MODEL, tn), lambda j: (0, j)),
            pl.BlockSpec((1, tn), lambda j: (0, j)),
        ],
        out_specs=pl.BlockSpec((8, tn), lambda j: (0, j)),
        out_shape=jax.ShapeDtypeStruct((8, n), F32),
        compiler_params=pltpu.CompilerParams(vmem_limit_bytes=VMEM_LIMIT),
        name="ada",
    )(cv, w_ada, b_ada)


def _in_proj_kernel(x_ref, sc_ref, sh_ref, nw_ref, w_ref, lbl_ref,
                    qna_ref, knat_ref, vna_ref, qhg_ref, gfw_ref, gbw_ref, vhg_ref, gate_ref):
    x = x_ref[0]
    h = _rms(x, nw_ref[...]) * (1.0 + sc_ref[0]) + sh_ref[0]
    hb = h.astype(BF16)

    def proj(i):
        return _dot(hb, w_ref[:, i * 512:(i + 1) * 512])

    lbl = lbl_ref[...]
    e = jnp.exp(lbl - jnp.max(lbl, axis=0, keepdims=True))
    lb = e[0:1] / jnp.sum(e, axis=0, keepdims=True)

    qna_ref[0] = (proj(0) * (NA_HEAD_DIM ** -0.5)).astype(BF16)
    k_t = proj(1).astype(BF16).T
    for j in range(knat_ref.shape[1]):
        knat_ref[0, j] = k_t[:, j * NA_TOK:(j + 1) * NA_TOK]
    vna_ref[0] = proj(2).astype(BF16)
    qhg_ref[0] = _silu(proj(3)).astype(BF16)
    lb_f = lb[:, :HG_WIDTH]
    lb_b = lb[:, HG_WIDTH:]
    gfw_ref[0] = jnp.log(lb_f + (1.0 - lb_f) * jax.nn.sigmoid(proj(4)))
    gbw_ref[0] = jnp.log(lb_b + (1.0 - lb_b) * jax.nn.sigmoid(proj(5)))
    vhg_ref[0] = proj(6).astype(BF16)
    gate_ref[0] = _silu(proj(7)).astype(BF16)


def _in_proj(x, sc, sh, nw, w_bf, lbl, tm):
    B, T, _ = x.shape
    tok = lambda b, i: (b, i, 0)
    out_bf = jax.ShapeDtypeStruct((B, T, 512), BF16)
    out_f = jax.ShapeDtypeStruct((B, T, 512), F32)
    ospec = pl.BlockSpec((1, tm, 512), tok)
    return pl.pallas_call(
        _in_proj_kernel,
        grid=(B, T // tm),
        in_specs=[
            pl.BlockSpec((1, tm, D_MODEL), tok),
            pl.BlockSpec((1, 1, D_MODEL), lambda b, i: (b, 0, 0)),
            pl.BlockSpec((1, 1, D_MODEL), lambda b, i: (b, 0, 0)),
            pl.BlockSpec((1, D_MODEL), lambda b, i: (0, 0)),
            pl.BlockSpec(w_bf.shape, lambda b, i: (0, 0)),
            pl.BlockSpec(lbl.shape, lambda b, i: (0, 0)),
        ],
        out_specs=[ospec, pl.BlockSpec((1, tm // NA_TOK, NA_WIDTH, NA_TOK), lambda b, i: (b, i, 0, 0))]
        + [ospec] * 6,
        out_shape=[out_bf, jax.ShapeDtypeStruct((B, T // NA_TOK, NA_WIDTH, NA_TOK), BF16),
                   out_bf, out_bf, out_f, out_f, out_bf, out_bf],
        compiler_params=pltpu.CompilerParams(
            dimension_semantics=("parallel", "parallel"), vmem_limit_bytes=VMEM_LIMIT),
        name="in_proj",
    )(x, sc, sh, nw, w_bf, lbl)


def _ctx_state_kernel(gfw_ref, gbw_ref, v_ref, sfw_ref, sbw_ref):
    L = gfw_ref.shape[1]
    r = lax.broadcasted_iota(jnp.int32, (L, L), 0)
    c = lax.broadcasted_iota(jnp.int32, (L, L), 1)
    upper = jnp.where(c > r, 1.0, 0.0).astype(BF16)
    lower = jnp.where(c < r, 1.0, 0.0).astype(BF16)
    v = v_ref[0]
    for g_ref, tri, s_ref in ((gfw_ref, upper, sfw_ref), (gbw_ref, lower, sbw_ref)):
        g = g_ref[0]
        kw = ((1.0 - jnp.exp(g)) * jnp.exp(_dot_exact_lhs(tri, g))).astype(BF16)
        for h in range(HG_HEADS):
            sl = slice(h * HG_DK, (h + 1) * HG_DK)
            s_ref[0, h] = _dot_tn(kw[:, sl], v[:, sl])


def _ctx_state(g_cfw, g_cbw, v_c):
    B, L, _ = g_cfw.shape
    tok = pl.BlockSpec((1, L, HG_WIDTH), lambda b: (b, 0, 0))
    st = pl.BlockSpec((1, HG_HEADS, HG_DK, HG_DK), lambda b: (b, 0, 0, 0))
    sshape = jax.ShapeDtypeStruct((B, HG_HEADS, HG_DK, HG_DK), F32)
    return pl.pallas_call(
        _ctx_state_kernel,
        grid=(B,),
        in_specs=[tok, tok, tok],
        out_specs=[st, st],
        out_shape=[sshape, sshape],
        compiler_params=pltpu.CompilerParams(vmem_limit_bytes=VMEM_LIMIT),
        name="ctx_state",
    )(g_cfw, g_cbw, v_c)


SUBLANES = 8
MXU_LEVELS = [j for j in range(1, N_LEVELS) if (1 << j) < SUBLANES]
N_SEG = len(MXU_LEVELS) + 1
HG_OUTPUT_LAG = 2


def _hg_consts():
    C = HG_CHUNK
    t = np.arange(C)[:, None]
    s = np.arange(C)[None, :]
    masks, sel = [], []
    for j in range(N_LEVELS):
        hs = 1 << j
        same = (t // (2 * hs)) == (s // (2 * hs))
        upper_t = (t % (2 * hs)) >= hs
        m = (t // (2 * hs)) * (2 * hs) + hs - 1
        masks.append(same & upper_t & ((s % (2 * hs)) < hs))
        if j in MXU_LEVELS:
            sel.append(np.where(upper_t, (s > m) & (s <= t), (s > t) & (s <= m)))
    masks.append(t == s)
    sel.append(s <= t)
    masks = np.stack(masks).astype(np.float32)
    sel = np.stack(sel).astype(np.float32)
    both = lambda a: np.stack([a, a[:, ::-1, ::-1]])
    sel = both(sel).reshape(2, N_SEG * C, C)
    return both(masks), np.tile(sel, (1, 1, 3))


class _HgDirection:
    def __init__(self, q_ref, v_ref, g_ref, st_ref, o_ref, sel_ref, mask_ref, rev):
        C = HG_CHUNK
        self.q_ref, self.v_ref, self.st_ref, self.o_ref, self.rev = q_ref, v_ref, st_ref, o_ref, rev
        d = 1 if rev else 0
        self.sel = sel_ref.at[d]
        g = g_ref[0]
        self.g_parts = _split3(g)
        self.f = jnp.exp(g)
        self.k = 1.0 - self.f
        self.level_mask = [mask_ref[d, j] for j in range(N_LEVELS + 1)]
        n_pairs = q_ref.shape[1] // (2 * C)
        self.pairs = list(range(n_pairs - 1, -1, -1) if rev else range(n_pairs))
        self.st = st_ref[...]

    def decays(self, p):
        C = HG_CHUNK
        g_cat = jnp.concatenate(
            [jnp.concatenate([part[2 * p * C:(2 * p + 1) * C], part[(2 * p + 1) * C:(2 * p + 2) * C]], axis=1)
             for part in self.g_parts], axis=0)
        sums = _dot(self.sel[...], g_cat)
        parts = [sums[i * C:(i + 1) * C] for i in range(N_SEG - 1)]
        b = sums[(N_SEG - 1) * C:N_SEG * C]
        rows = {}

        def row(m):
            if m not in rows:
                rows[m] = jnp.broadcast_to(b[m:m + 1, :], (SUBLANES, 2 * HG_DK))
            return rows[m]

        for j in range(MXU_LEVELS[-1] + 1, N_LEVELS):
            hs = 1 << j
            groups = []
            for r in range(0, C, SUBLANES):
                start = r // (2 * hs) * (2 * hs)
                b_mid = row(start + (hs if self.rev else hs - 1))
                query_side = (r - start >= hs) != self.rev
                groups.append(b[r:r + SUBLANES] - b_mid if query_side else b_mid - b[r:r + SUBLANES])
            parts.append(jnp.concatenate(groups, axis=0))
        parts.append(b)
        parts.append(jnp.concatenate([row(0 if self.rev else C - 1)] * (C // SUBLANES), axis=0) - b)
        return [jnp.exp(x) for x in parts]

    def scores(self, p, e2):
        C = HG_CHUNK
        out = []
        for c in ((2 * p + 1, 2 * p) if self.rev else (2 * p, 2 * p + 1)):
            rows = slice(c * C, (c + 1) * C)
            lanes = slice((c - 2 * p) * HG_DK, (c - 2 * p + 1) * HG_DK)
            qc = self.q_ref[0, rows, :].astype(F32)
            kc = self.k[rows]
            score = lambda qf, kf: _dot_nt(qf.astype(BF16), kf.astype(BF16))
            k_next = pltpu.roll(kc, C - 1 if self.rev else 1, axis=0)
            a = (jnp.sum(qc * kc, axis=1, keepdims=True) * self.level_mask[N_LEVELS]
                 + jnp.sum(qc * self.f[rows] * k_next, axis=1, keepdims=True) * self.level_mask[0])
            for j in range(1, N_LEVELS):
                ej = e2[j - 1][:, lanes]
                a = a + score(qc * ej, kc * ej) * self.level_mask[j]
            e_b = e2[N_LEVELS - 1][:, lanes]
            total = e_b[0:1] if self.rev else e_b[C - 1:C]
            q_dec = jnp.concatenate([(qc * e_b).astype(BF16), a.astype(BF16)], axis=1)
            out.append((rows, q_dec, (kc * e2[N_LEVELS][:, lanes]).astype(BF16), total))
        return out

    def outputs(self, chunk_terms):
        for rows, q_dec, k_dec, total in chunk_terms:
            vc = self.v_ref[0, rows, :]
            self.o_ref[0, rows, :] = _dot(q_dec, jnp.concatenate([self.st.astype(BF16), vc], axis=0))
            total_col = jnp.transpose(jnp.broadcast_to(total, (SUBLANES, HG_DK)))[:, 0:1]
            self.st = total_col * self.st + _dot_tn(k_dec, vc)

    def finish(self):
        self.st_ref[...] = self.st


def _hgrn_kernel(qf_ref, vf_ref, gf_ref, qb_ref, vb_ref, gb_ref, s0f_ref, s0b_ref, sel_ref, mask_ref,
                 of_ref, ob_ref, stf, stb):
    @pl.when(pl.program_id(2) == 0)
    def _():
        stf[...] = s0f_ref[0, 0]
        stb[...] = s0b_ref[0, 0]

    fw = _HgDirection(qf_ref, vf_ref, gf_ref, stf, of_ref, sel_ref, mask_ref, False)
    bw = _HgDirection(qb_ref, vb_ref, gb_ref, stb, ob_ref, sel_ref, mask_ref, True)
    units = [(dirn, p) for pf, pb in zip(fw.pairs, bw.pairs) for dirn, p in ((fw, pf), (bw, pb))]
    e_next = units[0][0].decays(units[0][1])
    pending = []
    for i, (dirn, p) in enumerate(units):
        e_cur = e_next
        if i + 1 < len(units):
            e_next = units[i + 1][0].decays(units[i + 1][1])
        pending.append((dirn, dirn.scores(p, e_cur)))
        if len(pending) > HG_OUTPUT_LAG:
            done, terms = pending.pop(0)
            done.outputs(terms)
    for done, terms in pending:
        done.outputs(terms)
    fw.finish()
    bw.finish()


def _hgrn(q_hg, v_hg, g_fw, g_bw, s_fw, s_bw, tb):
    B, T, _ = q_hg.shape
    nb = T // tb
    fwd = pl.BlockSpec((1, tb, HG_DK), lambda b, h, i: (b, i, h))
    bwd = pl.BlockSpec((1, tb, HG_DK), lambda b, h, i: (b, nb - 1 - i, h))
    st = pl.BlockSpec((1, 1, HG_DK, HG_DK), lambda b, h, i: (b, h, 0, 0))
    masks, sel = _hg_consts()
    masks, sel = jnp.asarray(masks), jnp.asarray(sel, dtype=BF16)
    oshape = jax.ShapeDtypeStruct((B, T, HG_WIDTH), F32)
    return pl.pallas_call(
        _hgrn_kernel,
        grid=(B, HG_HEADS, nb),
        in_specs=[fwd, fwd, fwd, bwd, bwd, bwd, st, st,
                  pl.BlockSpec(sel.shape, lambda b, h, i: (0, 0, 0)),
                  pl.BlockSpec(masks.shape, lambda b, h, i: (0, 0, 0, 0))],
        out_specs=[fwd, bwd],
        out_shape=[oshape, oshape],
        scratch_shapes=[pltpu.VMEM((HG_DK, HG_DK), F32), pltpu.VMEM((HG_DK, HG_DK), F32)],
        compiler_params=pltpu.CompilerParams(
            dimension_semantics=("parallel", "parallel", "arbitrary"), vmem_limit_bytes=VMEM_LIMIT),
        name="hgrn",
    )(q_hg, v_hg, g_fw, q_hg, v_hg, g_bw, s_fw, s_bw, sel, masks)


def _na_bias_kernel(taps_ref, valid_ref, o_ref):
    W = GRID_W
    for i in range(taps_ref.shape[2]):
        taps = jnp.broadcast_to(taps_ref[0, 0, i:i + 1, :], (W, taps_ref.shape[3]))
        shifted = pltpu.roll(taps, 0, axis=1, stride=1, stride_axis=0)
        o_ref[0, 0, i * W:(i + 1) * W, :] = jnp.where(valid_ref[0, i % NA_STEP_ROWS] > 0.0, shifted, NEG)


def _na_plan(rows):
    assert rows % NA_STEP_ROWS == 0 and rows >= NA_WIN_ROWS and (rows - NA_WIN_ROWS) % NA_STEP_ROWS == 0
    variants, starts = [], []
    for i in range(rows // NA_STEP_ROWS):
        w0 = int(np.clip(NA_STEP_ROWS * i - NA_KR // 2, 0, rows - NA_WIN_ROWS))
        geom = tuple((int(np.clip(r - NA_KR // 2, 0, rows - NA_KR)) - w0, r - w0)
                     for r in range(NA_STEP_ROWS * i, NA_STEP_ROWS * (i + 1)))
        if not variants or variants[-1] != geom:
            assert geom not in variants
            variants.append(geom)
            starts.append(i)
    return variants, starts


def _na_bias(rpb, rows):
    W = GRID_W
    n_dr = 2 * NA_KR - 1
    variants, _ = _na_plan(rows)
    dr = np.array([[[j - qrow + NA_KR - 1 if first <= j < first + NA_KR else n_dr for j in range(NA_WIN_ROWS)]
                    for first, qrow in geom] for geom in variants])
    cols = np.arange(W)
    c0 = np.clip(cols - NA_KC // 2, 0, W - NA_KC)
    col_ok = (cols[None, :] >= c0[:, None]) & (cols[None, :] < c0[:, None] + NA_KC)
    valid = (dr < n_dr)[:, :, None, :, None] & col_ok[None, None, :, None, :]
    valid = valid.reshape(len(variants), NA_STEP_ROWS, W, NA_WIN_ROWS * W).astype(np.float32)

    rpb_ext = jnp.zeros((NA_HEADS, n_dr + 1, 2 * NA_KC - 1), F32).at[:, :n_dr].set(rpb.astype(F32))
    picked = jnp.take(rpb_ext, jnp.asarray(dr.reshape(-1), dtype=jnp.int32), axis=1)
    picked = picked.reshape((NA_HEADS,) + dr.shape + (2 * NA_KC - 1,))
    ahead = picked[..., NA_KC - 1:]
    behind = jnp.roll(picked, -1, axis=3)[..., :NA_KC - 1]
    gap = jnp.zeros(picked.shape[:-1] + (W - 2 * NA_KC + 1,), F32)
    taps = jnp.concatenate([ahead, gap, behind], axis=-1)
    taps = taps.reshape(NA_HEADS // NA_GROUP, NA_GROUP, len(variants), NA_STEP_ROWS, NA_WIN_ROWS * W)
    taps = taps.transpose(2, 0, 1, 3, 4).reshape(len(variants), NA_HEADS // NA_GROUP, NA_GROUP * NA_STEP_ROWS,
                                                 NA_WIN_ROWS * W)
    out_block = (1, 1, NA_GROUP * NA_TOK, NA_WIN_ROWS * W)
    return pl.pallas_call(
        _na_bias_kernel,
        grid=(len(variants), NA_HEADS // NA_GROUP),
        in_specs=[pl.BlockSpec((1, 1) + taps.shape[2:], lambda v, g: (v, g, 0, 0)),
                  pl.BlockSpec((1,) + valid.shape[1:], lambda v, g: (v, 0, 0, 0))],
        out_specs=pl.BlockSpec(out_block, lambda v, g: (v, g, 0, 0)),
        out_shape=jax.ShapeDtypeStruct((len(variants), NA_HEADS // NA_GROUP) + out_block[2:], F32),
        name="na_bias",
    )(taps, jnp.asarray(valid))


def _natten_kernel(q_ref, kt_ref, v_ref, kct_ref, vc_ref, *rest):
    bias_refs, o_ref = rest[:-1], rest[-1]
    W = GRID_W
    rows = v_ref.shape[1] // W
    n_kblk = NA_WIN_ROWS // NA_STEP_ROWS
    rb = lax.broadcasted_iota(jnp.int32, (NA_GROUP * NA_TOK, NA_GW), 0) // NA_TOK
    cb = lax.broadcasted_iota(jnp.int32, (NA_GROUP * NA_TOK, NA_GW), 1) // NA_HEAD_DIM
    diag = rb == cb
    groups = [slice(grp * NA_GW, (grp + 1) * NA_GW) for grp in range(NA_HEADS // NA_GROUP)]

    units = []
    for u, bias_ref in enumerate(bias_refs):
        pair = len(bias_refs) * pl.program_id(1) + u
        w0 = jnp.clip(NA_STEP_ROWS * pair - NA_KR // 2, 0, rows - NA_WIN_ROWS)
        blk0 = w0 // NA_STEP_ROWS
        v_rows = pl.ds(pl.multiple_of(w0 * W, NA_TOK), NA_WIN_ROWS * W)
        for grp, gs in enumerate(groups):
            qg = q_ref[0, u * NA_TOK:(u + 1) * NA_TOK, gs]
            kt = jnp.concatenate([kt_ref[0, blk0 + j, gs, :] for j in range(n_kblk)], axis=1)
            kct = jnp.concatenate([kct_ref[0, j, gs, :] for j in range(kct_ref.shape[1])], axis=1)
            qbd = jnp.where(diag, jnp.concatenate([qg] * NA_GROUP, axis=0), jnp.zeros_like(qg[:1]))
            units.append((u, gs, v_rows, _dot(qbd, kt) + bias_ref[0, grp], _dot(qbd, kct)))
    for u, gs, v_rows, s_win, s_ctx in units:
        vw = v_ref[0, v_rows, gs]
        m = jnp.maximum(jnp.max(s_win, axis=-1, keepdims=True), jnp.max(s_ctx, axis=-1, keepdims=True))
        p_win = jnp.exp(s_win - m)
        p_ctx = jnp.exp(s_ctx - m)
        denom = jnp.sum(p_win, axis=-1, keepdims=True) + jnp.sum(p_ctx, axis=-1, keepdims=True)
        of = (_dot(p_win.astype(BF16), vw) + _dot(p_ctx.astype(BF16), vc_ref[0, :, gs])) / denom
        of = jnp.where(diag, of, 0.0)
        og = of[0:NA_TOK]
        for h in range(1, NA_GROUP):
            og = og + of[h * NA_TOK:(h + 1) * NA_TOK]
        o_ref[0, u * NA_TOK:(u + 1) * NA_TOK, gs] = og.astype(o_ref.dtype)


def _natten(q, kt, v, kct, vc, bias):
    B, T, _ = q.shape
    rows = T // GRID_W
    _, starts = _na_plan(rows)
    n_pairs = rows // NA_STEP_ROWS
    assert n_pairs % NA_STEP_PAIRS == 0
    tok = pl.BlockSpec((1, NA_STEP_PAIRS * NA_TOK, NA_WIDTH), lambda b, i: (b, i, 0))
    whole = lambda a: pl.BlockSpec((1,) + a.shape[1:], lambda b, i: (b,) + (0,) * (a.ndim - 1))

    def bias_spec(u):
        def index_map(b, i):
            pair = NA_STEP_PAIRS * i + u
            return (sum((pair >= s).astype(jnp.int32) for s in starts[1:]), 0, 0, 0)
        return pl.BlockSpec((1,) + bias.shape[1:], index_map)

    return pl.pallas_call(
        _natten_kernel,
        grid=(B, n_pairs // NA_STEP_PAIRS),
        in_specs=[tok, whole(kt), whole(v), whole(kct), whole(vc)] + [bias_spec(u) for u in range(NA_STEP_PAIRS)],
        out_specs=tok,
        out_shape=jax.ShapeDtypeStruct((B, T, NA_WIDTH), BF16),
        compiler_params=pltpu.CompilerParams(
            dimension_semantics=("parallel", "arbitrary"), vmem_limit_bytes=VMEM_LIMIT),
        name="natten",
    )(q, kt, v, kct, vc, *([bias] * NA_STEP_PAIRS))


def _post_mix_kernel(yna_ref, of_ref, ob_ref, gate_ref, hgw_ref, wo_ref, x_ref, gtm_ref, nmix_ref,
                     scf_ref, shf_ref, gtf_ref, npre_ref, npost_ref, w1_ref, w2_ref, o_ref, act_ref):
    o = of_ref[0] + ob_ref[0]
    gate = gate_ref[0].astype(F32)
    hgw = hgw_ref[...]
    parts = []
    for h in range(HG_HEADS):
        sl = slice(h * HG_DK, (h + 1) * HG_DK)
        parts.append((_rms(o[:, sl], hgw) * gate[:, sl]).astype(BF16))
    y_hg = jnp.concatenate(parts, axis=-1)
    y = _dot(yna_ref[0], wo_ref[:NA_WIDTH, :]) + _dot(y_hg, wo_ref[NA_WIDTH:, :])
    x1 = x_ref[0] + gtm_ref[0] * _rms(y, nmix_ref[...])
    d_ff = w2_ref.shape[0]
    hb = (_rms(x1, npre_ref[...]) * (1.0 + scf_ref[0]) + shf_ref[0]).astype(BF16)
    for j in range(0, d_ff, FFN_COLS):
        gate_j = _dot(hb, w1_ref[:, j:j + FFN_COLS])
        up_j = _dot(hb, w1_ref[:, d_ff + j:d_ff + j + FFN_COLS])
        act_ref[:, j:j + FFN_COLS] = (_silu(gate_j) * up_j).astype(BF16)
    z = _dot(act_ref[...], w2_ref[...])
    o_ref[0] = x1 + gtf_ref[0] * _rms(z, npost_ref[...])


def _post_mix(y_na, o_fw, o_bw, gate, hgw, wo_bf, x, gt_m, nmix, sc_f, sh_f, gt_f, npre, npost, w1_bf, w2_bf, tm):
    B, T, _ = x.shape
    d_ff = w2_bf.shape[0]
    assert d_ff % FFN_COLS == 0
    tok = lambda n: pl.BlockSpec((1, tm, n), lambda b, i: (b, i, 0))
    mod = pl.BlockSpec((1, 1, D_MODEL), lambda b, i: (b, 0, 0))
    const = lambda shape: pl.BlockSpec(shape, lambda b, i: (0,) * len(shape), pipeline_mode=pl.Buffered(1))
    return pl.pallas_call(
        _post_mix_kernel,
        grid=(B, T // tm),
        in_specs=[tok(NA_WIDTH), tok(HG_WIDTH), tok(HG_WIDTH), tok(HG_WIDTH), const((1, HG_DK)),
                  const(wo_bf.shape), tok(D_MODEL), mod, const((1, D_MODEL)),
                  mod, mod, mod, const((1, D_MODEL)), const((1, D_MODEL)),
                  const(w1_bf.shape), const(w2_bf.shape)],
        out_specs=tok(D_MODEL),
        out_shape=jax.ShapeDtypeStruct(x.shape, F32),
        scratch_shapes=[pltpu.VMEM((tm, d_ff), BF16)],
        compiler_params=pltpu.CompilerParams(
            dimension_semantics=("parallel", "parallel"), vmem_limit_bytes=VMEM_LIMIT),
        name="post_mix",
    )(y_na, o_fw, o_bw, gate, hgw, wo_bf, x, gt_m, nmix, sc_f, sh_f, gt_f, npre, npost, w1_bf, w2_bf)


def kernel(x, c, ctx, c_ctx, w_ada, b_ada, norm_mix_pre, norm_mix_post, norm_ffn_pre, norm_ffn_post,
           w_in, na_rpb, hg_lb_logits, hg_norm_w, w_out, w_ffn_in, w_ffn_out):
    B, T, D = x.shape
    assert w_ada.shape[0] == 1, "single-layer stack"
    rows = T // GRID_W

    cv = jnp.zeros((8, D), F32).at[:B].set(c).at[B].set(c_ctx)
    mod = _ada(cv, w_ada[0], b_ada[0][None, :])
    sh_m, sc_m, gt_m, sh_f, sc_f, gt_f = [mod[:, i * D:(i + 1) * D] for i in range(N_MOD)]
    lat = lambda m: m[:B, None, :]
    cx = lambda m: jnp.broadcast_to(m[B][None, None, :], (B, 1, D))

    w_in_bf = w_in[0].astype(BF16)
    lbl = hg_lb_logits.reshape(hg_lb_logits.shape[0], 2 * HG_WIDTH)
    nw_pre = norm_mix_pre[0][None, :]

    q_na, kt_na, v_na, q_hg, g_fw, g_bw, v_hg, gate = _in_proj(x, lat(sc_m), lat(sh_m), nw_pre, w_in_bf, lbl, 1024)
    _, kt_c, v_c, _, g_cfw, g_cbw, vhg_c, _ = _in_proj(ctx, cx(sc_m), cx(sh_m), nw_pre, w_in_bf, lbl,
                                                       ctx.shape[1])

    s_fw, s_bw = _ctx_state(g_cfw, g_cbw, vhg_c)
    o_fw, o_bw = _hgrn(q_hg, v_hg, g_fw, g_bw, s_fw, s_bw, 1024)

    y_na = _natten(q_na, kt_na, v_na, kt_c, v_c, _na_bias(na_rpb[0], rows))

    return _post_mix(y_na, o_fw, o_bw, gate, hg_norm_w[0][None, :], w_out[0].astype(BF16), x, lat(gt_m),
                     norm_mix_post[0][None, :], lat(sc_f), lat(sh_f), lat(gt_f), norm_ffn_pre[0][None, :],
                     norm_ffn_post[0][None, :], w_ffn_in[0].astype(BF16), w_ffn_out[0].astype(BF16), 512)
```

```python
import functools

import jax
import jax.numpy as jnp
import numpy as np
from jax import lax
from jax.experimental import pallas as pl
from jax.experimental.pallas import tpu as pltpu

D_MODEL = 1024
GRID_W = 64
NA_HEADS = 8
NA_HEAD_DIM = 64
NA_WIDTH = NA_HEADS * NA_HEAD_DIM
NA_KR = 8
NA_KC = 16
HG_HEADS = 4
HG_DK = 128
HG_WIDTH = HG_HEADS * HG_DK
HG_CHUNK = 64
N_LEVELS = 6
N_MOD = 6
EPS = 1e-6
NEG = -1e30
LOG2E = 1.4426950408889634

NA_GROUP = 4
NA_GW = NA_GROUP * NA_HEAD_DIM
NA_STEP_ROWS = 2
NA_TOK = NA_STEP_ROWS * GRID_W
NA_WIN_ROWS = NA_KR + NA_STEP_ROWS
NA_STEP_PAIRS = 2

F32 = jnp.float32
BF16 = jnp.bfloat16

VMEM_LIMIT = 56 * 1024 * 1024
FFN_COLS = 256


def _silu(x):
    return x * jax.nn.sigmoid(x)


def _dot(a, b):
    return jnp.dot(a, b, preferred_element_type=F32)


def _dot_nt(a, b):
    return lax.dot_general(a, b, (((1,), (1,)), ((), ())), preferred_element_type=F32)


def _dot_tn(a, b):
    return lax.dot_general(a, b, (((0,), (0,)), ((), ())), preferred_element_type=F32)


def _split3(x):
    x1 = x.astype(BF16)
    r1 = x - x1.astype(F32)
    x2 = r1.astype(BF16)
    r2 = r1 - x2.astype(F32)
    return x1, x2, r2.astype(BF16)


def _dot_exact_lhs(t, x):
    x1, x2, x3 = _split3(x)
    return _dot(t, x1) + _dot(t, x2) + _dot(t, x3)


def _rms(x, w):
    return x * lax.rsqrt(jnp.mean(x * x, axis=-1, keepdims=True) + EPS) * w


def _ada_kernel(cv_ref, w_ref, b_ref, o_ref):
    s = _silu(cv_ref[...])
    s1, s2, s3 = _split3(s)
    w1, w2, w3 = _split3(w_ref[...])
    acc = _dot(s1, w1) + (_dot(s1, w2) + _dot(s2, w1)) + (_dot(s1, w3) + _dot(s2, w2) + _dot(s3, w1))
    o_ref[...] = acc + b_ref[...]


def _ada(cv, w_ada, b_ada):
    n = w_ada.shape[1]
    tn = 1536
    return pl.pallas_call(
        _ada_kernel,
        grid=(n // tn,),
        in_specs=[
            pl.BlockSpec((8, D_MODEL), lambda j: (0, 0)),
            pl.BlockSpec((D_MODEL, tn), lambda j: (0, j)),
            pl.BlockSpec((1, tn), lambda j: (0, j)),
        ],
        out_specs=pl.BlockSpec((8, tn), lambda j: (0, j)),
        out_shape=jax.ShapeDtypeStruct((8, n), F32),
        compiler_params=pltpu.CompilerParams(vmem_limit_bytes=VMEM_LIMIT),
        name="ada",
    )(cv, w_ada, b_ada)


def _in_proj_kernel(x_ref, sc_ref, sh_ref, nw_ref, w_ref, lbl_ref,
                    qna_ref, knat_ref, vna_ref, qhg_ref, gfw_ref, gbw_ref, vhg_ref, gate_ref):
    x = x_ref[0]
    h = _rms(x, nw_ref[...]) * (1.0 + sc_ref[0]) + sh_ref[0]
    hb = h.astype(BF16)

    def proj(i):
        return _dot(hb, w_ref[:, i * 512:(i + 1) * 512])

    lbl = lbl_ref[...]
    e = jnp.exp(lbl - jnp.max(lbl, axis=0, keepdims=True))
    lb = e[0:1] / jnp.sum(e, axis=0, keepdims=True)

    qna_ref[0] = (proj(0) * (NA_HEAD_DIM ** -0.5)).astype(BF16)
    k_t = proj(1).astype(BF16).T
    for j in range(knat_ref.shape[1]):
        knat_ref[0, j] = k_t[:, j * NA_TOK:(j + 1) * NA_TOK]
    vna_ref[0] = proj(2).astype(BF16)
    qhg_ref[0] = _silu(proj(3)).astype(BF16)
    lb_f = lb[:, :HG_WIDTH]
    lb_b = lb[:, HG_WIDTH:]
    gfw_ref[0] = jnp.log(lb_f + (1.0 - lb_f) * jax.nn.sigmoid(proj(4)))
    gbw_ref[0] = jnp.log(lb_b + (1.0 - lb_b) * jax.nn.sigmoid(proj(5)))
    vhg_ref[0] = proj(6).astype(BF16)
    gate_ref[0] = _silu(proj(7)).astype(BF16)


def _in_proj(x, sc, sh, nw, w_bf, lbl, tm):
    B, T, _ = x.shape
    tok = lambda b, i: (b, i, 0)
    out_bf = jax.ShapeDtypeStruct((B, T, 512), BF16)
    out_f = jax.ShapeDtypeStruct((B, T, 512), F32)
    ospec = pl.BlockSpec((1, tm, 512), tok)
    return pl.pallas_call(
        _in_proj_kernel,
        grid=(B, T // tm),
        in_specs=[
            pl.BlockSpec((1, tm, D_MODEL), tok),
            pl.BlockSpec((1, 1, D_MODEL), lambda b, i: (b, 0, 0)),
            pl.BlockSpec((1, 1, D_MODEL), lambda b, i: (b, 0, 0)),
            pl.BlockSpec((1, D_MODEL), lambda b, i: (0, 0)),
            pl.BlockSpec(w_bf.shape, lambda b, i: (0, 0)),
            pl.BlockSpec(lbl.shape, lambda b, i: (0, 0)),
        ],
        out_specs=[ospec, pl.BlockSpec((1, tm // NA_TOK, NA_WIDTH, NA_TOK), lambda b, i: (b, i, 0, 0))]
        + [ospec] * 6,
        out_shape=[out_bf, jax.ShapeDtypeStruct((B, T // NA_TOK, NA_WIDTH, NA_TOK), BF16),
                   out_bf, out_bf, out_f, out_f, out_bf, out_bf],
        compiler_params=pltpu.CompilerParams(
            dimension_semantics=("parallel", "parallel"), vmem_limit_bytes=VMEM_LIMIT),
        name="in_proj",
    )(x, sc, sh, nw, w_bf, lbl)


def _ctx_state_kernel(gfw_ref, gbw_ref, v_ref, sfw_ref, sbw_ref):
    L = gfw_ref.shape[1]
    r = lax.broadcasted_iota(jnp.int32, (L, L), 0)
    c = lax.broadcasted_iota(jnp.int32, (L, L), 1)
    upper = jnp.where(c > r, 1.0, 0.0).astype(BF16)
    lower = jnp.where(c < r, 1.0, 0.0).astype(BF16)
    v = v_ref[0]
    for g_ref, tri, s_ref in ((gfw_ref, upper, sfw_ref), (gbw_ref, lower, sbw_ref)):
        g = g_ref[0]
        kw = ((1.0 - jnp.exp(g)) * jnp.exp(_dot_exact_lhs(tri, g))).astype(BF16)
        for h in range(HG_HEADS):
            sl = slice(h * HG_DK, (h + 1) * HG_DK)
            s_ref[0, h] = _dot_tn(kw[:, sl], v[:, sl])


def _ctx_state(g_cfw, g_cbw, v_c):
    B, L, _ = g_cfw.shape
    tok = pl.BlockSpec((1, L, HG_WIDTH), lambda b: (b, 0, 0))
    st = pl.BlockSpec((1, HG_HEADS, HG_DK, HG_DK), lambda b: (b, 0, 0, 0))
    sshape = jax.ShapeDtypeStruct((B, HG_HEADS, HG_DK, HG_DK), F32)
    return pl.pallas_call(
        _ctx_state_kernel,
        grid=(B,),
        in_specs=[tok, tok, tok],
        out_specs=[st, st],
        out_shape=[sshape, sshape],
        compiler_params=pltpu.CompilerParams(vmem_limit_bytes=VMEM_LIMIT),
        name="ctx_state",
    )(g_cfw, g_cbw, v_c)


SUBLANES = 8
HG_OUTPUT_LAG = 2


def _hg_masks():
    C = HG_CHUNK
    t = np.arange(C)[:, None]
    s = np.arange(C)[None, :]
    masks = []
    for j in range(N_LEVELS):
        hs = 1 << j
        same = (t // (2 * hs)) == (s // (2 * hs))
        masks.append(same & ((t % (2 * hs)) >= hs) & ((s % (2 * hs)) < hs))
    masks.append(t == s)
    masks = np.stack(masks).astype(np.float32)
    return np.stack([masks, masks[:, ::-1, ::-1]])


class _HgDirection:
    def __init__(self, q_ref, v_ref, g_ref, st_ref, o_ref, mask_ref, rev):
        C = HG_CHUNK
        self.q_ref, self.v_ref, self.st_ref, self.o_ref, self.rev = q_ref, v_ref, st_ref, o_ref, rev
        d = 1 if rev else 0
        self.g = g = g_ref[0] * LOG2E
        self.pos = lax.broadcasted_iota(jnp.int32, (C, 2 * HG_DK), 0) % SUBLANES
        self.f = jnp.exp2(g)
        self.k = 1.0 - self.f
        self.level_mask = [mask_ref[d, j] > 0.5 for j in range(N_LEVELS + 1)]
        t = lax.broadcasted_iota(jnp.int32, (C, HG_DK), 0)
        self.query_rows = {j: ((t % (2 << j)) < (1 << j)) if rev else ((t % (2 << j)) >= (1 << j))
                           for j in range(1, N_LEVELS) if (1 << j) < SUBLANES}
        n_pairs = q_ref.shape[1] // (2 * C)
        self.pairs = list(range(n_pairs - 1, -1, -1) if rev else range(n_pairs))
        self.st = st_ref[...]

    def decays(self, p):
        C, S, L = HG_CHUNK, SUBLANES, 2 * HG_DK
        rev = self.rev
        pos, pos_s = self.pos, lax.broadcasted_iota(jnp.int32, (S, L), 0)
        row = lambda a, r: jnp.broadcast_to(a[r:r + 1, :], (S, L))
        group = lambda a, i: a[i * S:(i + 1) * S]

        halves = []
        for ch in (2 * p, 2 * p + 1):
            c = self.g[ch * C:(ch + 1) * C]
            s = 1
            while s < S:
                if rev:
                    c = c + jnp.where(pos[:, :HG_DK] < S - s, pltpu.roll(c, C - s, axis=0), 0.0)
                else:
                    c = c + jnp.where(pos[:, :HG_DK] >= s, pltpu.roll(c, s, axis=0), 0.0)
                s *= 2
            halves.append(c)
        c = jnp.concatenate(halves, axis=1)

        def split_in_group(hs):
            blk = 2 * hs
            query = (pos_s % blk) < hs if rev else (pos_s % blk) >= hs
            out = []
            for i in range(C // S):
                mid = row(c, i * S + (hs if rev else hs - 1))
                for a in range(1, S // blk):
                    mid = jnp.where(pos_s >= a * blk, row(c, i * S + a * blk + (hs if rev else hs - 1)), mid)
                out.append(jnp.where(query, group(c, i) - mid, mid - group(c, i)))
            return jnp.concatenate(out, axis=0)

        parts = [split_in_group(1 << j) for j in range(1, N_LEVELS) if 2 << j <= S]

        b_groups, carried = [None] * (C // S), None
        for i in (range(C // S - 1, -1, -1) if rev else range(C // S)):
            b_groups[i] = group(c, i) if carried is None else group(c, i) + carried
            total = row(c, i * S + (0 if rev else S - 1))
            carried = total if carried is None else carried + total
        b = jnp.concatenate(b_groups, axis=0)

        rows = {}

        def b_row(m):
            if m not in rows:
                rows[m] = row(b, m)
            return rows[m]

        for j in range(1, N_LEVELS):
            hs = 1 << j
            if 2 * hs <= S:
                continue
            groups = []
            for r in range(0, C, S):
                start = r // (2 * hs) * (2 * hs)
                b_mid = b_row(start + (hs if rev else hs - 1))
                query_side = (r - start >= hs) != rev
                groups.append(b[r:r + S] - b_mid if query_side else b_mid - b[r:r + S])
            parts.append(jnp.concatenate(groups, axis=0))
        parts.append(b)
        parts.append(jnp.concatenate([b_row(0 if rev else C - 1)] * (C // S), axis=0) - b)
        return [jnp.exp2(x) for x in parts]

    def scores(self, p, e2):
        C = HG_CHUNK
        out = []
        for c in ((2 * p + 1, 2 * p) if self.rev else (2 * p, 2 * p + 1)):
            rows = slice(c * C, (c + 1) * C)
            lanes = slice((c - 2 * p) * HG_DK, (c - 2 * p + 1) * HG_DK)
            qc = self.q_ref[0, rows, :].astype(F32)
            kc = self.k[rows]
            k_next = pltpu.roll(kc, C - 1 if self.rev else 1, axis=0)
            a = jnp.where(self.level_mask[N_LEVELS], jnp.sum(qc * kc, axis=1, keepdims=True), 0.0)
            a = jnp.where(self.level_mask[0], jnp.sum(qc * self.f[rows] * k_next, axis=1, keepdims=True), a)
            for j in range(1, N_LEVELS):
                hs = 1 << j
                if hs >= SUBLANES:
                    side = jnp.concatenate(
                        [(qc if ((r % (2 * hs)) >= hs) != self.rev else kc)[r:r + SUBLANES]
                         for r in range(0, C, SUBLANES)], axis=0)
                else:
                    side = jnp.where(self.query_rows[j], qc, kc)
                z = (side * e2[j - 1][:, lanes]).astype(BF16)
                a = jnp.where(self.level_mask[j], _dot_nt(z, z), a)
            e_b = e2[N_LEVELS - 1][:, lanes]
            total = e_b[0:1] if self.rev else e_b[C - 1:C]
            q_dec = jnp.concatenate([(qc * e_b).astype(BF16), a.astype(BF16)], axis=1)
            out.append((rows, q_dec, (kc * e2[N_LEVELS][:, lanes]).astype(BF16), total))
        return out

    def outputs(self, chunk_terms):
        for rows, q_dec, k_dec, total in chunk_terms:
            vc = self.v_ref[0, rows, :]
            self.o_ref[0, rows, :] = _dot(q_dec, jnp.concatenate([self.st.astype(BF16), vc], axis=0))
            total_col = jnp.transpose(jnp.broadcast_to(total, (SUBLANES, HG_DK)))[:, 0:1]
            self.st = total_col * self.st + _dot_tn(k_dec, vc)

    def finish(self):
        self.st_ref[...] = self.st


def _hgrn_kernel(qf_ref, vf_ref, gf_ref, qb_ref, vb_ref, gb_ref, s0f_ref, s0b_ref, mask_ref,
                 of_ref, ob_ref, stf, stb):
    @pl.when(pl.program_id(2) == 0)
    def _():
        stf[...] = s0f_ref[0, 0]
        stb[...] = s0b_ref[0, 0]

    fw = _HgDirection(qf_ref, vf_ref, gf_ref, stf, of_ref, mask_ref, False)
    bw = _HgDirection(qb_ref, vb_ref, gb_ref, stb, ob_ref, mask_ref, True)
    units = [(dirn, p) for pf, pb in zip(fw.pairs, bw.pairs) for dirn, p in ((fw, pf), (bw, pb))]
    e_next = units[0][0].decays(units[0][1])
    pending = []
    for i, (dirn, p) in enumerate(units):
        e_cur = e_next
        if i + 1 < len(units):
            e_next = units[i + 1][0].decays(units[i + 1][1])
        pending.append((dirn, dirn.scores(p, e_cur)))
        if len(pending) > HG_OUTPUT_LAG:
            done, terms = pending.pop(0)
            done.outputs(terms)
    for done, terms in pending:
        done.outputs(terms)
    fw.finish()
    bw.finish()


def _hgrn(q_hg, v_hg, g_fw, g_bw, s_fw, s_bw, tb):
    B, T, _ = q_hg.shape
    nb = T // tb
    fwd = pl.BlockSpec((1, tb, HG_DK), lambda b, h, i: (b, i, h))
    bwd = pl.BlockSpec((1, tb, HG_DK), lambda b, h, i: (b, nb - 1 - i, h))
    st = pl.BlockSpec((1, 1, HG_DK, HG_DK), lambda b, h, i: (b, h, 0, 0))
    masks = jnp.asarray(_hg_masks())
    oshape = jax.ShapeDtypeStruct((B, T, HG_WIDTH), F32)
    return pl.pallas_call(
        _hgrn_kernel,
        grid=(B, HG_HEADS, nb),
        in_specs=[fwd, fwd, fwd, bwd, bwd, bwd, st, st,
                  pl.BlockSpec(masks.shape, lambda b, h, i: (0, 0, 0, 0))],
        out_specs=[fwd, bwd],
        out_shape=[oshape, oshape],
        scratch_shapes=[pltpu.VMEM((HG_DK, HG_DK), F32), pltpu.VMEM((HG_DK, HG_DK), F32)],
        compiler_params=pltpu.CompilerParams(
            dimension_semantics=("parallel", "parallel", "arbitrary"), vmem_limit_bytes=VMEM_LIMIT),
        name="hgrn",
    )(q_hg, v_hg, g_fw, q_hg, v_hg, g_bw, s_fw, s_bw, masks)


def _na_bias_kernel(taps_ref, valid_ref, o_ref):
    W = GRID_W
    for i in range(taps_ref.shape[2]):
        taps = jnp.broadcast_to(taps_ref[0, 0, i:i + 1, :], (W, taps_ref.shape[3]))
        shifted = pltpu.roll(taps, 0, axis=1, stride=1, stride_axis=0)
        o_ref[0, 0, i * W:(i + 1) * W, :] = jnp.where(valid_ref[0, i % NA_STEP_ROWS] > 0.0, shifted, NEG)


def _na_plan(rows):
    assert rows % NA_STEP_ROWS == 0 and rows >= NA_WIN_ROWS and (rows - NA_WIN_ROWS) % NA_STEP_ROWS == 0
    variants, starts = [], []
    for i in range(rows // NA_STEP_ROWS):
        w0 = int(np.clip(NA_STEP_ROWS * i - NA_KR // 2, 0, rows - NA_WIN_ROWS))
        geom = tuple((int(np.clip(r - NA_KR // 2, 0, rows - NA_KR)) - w0, r - w0)
                     for r in range(NA_STEP_ROWS * i, NA_STEP_ROWS * (i + 1)))
        if not variants or variants[-1] != geom:
            assert geom not in variants
            variants.append(geom)
            starts.append(i)
    return variants, starts


def _na_bias(rpb, rows):
    W = GRID_W
    n_dr = 2 * NA_KR - 1
    variants, _ = _na_plan(rows)
    dr = np.array([[[j - qrow + NA_KR - 1 if first <= j < first + NA_KR else n_dr for j in range(NA_WIN_ROWS)]
                    for first, qrow in geom] for geom in variants])
    cols = np.arange(W)
    c0 = np.clip(cols - NA_KC // 2, 0, W - NA_KC)
    col_ok = (cols[None, :] >= c0[:, None]) & (cols[None, :] < c0[:, None] + NA_KC)
    valid = (dr < n_dr)[:, :, None, :, None] & col_ok[None, None, :, None, :]
    valid = valid.reshape(len(variants), NA_STEP_ROWS, W, NA_WIN_ROWS * W).astype(np.float32)

    rpb_ext = jnp.zeros((NA_HEADS, n_dr + 1, 2 * NA_KC - 1), F32).at[:, :n_dr].set(rpb.astype(F32))
    picked = jnp.take(rpb_ext, jnp.asarray(dr.reshape(-1), dtype=jnp.int32), axis=1)
    picked = picked.reshape((NA_HEADS,) + dr.shape + (2 * NA_KC - 1,))
    ahead = picked[..., NA_KC - 1:]
    behind = jnp.roll(picked, -1, axis=3)[..., :NA_KC - 1]
    gap = jnp.zeros(picked.shape[:-1] + (W - 2 * NA_KC + 1,), F32)
    taps = jnp.concatenate([ahead, gap, behind], axis=-1)
    taps = taps.reshape(NA_HEADS // NA_GROUP, NA_GROUP, len(variants), NA_STEP_ROWS, NA_WIN_ROWS * W)
    taps = taps.transpose(2, 0, 1, 3, 4).reshape(len(variants), NA_HEADS // NA_GROUP, NA_GROUP * NA_STEP_ROWS,
                                                 NA_WIN_ROWS * W)
    out_block = (1, 1, NA_GROUP * NA_TOK, NA_WIN_ROWS * W)
    return pl.pallas_call(
        _na_bias_kernel,
        grid=(len(variants), NA_HEADS // NA_GROUP),
        in_specs=[pl.BlockSpec((1, 1) + taps.shape[2:], lambda v, g: (v, g, 0, 0)),
                  pl.BlockSpec((1,) + valid.shape[1:], lambda v, g: (v, 0, 0, 0))],
        out_specs=pl.BlockSpec(out_block, lambda v, g: (v, g, 0, 0)),
        out_shape=jax.ShapeDtypeStruct((len(variants), NA_HEADS // NA_GROUP) + out_block[2:], F32),
        name="na_bias",
    )(taps, jnp.asarray(valid))


def _natten_kernel(q_ref, kt_ref, v_ref, kct_ref, vc_ref, *rest):
    bias_refs, o_ref = rest[:-1], rest[-1]
    W = GRID_W
    rows = v_ref.shape[1] // W
    n_kblk = NA_WIN_ROWS // NA_STEP_ROWS
    rb = lax.broadcasted_iota(jnp.int32, (NA_GROUP * NA_TOK, NA_GW), 0) // NA_TOK
    cb = lax.broadcasted_iota(jnp.int32, (NA_GROUP * NA_TOK, NA_GW), 1) // NA_HEAD_DIM
    diag = rb == cb
    groups = [slice(grp * NA_GW, (grp + 1) * NA_GW) for grp in range(NA_HEADS // NA_GROUP)]

    units = []
    for u, bias_ref in enumerate(bias_refs):
        pair = len(bias_refs) * pl.program_id(1) + u
        w0 = jnp.clip(NA_STEP_ROWS * pair - NA_KR // 2, 0, rows - NA_WIN_ROWS)
        blk0 = w0 // NA_STEP_ROWS
        v_rows = pl.ds(pl.multiple_of(w0 * W, NA_TOK), NA_WIN_ROWS * W)
        for grp, gs in enumerate(groups):
            qg = q_ref[0, u * NA_TOK:(u + 1) * NA_TOK, gs]
            kt = jnp.concatenate([kt_ref[0, blk0 + j, gs, :] for j in range(n_kblk)], axis=1)
            kct = jnp.concatenate([kct_ref[0, j, gs, :] for j in range(kct_ref.shape[1])], axis=1)
            qbd = jnp.where(diag, jnp.concatenate([qg] * NA_GROUP, axis=0), jnp.zeros_like(qg[:1]))
            units.append((u, gs, v_rows, _dot(qbd, kt) + bias_ref[0, grp], _dot(qbd, kct)))
    for u, gs, v_rows, s_win, s_ctx in units:
        vw = v_ref[0, v_rows, gs]
        m = jnp.maximum(jnp.max(s_win, axis=-1, keepdims=True), jnp.max(s_ctx, axis=-1, keepdims=True))
        p_win = jnp.exp(s_win - m)
        p_ctx = jnp.exp(s_ctx - m)
        denom = jnp.sum(p_win, axis=-1, keepdims=True) + jnp.sum(p_ctx, axis=-1, keepdims=True)
        of = (_dot(p_win.astype(BF16), vw) + _dot(p_ctx.astype(BF16), vc_ref[0, :, gs])) / denom
        of = jnp.where(diag, of, 0.0)
        og = of[0:NA_TOK]
        for h in range(1, NA_GROUP):
            og = og + of[h * NA_TOK:(h + 1) * NA_TOK]
        o_ref[0, u * NA_TOK:(u + 1) * NA_TOK, gs] = og.astype(o_ref.dtype)


def _natten(q, kt, v, kct, vc, bias):
    B, T, _ = q.shape
    rows = T // GRID_W
    _, starts = _na_plan(rows)
    n_pairs = rows // NA_STEP_ROWS
    assert n_pairs % NA_STEP_PAIRS == 0
    tok = pl.BlockSpec((1, NA_STEP_PAIRS * NA_TOK, NA_WIDTH), lambda b, i: (b, i, 0))
    whole = lambda a: pl.BlockSpec((1,) + a.shape[1:], lambda b, i: (b,) + (0,) * (a.ndim - 1))

    def bias_spec(u):
        def index_map(b, i):
            pair = NA_STEP_PAIRS * i + u
            return (sum((pair >= s).astype(jnp.int32) for s in starts[1:]), 0, 0, 0)
        return pl.BlockSpec((1,) + bias.shape[1:], index_map)

    return pl.pallas_call(
        _natten_kernel,
        grid=(B, n_pairs // NA_STEP_PAIRS),
        in_specs=[tok, whole(kt), whole(v), whole(kct), whole(vc)] + [bias_spec(u) for u in range(NA_STEP_PAIRS)],
        out_specs=tok,
        out_shape=jax.ShapeDtypeStruct((B, T, NA_WIDTH), BF16),
        compiler_params=pltpu.CompilerParams(
            dimension_semantics=("parallel", "arbitrary"), vmem_limit_bytes=VMEM_LIMIT),
        name="natten",
    )(q, kt, v, kct, vc, *([bias] * NA_STEP_PAIRS))


def _post_mix_kernel(yna_ref, of_ref, ob_ref, gate_ref, hgw_ref, wo_ref, x_ref, gtm_ref, nmix_ref,
                     scf_ref, shf_ref, gtf_ref, npre_ref, npost_ref, w1_ref, w2_ref, o_ref, act_ref):
    o = of_ref[0] + ob_ref[0]
    gate = gate_ref[0].astype(F32)
    hgw = hgw_ref[...]
    parts = []
    for h in range(HG_HEADS):
        sl = slice(h * HG_DK, (h + 1) * HG_DK)
        parts.append((_rms(o[:, sl], hgw) * gate[:, sl]).astype(BF16))
    y_hg = jnp.concatenate(parts, axis=-1)
    y = _dot(yna_ref[0], wo_ref[:NA_WIDTH, :]) + _dot(y_hg, wo_ref[NA_WIDTH:, :])
    x1 = x_ref[0] + gtm_ref[0] * _rms(y, nmix_ref[...])
    d_ff = w2_ref.shape[0]
    hb = (_rms(x1, npre_ref[...]) * (1.0 + scf_ref[0]) + shf_ref[0]).astype(BF16)
    for j in range(0, d_ff, FFN_COLS):
        gate_j = _dot(hb, w1_ref[:, j:j + FFN_COLS])
        up_j = _dot(hb, w1_ref[:, d_ff + j:d_ff + j + FFN_COLS])
        act_ref[:, j:j + FFN_COLS] = (_silu(gate_j) * up_j).astype(BF16)
    z = _dot(act_ref[...], w2_ref[...])
    o_ref[0] = x1 + gtf_ref[0] * _rms(z, npost_ref[...])


def _post_mix(y_na, o_fw, o_bw, gate, hgw, wo_bf, x, gt_m, nmix, sc_f, sh_f, gt_f, npre, npost, w1_bf, w2_bf, tm):
    B, T, _ = x.shape
    d_ff = w2_bf.shape[0]
    assert d_ff % FFN_COLS == 0
    tok = lambda n: pl.BlockSpec((1, tm, n), lambda b, i: (b, i, 0))
    mod = pl.BlockSpec((1, 1, D_MODEL), lambda b, i: (b, 0, 0))
    const = lambda shape: pl.BlockSpec(shape, lambda b, i: (0,) * len(shape), pipeline_mode=pl.Buffered(1))
    return pl.pallas_call(
        _post_mix_kernel,
        grid=(B, T // tm),
        in_specs=[tok(NA_WIDTH), tok(HG_WIDTH), tok(HG_WIDTH), tok(HG_WIDTH), const((1, HG_DK)),
                  const(wo_bf.shape), tok(D_MODEL), mod, const((1, D_MODEL)),
                  mod, mod, mod, const((1, D_MODEL)), const((1, D_MODEL)),
                  const(w1_bf.shape), const(w2_bf.shape)],
        out_specs=tok(D_MODEL),
        out_shape=jax.ShapeDtypeStruct(x.shape, F32),
        scratch_shapes=[pltpu.VMEM((tm, d_ff), BF16)],
        compiler_params=pltpu.CompilerParams(
            dimension_semantics=("parallel", "parallel"), vmem_limit_bytes=VMEM_LIMIT),
        name="post_mix",
    )(y_na, o_fw, o_bw, gate, hgw, wo_bf, x, gt_m, nmix, sc_f, sh_f, gt_f, npre, npost, w1_bf, w2_bf)


def kernel(x, c, ctx, c_ctx, w_ada, b_ada, norm_mix_pre, norm_mix_post, norm_ffn_pre, norm_ffn_post,
           w_in, na_rpb, hg_lb_logits, hg_norm_w, w_out, w_ffn_in, w_ffn_out):
    B, T, D = x.shape
    assert w_ada.shape[0] == 1, "single-layer stack"
    rows = T // GRID_W

    cv = jnp.zeros((8, D), F32).at[:B].set(c).at[B].set(c_ctx)
    mod = _ada(cv, w_ada[0], b_ada[0][None, :])
    sh_m, sc_m, gt_m, sh_f, sc_f, gt_f = [mod[:, i * D:(i + 1) * D] for i in range(N_MOD)]
    lat = lambda m: m[:B, None, :]
    cx = lambda m: jnp.broadcast_to(m[B][None, None, :], (B, 1, D))

    w_in_bf = w_in[0].astype(BF16)
    lbl = hg_lb_logits.reshape(hg_lb_logits.shape[0], 2 * HG_WIDTH)
    nw_pre = norm_mix_pre[0][None, :]

    q_na, kt_na, v_na, q_hg, g_fw, g_bw, v_hg, gate = _in_proj(x, lat(sc_m), lat(sh_m), nw_pre, w_in_bf, lbl, 1024)
    _, kt_c, v_c, _, g_cfw, g_cbw, vhg_c, _ = _in_proj(ctx, cx(sc_m), cx(sh_m), nw_pre, w_in_bf, lbl,
                                                       ctx.shape[1])

    s_fw, s_bw = _ctx_state(g_cfw, g_cbw, vhg_c)
    o_fw, o_bw = _hgrn(q_hg, v_hg, g_fw, g_bw, s_fw, s_bw, 1024)

    y_na = _natten(q_na, kt_na, v_na, kt_c, v_c, _na_bias(na_rpb[0], rows))

    return _post_mix(y_na, o_fw, o_bw, gate, hg_norm_w[0][None, :], w_out[0].astype(BF16), x, lat(gt_m),
                     norm_mix_post[0][None, :], lat(sc_f), lat(sh_f), lat(gt_f), norm_ffn_pre[0][None, :],
                     norm_ffn_post[0][None, :], w_ffn_in[0].astype(BF16), w_ffn_out[0].astype(BF16), 512)
```

```python
import functools

import jax
import jax.numpy as jnp
import numpy as np
from jax import lax
from jax.experimental import pallas as pl
from jax.experimental.pallas import tpu as pltpu

D_MODEL = 1024
GRID_W = 64
NA_HEADS = 8
NA_HEAD_DIM = 64
NA_WIDTH = NA_HEADS * NA_HEAD_DIM
NA_KR = 8
NA_KC = 16
HG_HEADS = 4
HG_DK = 128
HG_WIDTH = HG_HEADS * HG_DK
HG_CHUNK = 64
N_LEVELS = 6
N_MOD = 6
EPS = 1e-6
NEG = -1e30
LOG2E = 1.4426950408889634

NA_GROUP = 4
NA_GW = NA_GROUP * NA_HEAD_DIM
NA_STEP_ROWS = 2
NA_TOK = NA_STEP_ROWS * GRID_W
NA_WIN_ROWS = NA_KR + NA_STEP_ROWS
NA_STEP_PAIRS = 2

F32 = jnp.float32
BF16 = jnp.bfloat16

VMEM_LIMIT = 56 * 1024 * 1024
FFN_COLS = 256


def _silu(x):
    return x * jax.nn.sigmoid(x)


def _dot(a, b):
    return jnp.dot(a, b, preferred_element_type=F32)


def _dot_nt(a, b):
    return lax.dot_general(a, b, (((1,), (1,)), ((), ())), preferred_element_type=F32)


def _dot_tn(a, b):
    return lax.dot_general(a, b, (((0,), (0,)), ((), ())), preferred_element_type=F32)


def _split3(x):
    x1 = x.astype(BF16)
    r1 = x - x1.astype(F32)
    x2 = r1.astype(BF16)
    r2 = r1 - x2.astype(F32)
    return x1, x2, r2.astype(BF16)


def _dot_exact_lhs(t, x):
    x1, x2, x3 = _split3(x)
    return _dot(t, x1) + _dot(t, x2) + _dot(t, x3)


def _rms(x, w):
    return x * lax.rsqrt(jnp.mean(x * x, axis=-1, keepdims=True) + EPS) * w


def _ada_kernel(cv_ref, w_ref, b_ref, o_ref):
    s = _silu(cv_ref[...])
    s1, s2, s3 = _split3(s)
    w1, w2, w3 = _split3(w_ref[...])
    acc = _dot(s1, w1) + (_dot(s1, w2) + _dot(s2, w1)) + (_dot(s1, w3) + _dot(s2, w2) + _dot(s3, w1))
    o_ref[...] = acc + b_ref[...]


def _ada(cv, w_ada, b_ada):
    n = w_ada.shape[1]
    tn = 1536
    return pl.pallas_call(
        _ada_kernel,
        grid=(n // tn,),
        in_specs=[
            pl.BlockSpec((8, D_MODEL), lambda j: (0, 0)),
            pl.BlockSpec((D_MODEL, tn), lambda j: (0, j)),
            pl.BlockSpec((1, tn), lambda j: (0, j)),
        ],
        out_specs=pl.BlockSpec((8, tn), lambda j: (0, j)),
        out_shape=jax.ShapeDtypeStruct((8, n), F32),
        compiler_params=pltpu.CompilerParams(vmem_limit_bytes=VMEM_LIMIT),
        name="ada",
    )(cv, w_ada, b_ada)


def _in_proj_kernel(x_ref, sc_ref, sh_ref, nw_ref, w_ref, lbl_ref,
                    qna_ref, knat_ref, vna_ref, qhg_ref, gfw_ref, gbw_ref, vhg_ref, gate_ref):
    x = x_ref[0]
    h = _rms(x, nw_ref[...]) * (1.0 + sc_ref[0]) + sh_ref[0]
    hb = h.astype(BF16)

    def proj(i):
        return _dot(hb, w_ref[:, i * 512:(i + 1) * 512])

    lbl = lbl_ref[...]
    e = jnp.exp(lbl - jnp.max(lbl, axis=0, keepdims=True))
    lb = e[0:1] / jnp.sum(e, axis=0, keepdims=True)

    qna_ref[0] = (proj(0) * (NA_HEAD_DIM ** -0.5)).astype(BF16)
    k_t = proj(1).astype(BF16).T
    for j in range(knat_ref.shape[1]):
        knat_ref[0, j] = k_t[:, j * NA_TOK:(j + 1) * NA_TOK]
    vna_ref[0] = proj(2).astype(BF16)
    qhg_ref[0] = _silu(proj(3)).astype(BF16)
    lb_f = lb[:, :HG_WIDTH]
    lb_b = lb[:, HG_WIDTH:]
    gfw_ref[0] = jnp.log(lb_f + (1.0 - lb_f) * jax.nn.sigmoid(proj(4)))
    gbw_ref[0] = jnp.log(lb_b + (1.0 - lb_b) * jax.nn.sigmoid(proj(5)))
    vhg_ref[0] = proj(6).astype(BF16)
    gate_ref[0] = _silu(proj(7)).astype(BF16)


def _in_proj(x, sc, sh, nw, w_bf, lbl, tm):
    B, T, _ = x.shape
    tok = lambda b, i: (b, i, 0)
    out_bf = jax.ShapeDtypeStruct((B, T, 512), BF16)
    out_f = jax.ShapeDtypeStruct((B, T, 512), F32)
    ospec = pl.BlockSpec((1, tm, 512), tok)
    return pl.pallas_call(
        _in_proj_kernel,
        grid=(B, T // tm),
        in_specs=[
            pl.BlockSpec((1, tm, D_MODEL), tok),
            pl.BlockSpec((1, 1, D_MODEL), lambda b, i: (b, 0, 0)),
            pl.BlockSpec((1, 1, D_MODEL), lambda b, i: (b, 0, 0)),
            pl.BlockSpec((1, D_MODEL), lambda b, i: (0, 0)),
            pl.BlockSpec(w_bf.shape, lambda b, i: (0, 0)),
            pl.BlockSpec(lbl.shape, lambda b, i: (0, 0)),
        ],
        out_specs=[ospec, pl.BlockSpec((1, tm // NA_TOK, NA_WIDTH, NA_TOK), lambda b, i: (b, i, 0, 0))]
        + [ospec] * 6,
        out_shape=[out_bf, jax.ShapeDtypeStruct((B, T // NA_TOK, NA_WIDTH, NA_TOK), BF16),
                   out_bf, out_bf, out_f, out_f, out_bf, out_bf],
        compiler_params=pltpu.CompilerParams(
            dimension_semantics=("parallel", "parallel"), vmem_limit_bytes=VMEM_LIMIT),
        name="in_proj",
    )(x, sc, sh, nw, w_bf, lbl)


def _ctx_state_kernel(gfw_ref, gbw_ref, v_ref, sfw_ref, sbw_ref):
    L = gfw_ref.shape[1]
    r = lax.broadcasted_iota(jnp.int32, (L, L), 0)
    c = lax.broadcasted_iota(jnp.int32, (L, L), 1)
    upper = jnp.where(c > r, 1.0, 0.0).astype(BF16)
    lower = jnp.where(c < r, 1.0, 0.0).astype(BF16)
    v = v_ref[0]
    for g_ref, tri, s_ref in ((gfw_ref, upper, sfw_ref), (gbw_ref, lower, sbw_ref)):
        g = g_ref[0]
        kw = ((1.0 - jnp.exp(g)) * jnp.exp(_dot_exact_lhs(tri, g))).astype(BF16)
        for h in range(HG_HEADS):
            sl = slice(h * HG_DK, (h + 1) * HG_DK)
            s_ref[0, h] = _dot_tn(kw[:, sl], v[:, sl])


def _ctx_state(g_cfw, g_cbw, v_c):
    B, L, _ = g_cfw.shape
    tok = pl.BlockSpec((1, L, HG_WIDTH), lambda b: (b, 0, 0))
    st = pl.BlockSpec((1, HG_HEADS, HG_DK, HG_DK), lambda b: (b, 0, 0, 0))
    sshape = jax.ShapeDtypeStruct((B, HG_HEADS, HG_DK, HG_DK), F32)
    return pl.pallas_call(
        _ctx_state_kernel,
        grid=(B,),
        in_specs=[tok, tok, tok],
        out_specs=[st, st],
        out_shape=[sshape, sshape],
        compiler_params=pltpu.CompilerParams(vmem_limit_bytes=VMEM_LIMIT),
        name="ctx_state",
    )(g_cfw, g_cbw, v_c)


SUBLANES = 8
HG_OUTPUT_LAG = 2


def _hg_masks():
    C = HG_CHUNK
    t = np.arange(C)[:, None]
    s = np.arange(C)[None, :]
    masks = []
    for j in range(N_LEVELS):
        hs = 1 << j
        same = (t // (2 * hs)) == (s // (2 * hs))
        masks.append(same & ((t % (2 * hs)) >= hs) & ((s % (2 * hs)) < hs))
    masks.append(t == s)
    masks = np.stack(masks).astype(np.float32)
    return np.stack([masks, masks[:, ::-1, ::-1]])


class _HgDirection:
    def __init__(self, q_ref, v_ref, g_ref, st_ref, o_ref, mask_ref, rev):
        C = HG_CHUNK
        self.q_ref, self.v_ref, self.st_ref, self.o_ref, self.rev = q_ref, v_ref, st_ref, o_ref, rev
        d = 1 if rev else 0
        self.g = g = g_ref[0] * LOG2E
        self.pos = lax.broadcasted_iota(jnp.int32, (C, 2 * HG_DK), 0) % SUBLANES
        self.f = jnp.exp2(g)
        self.k = 1.0 - self.f
        self.level_mask = [mask_ref[d, j] > 0.5 for j in range(N_LEVELS + 1)]
        self.lane = lax.broadcasted_iota(jnp.int32, (SUBLANES, C), 1)
        t = lax.broadcasted_iota(jnp.int32, (C, HG_DK), 0)
        self.query_rows = {j: ((t % (2 << j)) < (1 << j)) if rev else ((t % (2 << j)) >= (1 << j))
                           for j in range(1, N_LEVELS) if (1 << j) < SUBLANES}
        n_pairs = q_ref.shape[1] // (2 * C)
        self.pairs = list(range(n_pairs - 1, -1, -1) if rev else range(n_pairs))
        self.st = st_ref[...]

    def decays(self, p):
        C, S, L = HG_CHUNK, SUBLANES, 2 * HG_DK
        rev = self.rev
        pos, pos_s = self.pos, lax.broadcasted_iota(jnp.int32, (S, L), 0)
        row = lambda a, r: jnp.broadcast_to(a[r:r + 1, :], (S, L))
        group = lambda a, i: a[i * S:(i + 1) * S]

        halves = []
        for ch in (2 * p, 2 * p + 1):
            c = self.g[ch * C:(ch + 1) * C]
            s = 1
            while s < S:
                if rev:
                    c = c + jnp.where(pos[:, :HG_DK] < S - s, pltpu.roll(c, C - s, axis=0), 0.0)
                else:
                    c = c + jnp.where(pos[:, :HG_DK] >= s, pltpu.roll(c, s, axis=0), 0.0)
                s *= 2
            halves.append(c)
        c = jnp.concatenate(halves, axis=1)

        def split_in_group(hs):
            blk = 2 * hs
            query = (pos_s % blk) < hs if rev else (pos_s % blk) >= hs
            out = []
            for i in range(C // S):
                mid = row(c, i * S + (hs if rev else hs - 1))
                for a in range(1, S // blk):
                    mid = jnp.where(pos_s >= a * blk, row(c, i * S + a * blk + (hs if rev else hs - 1)), mid)
                out.append(jnp.where(query, group(c, i) - mid, mid - group(c, i)))
            return jnp.concatenate(out, axis=0)

        parts = [split_in_group(1 << j) for j in range(1, N_LEVELS) if 2 << j <= S]

        b_groups, carried = [None] * (C // S), None
        for i in (range(C // S - 1, -1, -1) if rev else range(C // S)):
            b_groups[i] = group(c, i) if carried is None else group(c, i) + carried
            total = row(c, i * S + (0 if rev else S - 1))
            carried = total if carried is None else carried + total
        b = jnp.concatenate(b_groups, axis=0)

        rows = {}

        def b_row(m):
            if m not in rows:
                rows[m] = row(b, m)
            return rows[m]

        for j in range(1, N_LEVELS):
            hs = 1 << j
            if 2 * hs <= S:
                continue
            groups = []
            for r in range(0, C, S):
                start = r // (2 * hs) * (2 * hs)
                b_mid = b_row(start + (hs if rev else hs - 1))
                query_side = (r - start >= hs) != rev
                groups.append(b[r:r + S] - b_mid if query_side else b_mid - b[r:r + S])
            parts.append(jnp.concatenate(groups, axis=0))
        parts.append(b)
        parts.append(jnp.concatenate([b_row(0 if rev else C - 1)] * (C // S), axis=0) - b)
        return [jnp.exp2(x) for x in parts]

    def scores(self, p, e2):
        C = HG_CHUNK
        out = []
        for c in ((2 * p + 1, 2 * p) if self.rev else (2 * p, 2 * p + 1)):
            rows = slice(c * C, (c + 1) * C)
            lanes = slice((c - 2 * p) * HG_DK, (c - 2 * p + 1) * HG_DK)
            qc = self.q_ref[0, rows, :].astype(F32)
            kc = self.k[rows]
            k_next = pltpu.roll(kc, C - 1 if self.rev else 1, axis=0)
            a = jnp.where(self.level_mask[N_LEVELS], jnp.sum(qc * kc, axis=1, keepdims=True), 0.0)
            a = jnp.where(self.level_mask[0], jnp.sum(qc * self.f[rows] * k_next, axis=1, keepdims=True), a)
            for j in range(1, N_LEVELS):
                hs = 1 << j
                if hs < SUBLANES:
                    z = (jnp.where(self.query_rows[j], qc, kc) * e2[j - 1][:, lanes]).astype(BF16)
                    a = jnp.where(self.level_mask[j], _dot_nt(z, z), a)
                    continue
                is_query = lambda r: ((r % (2 * hs)) >= hs) != self.rev
                starts = range(0, C, SUBLANES)
                z = jnp.concatenate([(qc if is_query(r) else kc)[r:r + SUBLANES] for r in starts], axis=0)
                z = (z * e2[j - 1][:, lanes]).astype(BF16)
                q_starts = [r for r in starts if is_query(r)]
                x = _dot_nt(jnp.concatenate([z[r:r + SUBLANES] for r in q_starts], axis=0), z)
                pieces = []
                for r in starts:
                    if not is_query(r):
                        pieces.append(a[r:r + SUBLANES])
                        continue
                    key_lo = r // (2 * hs) * (2 * hs) + (hs if self.rev else 0)
                    on_keys = (self.lane >= key_lo) & (self.lane < key_lo + hs)
                    i = q_starts.index(r) * SUBLANES
                    pieces.append(jnp.where(on_keys, x[i:i + SUBLANES], a[r:r + SUBLANES]))
                a = jnp.concatenate(pieces, axis=0)
            e_b = e2[N_LEVELS - 1][:, lanes]
            total = e_b[0:1] if self.rev else e_b[C - 1:C]
            q_dec = jnp.concatenate([(qc * e_b).astype(BF16), a.astype(BF16)], axis=1)
            out.append((rows, q_dec, (kc * e2[N_LEVELS][:, lanes]).astype(BF16), total))
        return out

    def outputs(self, chunk_terms):
        for rows, q_dec, k_dec, total in chunk_terms:
            vc = self.v_ref[0, rows, :]
            self.o_ref[0, rows, :] = _dot(q_dec, jnp.concatenate([self.st.astype(BF16), vc], axis=0))
            total_col = jnp.transpose(jnp.broadcast_to(total, (SUBLANES, HG_DK)))[:, 0:1]
            self.st = total_col * self.st + _dot_tn(k_dec, vc)

    def finish(self):
        self.st_ref[...] = self.st


def _hgrn_kernel(qf_ref, vf_ref, gf_ref, qb_ref, vb_ref, gb_ref, s0f_ref, s0b_ref, mask_ref,
                 of_ref, ob_ref, stf, stb):
    @pl.when(pl.program_id(2) == 0)
    def _():
        stf[...] = s0f_ref[0, 0]
        stb[...] = s0b_ref[0, 0]

    fw = _HgDirection(qf_ref, vf_ref, gf_ref, stf, of_ref, mask_ref, False)
    bw = _HgDirection(qb_ref, vb_ref, gb_ref, stb, ob_ref, mask_ref, True)
    units = [(dirn, p) for pf, pb in zip(fw.pairs, bw.pairs) for dirn, p in ((fw, pf), (bw, pb))]
    e_next = units[0][0].decays(units[0][1])
    pending = []
    for i, (dirn, p) in enumerate(units):
        e_cur = e_next
        if i + 1 < len(units):
            e_next = units[i + 1][0].decays(units[i + 1][1])
        pending.append((dirn, dirn.scores(p, e_cur)))
        if len(pending) > HG_OUTPUT_LAG:
            done, terms = pending.pop(0)
            done.outputs(terms)
    for done, terms in pending:
        done.outputs(terms)
    fw.finish()
    bw.finish()


def _hgrn(q_hg, v_hg, g_fw, g_bw, s_fw, s_bw, tb):
    B, T, _ = q_hg.shape
    nb = T // tb
    fwd = pl.BlockSpec((1, tb, HG_DK), lambda b, h, i: (b, i, h))
    bwd = pl.BlockSpec((1, tb, HG_DK), lambda b, h, i: (b, nb - 1 - i, h))
    st = pl.BlockSpec((1, 1, HG_DK, HG_DK), lambda b, h, i: (b, h, 0, 0))
    masks = jnp.asarray(_hg_masks())
    oshape = jax.ShapeDtypeStruct((B, T, HG_WIDTH), F32)
    return pl.pallas_call(
        _hgrn_kernel,
        grid=(B, HG_HEADS, nb),
        in_specs=[fwd, fwd, fwd, bwd, bwd, bwd, st, st,
                  pl.BlockSpec(masks.shape, lambda b, h, i: (0, 0, 0, 0))],
        out_specs=[fwd, bwd],
        out_shape=[oshape, oshape],
        scratch_shapes=[pltpu.VMEM((HG_DK, HG_DK), F32), pltpu.VMEM((HG_DK, HG_DK), F32)],
        compiler_params=pltpu.CompilerParams(
            dimension_semantics=("parallel", "parallel", "arbitrary"), vmem_limit_bytes=VMEM_LIMIT),
        name="hgrn",
    )(q_hg, v_hg, g_fw, q_hg, v_hg, g_bw, s_fw, s_bw, masks)


def _na_bias_kernel(taps_ref, valid_ref, o_ref):
    W = GRID_W
    for i in range(taps_ref.shape[2]):
        taps = jnp.broadcast_to(taps_ref[0, 0, i:i + 1, :], (W, taps_ref.shape[3]))
        shifted = pltpu.roll(taps, 0, axis=1, stride=1, stride_axis=0)
        o_ref[0, 0, i * W:(i + 1) * W, :] = jnp.where(valid_ref[0, i % NA_STEP_ROWS] > 0.0, shifted, NEG)


def _na_plan(rows):
    assert rows % NA_STEP_ROWS == 0 and rows >= NA_WIN_ROWS and (rows - NA_WIN_ROWS) % NA_STEP_ROWS == 0
    variants, starts = [], []
    for i in range(rows // NA_STEP_ROWS):
        w0 = int(np.clip(NA_STEP_ROWS * i - NA_KR // 2, 0, rows - NA_WIN_ROWS))
        geom = tuple((int(np.clip(r - NA_KR // 2, 0, rows - NA_KR)) - w0, r - w0)
                     for r in range(NA_STEP_ROWS * i, NA_STEP_ROWS * (i + 1)))
        if not variants or variants[-1] != geom:
            assert geom not in variants
            variants.append(geom)
            starts.append(i)
    return variants, starts


def _na_bias(rpb, rows):
    W = GRID_W
    n_dr = 2 * NA_KR - 1
    variants, _ = _na_plan(rows)
    dr = np.array([[[j - qrow + NA_KR - 1 if first <= j < first + NA_KR else n_dr for j in range(NA_WIN_ROWS)]
                    for first, qrow in geom] for geom in variants])
    cols = np.arange(W)
    c0 = np.clip(cols - NA_KC // 2, 0, W - NA_KC)
    col_ok = (cols[None, :] >= c0[:, None]) & (cols[None, :] < c0[:, None] + NA_KC)
    valid = (dr < n_dr)[:, :, None, :, None] & col_ok[None, None, :, None, :]
    valid = valid.reshape(len(variants), NA_STEP_ROWS, W, NA_WIN_ROWS * W).astype(np.float32)

    rpb_ext = jnp.zeros((NA_HEADS, n_dr + 1, 2 * NA_KC - 1), F32).at[:, :n_dr].set(rpb.astype(F32))
    picked = jnp.take(rpb_ext, jnp.asarray(dr.reshape(-1), dtype=jnp.int32), axis=1)
    picked = picked.reshape((NA_HEADS,) + dr.shape + (2 * NA_KC - 1,))
    ahead = picked[..., NA_KC - 1:]
    behind = jnp.roll(picked, -1, axis=3)[..., :NA_KC - 1]
    gap = jnp.zeros(picked.shape[:-1] + (W - 2 * NA_KC + 1,), F32)
    taps = jnp.concatenate([ahead, gap, behind], axis=-1)
    taps = taps.reshape(NA_HEADS // NA_GROUP, NA_GROUP, len(variants), NA_STEP_ROWS, NA_WIN_ROWS * W)
    taps = taps.transpose(2, 0, 1, 3, 4).reshape(len(variants), NA_HEADS // NA_GROUP, NA_GROUP * NA_STEP_ROWS,
                                                 NA_WIN_ROWS * W)
    out_block = (1, 1, NA_GROUP * NA_TOK, NA_WIN_ROWS * W)
    return pl.pallas_call(
        _na_bias_kernel,
        grid=(len(variants), NA_HEADS // NA_GROUP),
        in_specs=[pl.BlockSpec((1, 1) + taps.shape[2:], lambda v, g: (v, g, 0, 0)),
                  pl.BlockSpec((1,) + valid.shape[1:], lambda v, g: (v, 0, 0, 0))],
        out_specs=pl.BlockSpec(out_block, lambda v, g: (v, g, 0, 0)),
        out_shape=jax.ShapeDtypeStruct((len(variants), NA_HEADS // NA_GROUP) + out_block[2:], F32),
        name="na_bias",
    )(taps, jnp.asarray(valid))


def _natten_kernel(q_ref, kt_ref, v_ref, kct_ref, vc_ref, *rest):
    bias_refs, o_ref = rest[:-1], rest[-1]
    W = GRID_W
    rows = v_ref.shape[1] // W
    n_kblk = NA_WIN_ROWS // NA_STEP_ROWS
    rb = lax.broadcasted_iota(jnp.int32, (NA_GROUP * NA_TOK, NA_GW), 0) // NA_TOK
    cb = lax.broadcasted_iota(jnp.int32, (NA_GROUP * NA_TOK, NA_GW), 1) // NA_HEAD_DIM
    diag = rb == cb
    groups = [slice(grp * NA_GW, (grp + 1) * NA_GW) for grp in range(NA_HEADS // NA_GROUP)]

    units = []
    for u, bias_ref in enumerate(bias_refs):
        pair = len(bias_refs) * pl.program_id(1) + u
        w0 = jnp.clip(NA_STEP_ROWS * pair - NA_KR // 2, 0, rows - NA_WIN_ROWS)
        blk0 = w0 // NA_STEP_ROWS
        v_rows = pl.ds(pl.multiple_of(w0 * W, NA_TOK), NA_WIN_ROWS * W)
        for grp, gs in enumerate(groups):
            qg = q_ref[0, u * NA_TOK:(u + 1) * NA_TOK, gs]
            kt = jnp.concatenate([kt_ref[0, blk0 + j, gs, :] for j in range(n_kblk)], axis=1)
            kct = jnp.concatenate([kct_ref[0, j, gs, :] for j in range(kct_ref.shape[1])], axis=1)
            qbd = jnp.where(diag, jnp.concatenate([qg] * NA_GROUP, axis=0), jnp.zeros_like(qg[:1]))
            units.append((u, gs, v_rows, _dot(qbd, kt) + bias_ref[0, grp], _dot(qbd, kct)))
    for u, gs, v_rows, s_win, s_ctx in units:
        vw = v_ref[0, v_rows, gs]
        m = jnp.maximum(jnp.max(s_win, axis=-1, keepdims=True), jnp.max(s_ctx, axis=-1, keepdims=True))
        p_win = jnp.exp(s_win - m)
        p_ctx = jnp.exp(s_ctx - m)
        denom = jnp.sum(p_win, axis=-1, keepdims=True) + jnp.sum(p_ctx, axis=-1, keepdims=True)
        of = (_dot(p_win.astype(BF16), vw) + _dot(p_ctx.astype(BF16), vc_ref[0, :, gs])) / denom
        of = jnp.where(diag, of, 0.0)
        og = of[0:NA_TOK]
        for h in range(1, NA_GROUP):
            og = og + of[h * NA_TOK:(h + 1) * NA_TOK]
        o_ref[0, u * NA_TOK:(u + 1) * NA_TOK, gs] = og.astype(o_ref.dtype)


def _natten(q, kt, v, kct, vc, bias):
    B, T, _ = q.shape
    rows = T // GRID_W
    _, starts = _na_plan(rows)
    n_pairs = rows // NA_STEP_ROWS
    assert n_pairs % NA_STEP_PAIRS == 0
    tok = pl.BlockSpec((1, NA_STEP_PAIRS * NA_TOK, NA_WIDTH), lambda b, i: (b, i, 0))
    whole = lambda a: pl.BlockSpec((1,) + a.shape[1:], lambda b, i: (b,) + (0,) * (a.ndim - 1))

    def bias_spec(u):
        def index_map(b, i):
            pair = NA_STEP_PAIRS * i + u
            return (sum((pair >= s).astype(jnp.int32) for s in starts[1:]), 0, 0, 0)
        return pl.BlockSpec((1,) + bias.shape[1:], index_map)

    return pl.pallas_call(
        _natten_kernel,
        grid=(B, n_pairs // NA_STEP_PAIRS),
        in_specs=[tok, whole(kt), whole(v), whole(kct), whole(vc)] + [bias_spec(u) for u in range(NA_STEP_PAIRS)],
        out_specs=tok,
        out_shape=jax.ShapeDtypeStruct((B, T, NA_WIDTH), BF16),
        compiler_params=pltpu.CompilerParams(
            dimension_semantics=("parallel", "arbitrary"), vmem_limit_bytes=VMEM_LIMIT),
        name="natten",
    )(q, kt, v, kct, vc, *([bias] * NA_STEP_PAIRS))


def _post_mix_kernel(yna_ref, of_ref, ob_ref, gate_ref, hgw_ref, wo_ref, x_ref, gtm_ref, nmix_ref,
                     scf_ref, shf_ref, gtf_ref, npre_ref, npost_ref, w1_ref, w2_ref, o_ref, act_ref):
    o = of_ref[0] + ob_ref[0]
    gate = gate_ref[0].astype(F32)
    hgw = hgw_ref[...]
    parts = []
    for h in range(HG_HEADS):
        sl = slice(h * HG_DK, (h + 1) * HG_DK)
        parts.append((_rms(o[:, sl], hgw) * gate[:, sl]).astype(BF16))
    y_hg = jnp.concatenate(parts, axis=-1)
    y = _dot(yna_ref[0], wo_ref[:NA_WIDTH, :]) + _dot(y_hg, wo_ref[NA_WIDTH:, :])
    x1 = x_ref[0] + gtm_ref[0] * _rms(y, nmix_ref[...])
    d_ff = w2_ref.shape[0]
    hb = (_rms(x1, npre_ref[...]) * (1.0 + scf_ref[0]) + shf_ref[0]).astype(BF16)
    for j in range(0, d_ff, FFN_COLS):
        gate_j = _dot(hb, w1_ref[:, j:j + FFN_COLS])
        up_j = _dot(hb, w1_ref[:, d_ff + j:d_ff + j + FFN_COLS])
        act_ref[:, j:j + FFN_COLS] = (_silu(gate_j) * up_j).astype(BF16)
    z = _dot(act_ref[...], w2_ref[...])
    o_ref[0] = x1 + gtf_ref[0] * _rms(z, npost_ref[...])


def _post_mix(y_na, o_fw, o_bw, gate, hgw, wo_bf, x, gt_m, nmix, sc_f, sh_f, gt_f, npre, npost, w1_bf, w2_bf, tm):
    B, T, _ = x.shape
    d_ff = w2_bf.shape[0]
    assert d_ff % FFN_COLS == 0
    tok = lambda n: pl.BlockSpec((1, tm, n), lambda b, i: (b, i, 0))
    mod = pl.BlockSpec((1, 1, D_MODEL), lambda b, i: (b, 0, 0))
    const = lambda shape: pl.BlockSpec(shape, lambda b, i: (0,) * len(shape), pipeline_mode=pl.Buffered(1))
    return pl.pallas_call(
        _post_mix_kernel,
        grid=(B, T // tm),
        in_specs=[tok(NA_WIDTH), tok(HG_WIDTH), tok(HG_WIDTH), tok(HG_WIDTH), const((1, HG_DK)),
                  const(wo_bf.shape), tok(D_MODEL), mod, const((1, D_MODEL)),
                  mod, mod, mod, const((1, D_MODEL)), const((1, D_MODEL)),
                  const(w1_bf.shape), const(w2_bf.shape)],
        out_specs=tok(D_MODEL),
        out_shape=jax.ShapeDtypeStruct(x.shape, F32),
        scratch_shapes=[pltpu.VMEM((tm, d_ff), BF16)],
        compiler_params=pltpu.CompilerParams(
            dimension_semantics=("parallel", "parallel"), vmem_limit_bytes=VMEM_LIMIT),
        name="post_mix",
    )(y_na, o_fw, o_bw, gate, hgw, wo_bf, x, gt_m, nmix, sc_f, sh_f, gt_f, npre, npost, w1_bf, w2_bf)


def kernel(x, c, ctx, c_ctx, w_ada, b_ada, norm_mix_pre, norm_mix_post, norm_ffn_pre, norm_ffn_post,
           w_in, na_rpb, hg_lb_logits, hg_norm_w, w_out, w_ffn_in, w_ffn_out):
    B, T, D = x.shape
    assert w_ada.shape[0] == 1, "single-layer stack"
    rows = T // GRID_W

    cv = jnp.zeros((8, D), F32).at[:B].set(c).at[B].set(c_ctx)
    mod = _ada(cv, w_ada[0], b_ada[0][None, :])
    sh_m, sc_m, gt_m, sh_f, sc_f, gt_f = [mod[:, i * D:(i + 1) * D] for i in range(N_MOD)]
    lat = lambda m: m[:B, None, :]
    cx = lambda m: jnp.broadcast_to(m[B][None, None, :], (B, 1, D))

    w_in_bf = w_in[0].astype(BF16)
    lbl = hg_lb_logits.reshape(hg_lb_logits.shape[0], 2 * HG_WIDTH)
    nw_pre = norm_mix_pre[0][None, :]

    q_na, kt_na, v_na, q_hg, g_fw, g_bw, v_hg, gate = _in_proj(x, lat(sc_m), lat(sh_m), nw_pre, w_in_bf, lbl, 1024)
    _, kt_c, v_c, _, g_cfw, g_cbw, vhg_c, _ = _in_proj(ctx, cx(sc_m), cx(sh_m), nw_pre, w_in_bf, lbl,
                                                       ctx.shape[1])

    s_fw, s_bw = _ctx_state(g_cfw, g_cbw, vhg_c)
    o_fw, o_bw = _hgrn(q_hg, v_hg, g_fw, g_bw, s_fw, s_bw, 2048)

    y_na = _natten(q_na, kt_na, v_na, kt_c, v_c, _na_bias(na_rpb[0], rows))

    return _post_mix(y_na, o_fw, o_bw, gate, hg_norm_w[0][None, :], w_out[0].astype(BF16), x, lat(gt_m),
                     norm_mix_post[0][None, :], lat(sc_f), lat(sh_f), lat(gt_f), norm_ffn_pre[0][None, :],
                     norm_ffn_post[0][None, :], w_ffn_in[0].astype(BF16), w_ffn_out[0].astype(BF16), 512)
```

```python
import functools

import jax
import jax.numpy as jnp
import numpy as np
from jax import lax
from jax.experimental import pallas as pl
from jax.experimental.pallas import tpu as pltpu

D_MODEL = 1024
GRID_W = 64
NA_HEADS = 8
NA_HEAD_DIM = 64
NA_WIDTH = NA_HEADS * NA_HEAD_DIM
NA_KR = 8
NA_KC = 16
HG_HEADS = 4
HG_DK = 128
HG_WIDTH = HG_HEADS * HG_DK
HG_CHUNK = 64
N_LEVELS = 6
N_MOD = 6
EPS = 1e-6
NEG = -1e30
LOG2E = 1.4426950408889634

NA_GROUP = 4
NA_GW = NA_GROUP * NA_HEAD_DIM
NA_STEP_ROWS = 2
NA_TOK = NA_STEP_ROWS * GRID_W
NA_WIN_ROWS = NA_KR + NA_STEP_ROWS
NA_STEP_PAIRS = 2

F32 = jnp.float32
BF16 = jnp.bfloat16

VMEM_LIMIT = 56 * 1024 * 1024
FFN_COLS = 256


def _silu(x):
    return x * jax.nn.sigmoid(x)


def _dot(a, b):
    return jnp.dot(a, b, preferred_element_type=F32)


def _dot_nt(a, b):
    return lax.dot_general(a, b, (((1,), (1,)), ((), ())), preferred_element_type=F32)


def _dot_tn(a, b):
    return lax.dot_general(a, b, (((0,), (0,)), ((), ())), preferred_element_type=F32)


def _split3(x):
    x1 = x.astype(BF16)
    r1 = x - x1.astype(F32)
    x2 = r1.astype(BF16)
    r2 = r1 - x2.astype(F32)
    return x1, x2, r2.astype(BF16)


def _dot_exact_lhs(t, x):
    x1, x2, x3 = _split3(x)
    return _dot(t, x1) + _dot(t, x2) + _dot(t, x3)


def _rms(x, w):
    return x * lax.rsqrt(jnp.mean(x * x, axis=-1, keepdims=True) + EPS) * w


def _ada_kernel(cv_ref, w_ref, b_ref, o_ref):
    s = _silu(cv_ref[...])
    s1, s2, s3 = _split3(s)
    w1, w2, w3 = _split3(w_ref[...])
    acc = _dot(s1, w1) + (_dot(s1, w2) + _dot(s2, w1)) + (_dot(s1, w3) + _dot(s2, w2) + _dot(s3, w1))
    o_ref[...] = acc + b_ref[...]


def _ada(cv, w_ada, b_ada):
    n = w_ada.shape[1]
    tn = 1536
    return pl.pallas_call(
        _ada_kernel,
        grid=(n // tn,),
        in_specs=[
            pl.BlockSpec((8, D_MODEL), lambda j: (0, 0)),
            pl.BlockSpec((D_MODEL, tn), lambda j: (0, j)),
            pl.BlockSpec((1, tn), lambda j: (0, j)),
        ],
        out_specs=pl.BlockSpec((8, tn), lambda j: (0, j)),
        out_shape=jax.ShapeDtypeStruct((8, n), F32),
        compiler_params=pltpu.CompilerParams(vmem_limit_bytes=VMEM_LIMIT),
        name="ada",
    )(cv, w_ada, b_ada)


def _in_proj_kernel(x_ref, sc_ref, sh_ref, nw_ref, w_ref, lbl_ref,
                    qna_ref, knat_ref, vna_ref, qhg_ref, gfw_ref, gbw_ref, vhg_ref, gate_ref):
    x = x_ref[0]
    h = _rms(x, nw_ref[...]) * (1.0 + sc_ref[0]) + sh_ref[0]
    hb = h.astype(BF16)

    def proj(i):
        return _dot(hb, w_ref[:, i * 512:(i + 1) * 512])

    lbl = lbl_ref[...]
    e = jnp.exp(lbl - jnp.max(lbl, axis=0, keepdims=True))
    lb = e[0:1] / jnp.sum(e, axis=0, keepdims=True)

    qna_ref[0] = (proj(0) * (NA_HEAD_DIM ** -0.5)).astype(BF16)
    k_t = proj(1).astype(BF16).T
    for j in range(knat_ref.shape[1]):
        knat_ref[0, j] = k_t[:, j * NA_TOK:(j + 1) * NA_TOK]
    vna_ref[0] = proj(2).astype(BF16)
    qhg_ref[0] = _silu(proj(3)).astype(BF16)
    lb_f = lb[:, :HG_WIDTH]
    lb_b = lb[:, HG_WIDTH:]
    gfw_ref[0] = jnp.log(lb_f + (1.0 - lb_f) * jax.nn.sigmoid(proj(4)))
    gbw_ref[0] = jnp.log(lb_b + (1.0 - lb_b) * jax.nn.sigmoid(proj(5)))
    vhg_ref[0] = proj(6).astype(BF16)
    gate_ref[0] = _silu(proj(7)).astype(BF16)


def _in_proj(x, sc, sh, nw, w_bf, lbl, tm):
    B, T, _ = x.shape
    tok = lambda b, i: (b, i, 0)
    out_bf = jax.ShapeDtypeStruct((B, T, 512), BF16)
    out_f = jax.ShapeDtypeStruct((B, T, 512), F32)
    ospec = pl.BlockSpec((1, tm, 512), tok)
    return pl.pallas_call(
        _in_proj_kernel,
        grid=(B, T // tm),
        in_specs=[
            pl.BlockSpec((1, tm, D_MODEL), tok),
            pl.BlockSpec((1, 1, D_MODEL), lambda b, i: (b, 0, 0)),
            pl.BlockSpec((1, 1, D_MODEL), lambda b, i: (b, 0, 0)),
            pl.BlockSpec((1, D_MODEL), lambda b, i: (0, 0)),
            pl.BlockSpec(w_bf.shape, lambda b, i: (0, 0)),
            pl.BlockSpec(lbl.shape, lambda b, i: (0, 0)),
        ],
        out_specs=[ospec, pl.BlockSpec((1, tm // NA_TOK, NA_WIDTH, NA_TOK), lambda b, i: (b, i, 0, 0))]
        + [ospec] * 6,
        out_shape=[out_bf, jax.ShapeDtypeStruct((B, T // NA_TOK, NA_WIDTH, NA_TOK), BF16),
                   out_bf, out_bf, out_f, out_f, out_bf, out_bf],
        compiler_params=pltpu.CompilerParams(
            dimension_semantics=("parallel", "parallel"), vmem_limit_bytes=VMEM_LIMIT),
        name="in_proj",
    )(x, sc, sh, nw, w_bf, lbl)


def _ctx_state_kernel(gfw_ref, gbw_ref, v_ref, sfw_ref, sbw_ref):
    L = gfw_ref.shape[1]
    r = lax.broadcasted_iota(jnp.int32, (L, L), 0)
    c = lax.broadcasted_iota(jnp.int32, (L, L), 1)
    upper = jnp.where(c > r, 1.0, 0.0).astype(BF16)
    lower = jnp.where(c < r, 1.0, 0.0).astype(BF16)
    v = v_ref[0]
    for g_ref, tri, s_ref in ((gfw_ref, upper, sfw_ref), (gbw_ref, lower, sbw_ref)):
        g = g_ref[0]
        kw = ((1.0 - jnp.exp(g)) * jnp.exp(_dot_exact_lhs(tri, g))).astype(BF16)
        for h in range(HG_HEADS):
            sl = slice(h * HG_DK, (h + 1) * HG_DK)
            s_ref[0, h] = _dot_tn(kw[:, sl], v[:, sl])


def _ctx_state(g_cfw, g_cbw, v_c):
    B, L, _ = g_cfw.shape
    tok = pl.BlockSpec((1, L, HG_WIDTH), lambda b: (b, 0, 0))
    st = pl.BlockSpec((1, HG_HEADS, HG_DK, HG_DK), lambda b: (b, 0, 0, 0))
    sshape = jax.ShapeDtypeStruct((B, HG_HEADS, HG_DK, HG_DK), F32)
    return pl.pallas_call(
        _ctx_state_kernel,
        grid=(B,),
        in_specs=[tok, tok, tok],
        out_specs=[st, st],
        out_shape=[sshape, sshape],
        compiler_params=pltpu.CompilerParams(vmem_limit_bytes=VMEM_LIMIT),
        name="ctx_state",
    )(g_cfw, g_cbw, v_c)


SUBLANES = 8
HG_OUTPUT_LAG = 2


def _hg_masks():
    C = HG_CHUNK
    t = np.arange(C)[:, None]
    s = np.arange(C)[None, :]
    masks = []
    for j in range(N_LEVELS):
        hs = 1 << j
        same = (t // (2 * hs)) == (s // (2 * hs))
        masks.append(same & ((t % (2 * hs)) >= hs) & ((s % (2 * hs)) < hs))
    masks.append(t == s)
    masks = np.stack(masks).astype(np.float32)
    return np.stack([masks, masks[:, ::-1, ::-1]])


class _HgDirection:
    def __init__(self, q_ref, v_ref, g_ref, st_ref, o_ref, mask_ref, rev):
        C = HG_CHUNK
        self.q_ref, self.v_ref, self.st_ref, self.o_ref, self.rev = q_ref, v_ref, st_ref, o_ref, rev
        d = 1 if rev else 0
        self.g = g = g_ref[0] * LOG2E
        self.pos = lax.broadcasted_iota(jnp.int32, (C, 2 * HG_DK), 0) % SUBLANES
        self.f = jnp.exp2(g)
        self.k = 1.0 - self.f
        self.level_mask = [mask_ref[d, j] > 0.5 for j in range(N_LEVELS + 1)]
        self.lane = lax.broadcasted_iota(jnp.int32, (SUBLANES, C), 1)
        t = lax.broadcasted_iota(jnp.int32, (C, HG_DK), 0)
        self.query_rows = {j: ((t % (2 << j)) < (1 << j)) if rev else ((t % (2 << j)) >= (1 << j))
                           for j in range(1, N_LEVELS) if (1 << j) < SUBLANES}
        n_pairs = q_ref.shape[1] // (2 * C)
        self.pairs = list(range(n_pairs - 1, -1, -1) if rev else range(n_pairs))
        self.st = st_ref[...]

    def decays(self, p):
        C, S, L = HG_CHUNK, SUBLANES, 2 * HG_DK
        rev = self.rev
        pos, pos_s = self.pos, lax.broadcasted_iota(jnp.int32, (S, L), 0)
        row = lambda a, r: jnp.broadcast_to(a[r:r + 1, :], (S, L))
        group = lambda a, i: a[i * S:(i + 1) * S]

        halves = []
        for ch in (2 * p, 2 * p + 1):
            c = self.g[ch * C:(ch + 1) * C]
            s = 1
            while s < S:
                if rev:
                    c = c + jnp.where(pos[:, :HG_DK] < S - s, pltpu.roll(c, C - s, axis=0), 0.0)
                else:
                    c = c + jnp.where(pos[:, :HG_DK] >= s, pltpu.roll(c, s, axis=0), 0.0)
                s *= 2
            halves.append(c)
        c = jnp.concatenate(halves, axis=1)

        def split_in_group(hs):
            blk = 2 * hs
            query = (pos_s % blk) < hs if rev else (pos_s % blk) >= hs
            out = []
            for i in range(C // S):
                mid = row(c, i * S + (hs if rev else hs - 1))
                for a in range(1, S // blk):
                    mid = jnp.where(pos_s >= a * blk, row(c, i * S + a * blk + (hs if rev else hs - 1)), mid)
                out.append(jnp.where(query, group(c, i) - mid, mid - group(c, i)))
            return jnp.concatenate(out, axis=0)

        parts = [split_in_group(1 << j) for j in range(1, N_LEVELS) if 2 << j <= S]

        b_groups, carried = [None] * (C // S), None
        for i in (range(C // S - 1, -1, -1) if rev else range(C // S)):
            b_groups[i] = group(c, i) if carried is None else group(c, i) + carried
            total = row(c, i * S + (0 if rev else S - 1))
            carried = total if carried is None else carried + total
        b = jnp.concatenate(b_groups, axis=0)

        rows = {}

        def b_row(m):
            if m not in rows:
                rows[m] = row(b, m)
            return rows[m]

        for j in range(1, N_LEVELS):
            hs = 1 << j
            if 2 * hs <= S:
                continue
            groups = []
            for r in range(0, C, S):
                start = r // (2 * hs) * (2 * hs)
                b_mid = b_row(start + (hs if rev else hs - 1))
                query_side = (r - start >= hs) != rev
                groups.append(b[r:r + S] - b_mid if query_side else b_mid - b[r:r + S])
            parts.append(jnp.concatenate(groups, axis=0))
        parts.append(b)
        parts.append(jnp.concatenate([b_row(0 if rev else C - 1)] * (C // S), axis=0) - b)
        return [jnp.exp2(x) for x in parts]

    def scores(self, p, e2):
        C = HG_CHUNK
        out = []
        for c in ((2 * p + 1, 2 * p) if self.rev else (2 * p, 2 * p + 1)):
            rows = slice(c * C, (c + 1) * C)
            lanes = slice((c - 2 * p) * HG_DK, (c - 2 * p + 1) * HG_DK)
            qc = self.q_ref[0, rows, :].astype(F32)
            kc = self.k[rows]
            k_next = pltpu.roll(kc, C - 1 if self.rev else 1, axis=0)
            a = jnp.where(self.level_mask[N_LEVELS], jnp.sum(qc * kc, axis=1, keepdims=True), 0.0)
            a = jnp.where(self.level_mask[0], jnp.sum(qc * self.f[rows] * k_next, axis=1, keepdims=True), a)
            for j in range(1, N_LEVELS):
                hs = 1 << j
                if hs < SUBLANES:
                    z = (jnp.where(self.query_rows[j], qc, kc) * e2[j - 1][:, lanes]).astype(BF16)
                    a = jnp.where(self.level_mask[j], _dot_nt(z, z), a)
                    continue
                is_query = lambda r: ((r % (2 * hs)) >= hs) != self.rev
                starts = range(0, C, SUBLANES)
                z = jnp.concatenate([(qc if is_query(r) else kc)[r:r + SUBLANES] for r in starts], axis=0)
                z = (z * e2[j - 1][:, lanes]).astype(BF16)
                q_starts = [r for r in starts if is_query(r)]
                x = _dot_nt(jnp.concatenate([z[r:r + SUBLANES] for r in q_starts], axis=0), z)
                pieces = []
                for r in starts:
                    if not is_query(r):
                        pieces.append(a[r:r + SUBLANES])
                        continue
                    key_lo = r // (2 * hs) * (2 * hs) + (hs if self.rev else 0)
                    on_keys = (self.lane >= key_lo) & (self.lane < key_lo + hs)
                    i = q_starts.index(r) * SUBLANES
                    pieces.append(jnp.where(on_keys, x[i:i + SUBLANES], a[r:r + SUBLANES]))
                a = jnp.concatenate(pieces, axis=0)
            e_b = e2[N_LEVELS - 1][:, lanes]
            total = e_b[0:1] if self.rev else e_b[C - 1:C]
            q_dec = jnp.concatenate([(qc * e_b).astype(BF16), a.astype(BF16)], axis=1)
            out.append((rows, q_dec, (kc * e2[N_LEVELS][:, lanes]).astype(BF16), total))
        return out

    def outputs(self, chunk_terms):
        for rows, q_dec, k_dec, total in chunk_terms:
            vc = self.v_ref[0, rows, :]
            self.o_ref[0, rows, :] = _dot(q_dec, jnp.concatenate([self.st.astype(BF16), vc], axis=0))
            total_col = jnp.transpose(jnp.broadcast_to(total, (SUBLANES, HG_DK)))[:, 0:1]
            self.st = total_col * self.st + _dot_tn(k_dec, vc)

    def finish(self):
        self.st_ref[...] = self.st


def _hgrn_kernel(qf_ref, vf_ref, gf_ref, qb_ref, vb_ref, gb_ref, s0f_ref, s0b_ref, mask_ref,
                 of_ref, ob_ref, stf, stb):
    @pl.when(pl.program_id(2) == 0)
    def _():
        stf[...] = s0f_ref[0, 0]
        stb[...] = s0b_ref[0, 0]

    fw = _HgDirection(qf_ref, vf_ref, gf_ref, stf, of_ref, mask_ref, False)
    bw = _HgDirection(qb_ref, vb_ref, gb_ref, stb, ob_ref, mask_ref, True)
    units = [(dirn, p) for pf, pb in zip(fw.pairs, bw.pairs) for dirn, p in ((fw, pf), (bw, pb))]
    e_next = units[0][0].decays(units[0][1])
    pending = []
    for i, (dirn, p) in enumerate(units):
        e_cur = e_next
        if i + 1 < len(units):
            e_next = units[i + 1][0].decays(units[i + 1][1])
        pending.append((dirn, dirn.scores(p, e_cur)))
        if len(pending) > HG_OUTPUT_LAG:
            done, terms = pending.pop(0)
            done.outputs(terms)
    for done, terms in pending:
        done.outputs(terms)
    fw.finish()
    bw.finish()


def _hgrn(q_hg, v_hg, g_fw, g_bw, s_fw, s_bw, tb):
    B, T, _ = q_hg.shape
    nb = T // tb
    fwd = pl.BlockSpec((1, tb, HG_DK), lambda b, h, i: (b, i, h))
    bwd = pl.BlockSpec((1, tb, HG_DK), lambda b, h, i: (b, nb - 1 - i, h))
    st = pl.BlockSpec((1, 1, HG_DK, HG_DK), lambda b, h, i: (b, h, 0, 0))
    masks = jnp.asarray(_hg_masks())
    oshape = jax.ShapeDtypeStruct((B, T, HG_WIDTH), F32)
    return pl.pallas_call(
        _hgrn_kernel,
        grid=(B, HG_HEADS, nb),
        in_specs=[fwd, fwd, fwd, bwd, bwd, bwd, st, st,
                  pl.BlockSpec(masks.shape, lambda b, h, i: (0, 0, 0, 0))],
        out_specs=[fwd, bwd],
        out_shape=[oshape, oshape],
        scratch_shapes=[pltpu.VMEM((HG_DK, HG_DK), F32), pltpu.VMEM((HG_DK, HG_DK), F32)],
        compiler_params=pltpu.CompilerParams(
            dimension_semantics=("parallel", "parallel", "arbitrary"), vmem_limit_bytes=VMEM_LIMIT),
        name="hgrn",
    )(q_hg, v_hg, g_fw, q_hg, v_hg, g_bw, s_fw, s_bw, masks)


def _na_bias_kernel(taps_ref, valid_ref, o_ref):
    W = GRID_W
    for i in range(taps_ref.shape[2]):
        taps = jnp.broadcast_to(taps_ref[0, 0, i:i + 1, :], (W, taps_ref.shape[3]))
        shifted = pltpu.roll(taps, 0, axis=1, stride=1, stride_axis=0)
        o_ref[0, 0, i * W:(i + 1) * W, :] = jnp.where(valid_ref[0, i % NA_STEP_ROWS] > 0.0, shifted, NEG)


def _na_plan(rows):
    assert rows % NA_STEP_ROWS == 0 and rows >= NA_WIN_ROWS and (rows - NA_WIN_ROWS) % NA_STEP_ROWS == 0
    variants, starts = [], []
    for i in range(rows // NA_STEP_ROWS):
        w0 = int(np.clip(NA_STEP_ROWS * i - NA_KR // 2, 0, rows - NA_WIN_ROWS))
        geom = tuple((int(np.clip(r - NA_KR // 2, 0, rows - NA_KR)) - w0, r - w0)
                     for r in range(NA_STEP_ROWS * i, NA_STEP_ROWS * (i + 1)))
        if not variants or variants[-1] != geom:
            assert geom not in variants
            variants.append(geom)
            starts.append(i)
    return variants, starts


def _na_bias(rpb, rows):
    W = GRID_W
    n_dr = 2 * NA_KR - 1
    variants, _ = _na_plan(rows)
    dr = np.array([[[j - qrow + NA_KR - 1 if first <= j < first + NA_KR else n_dr for j in range(NA_WIN_ROWS)]
                    for first, qrow in geom] for geom in variants])
    cols = np.arange(W)
    c0 = np.clip(cols - NA_KC // 2, 0, W - NA_KC)
    col_ok = (cols[None, :] >= c0[:, None]) & (cols[None, :] < c0[:, None] + NA_KC)
    valid = (dr < n_dr)[:, :, None, :, None] & col_ok[None, None, :, None, :]
    valid = valid.reshape(len(variants), NA_STEP_ROWS, W, NA_WIN_ROWS * W).astype(np.float32)

    rpb_ext = jnp.zeros((NA_HEADS, n_dr + 1, 2 * NA_KC - 1), F32).at[:, :n_dr].set(rpb.astype(F32))
    picked = jnp.take(rpb_ext, jnp.asarray(dr.reshape(-1), dtype=jnp.int32), axis=1)
    picked = picked.reshape((NA_HEADS,) + dr.shape + (2 * NA_KC - 1,))
    ahead = picked[..., NA_KC - 1:]
    behind = jnp.roll(picked, -1, axis=3)[..., :NA_KC - 1]
    gap = jnp.zeros(picked.shape[:-1] + (W - 2 * NA_KC + 1,), F32)
    taps = jnp.concatenate([ahead, gap, behind], axis=-1)
    taps = taps.reshape(NA_HEADS // NA_GROUP, NA_GROUP, len(variants), NA_STEP_ROWS, NA_WIN_ROWS * W)
    taps = taps.transpose(2, 0, 1, 3, 4).reshape(len(variants), NA_HEADS // NA_GROUP, NA_GROUP * NA_STEP_ROWS,
                                                 NA_WIN_ROWS * W)
    out_block = (1, 1, NA_GROUP * NA_TOK, NA_WIN_ROWS * W)
    return pl.pallas_call(
        _na_bias_kernel,
        grid=(len(variants), NA_HEADS // NA_GROUP),
        in_specs=[pl.BlockSpec((1, 1) + taps.shape[2:], lambda v, g: (v, g, 0, 0)),
                  pl.BlockSpec((1,) + valid.shape[1:], lambda v, g: (v, 0, 0, 0))],
        out_specs=pl.BlockSpec(out_block, lambda v, g: (v, g, 0, 0)),
        out_shape=jax.ShapeDtypeStruct((len(variants), NA_HEADS // NA_GROUP) + out_block[2:], F32),
        name="na_bias",
    )(taps, jnp.asarray(valid))


def _natten_kernel(q_ref, kt_ref, v_ref, kct_ref, vc_ref, *rest):
    bias_refs, o_ref = rest[:-1], rest[-1]
    W = GRID_W
    rows = v_ref.shape[1] // W
    n_kblk = NA_WIN_ROWS // NA_STEP_ROWS
    rb = lax.broadcasted_iota(jnp.int32, (NA_GROUP * NA_TOK, NA_GW), 0) // NA_TOK
    cb = lax.broadcasted_iota(jnp.int32, (NA_GROUP * NA_TOK, NA_GW), 1) // NA_HEAD_DIM
    diag = rb == cb
    groups = [slice(grp * NA_GW, (grp + 1) * NA_GW) for grp in range(NA_HEADS // NA_GROUP)]

    units = []
    for u, bias_ref in enumerate(bias_refs):
        pair = len(bias_refs) * pl.program_id(1) + u
        w0 = jnp.clip(NA_STEP_ROWS * pair - NA_KR // 2, 0, rows - NA_WIN_ROWS)
        blk0 = w0 // NA_STEP_ROWS
        v_rows = pl.ds(pl.multiple_of(w0 * W, NA_TOK), NA_WIN_ROWS * W)
        for grp, gs in enumerate(groups):
            qg = q_ref[0, u * NA_TOK:(u + 1) * NA_TOK, gs]
            kt = jnp.concatenate([kt_ref[0, blk0 + j, gs, :] for j in range(n_kblk)], axis=1)
            kct = jnp.concatenate([kct_ref[0, j, gs, :] for j in range(kct_ref.shape[1])], axis=1)
            qbd = jnp.where(diag, jnp.concatenate([qg] * NA_GROUP, axis=0), jnp.zeros_like(qg[:1]))
            units.append((u, gs, v_rows, _dot(qbd, kt) + bias_ref[0, grp], _dot(qbd, kct)))
    for u, gs, v_rows, s_win, s_ctx in units:
        vw = v_ref[0, v_rows, gs]
        m = jnp.maximum(jnp.max(s_win, axis=-1, keepdims=True), jnp.max(s_ctx, axis=-1, keepdims=True))
        p_win = jnp.exp(s_win - m)
        p_ctx = jnp.exp(s_ctx - m)
        denom = jnp.sum(p_win, axis=-1, keepdims=True) + jnp.sum(p_ctx, axis=-1, keepdims=True)
        of = (_dot(p_win.astype(BF16), vw) + _dot(p_ctx.astype(BF16), vc_ref[0, :, gs])) / denom
        of = jnp.where(diag, of, 0.0)
        og = of[0:NA_TOK]
        for h in range(1, NA_GROUP):
            og = og + of[h * NA_TOK:(h + 1) * NA_TOK]
        o_ref[0, u * NA_TOK:(u + 1) * NA_TOK, gs] = og.astype(o_ref.dtype)


def _natten(q, kt, v, kct, vc, bias):
    B, T, _ = q.shape
    rows = T // GRID_W
    _, starts = _na_plan(rows)
    n_pairs = rows // NA_STEP_ROWS
    assert n_pairs % NA_STEP_PAIRS == 0
    tok = pl.BlockSpec((1, NA_STEP_PAIRS * NA_TOK, NA_WIDTH), lambda b, i: (b, i, 0))
    whole = lambda a: pl.BlockSpec((1,) + a.shape[1:], lambda b, i: (b,) + (0,) * (a.ndim - 1))

    def bias_spec(u):
        def index_map(b, i):
            pair = NA_STEP_PAIRS * i + u
            return (sum((pair >= s).astype(jnp.int32) for s in starts[1:]), 0, 0, 0)
        return pl.BlockSpec((1,) + bias.shape[1:], index_map)

    return pl.pallas_call(
        _natten_kernel,
        grid=(B, n_pairs // NA_STEP_PAIRS),
        in_specs=[tok, whole(kt), whole(v), whole(kct), whole(vc)] + [bias_spec(u) for u in range(NA_STEP_PAIRS)],
        out_specs=tok,
        out_shape=jax.ShapeDtypeStruct((B, T, NA_WIDTH), BF16),
        compiler_params=pltpu.CompilerParams(
            dimension_semantics=("parallel", "arbitrary"), vmem_limit_bytes=VMEM_LIMIT),
        name="natten",
    )(q, kt, v, kct, vc, *([bias] * NA_STEP_PAIRS))


def _load_weight_bf16(src_hbm, dst_ref, stage_ref, sem, chunks):
    def copy(k):
        return pltpu.make_async_copy(src_hbm.at[chunks[k][0]], stage_ref.at[k % 2], sem.at[k % 2])

    copy(0).start()
    for k, (_, dst) in enumerate(chunks):
        if k + 1 < len(chunks):
            copy(k + 1).start()
        copy(k).wait()
        dst_ref[dst] = stage_ref[k % 2].astype(BF16)


def _post_mix_kernel(yna_ref, of_ref, ob_ref, gate_ref, hgw_ref, wo_hbm, x_ref, gtm_ref, nmix_ref,
                     scf_ref, shf_ref, gtf_ref, npre_ref, npost_ref, w1_hbm, w2_hbm, o_ref,
                     act_ref, wo_ref, w1_ref, w2_ref, stage_cols, stage_rows, sem_cols, sem_rows):
    @pl.when((pl.program_id(0) == 0) & (pl.program_id(1) == 0))
    def _():
        rows = stage_rows.shape[1]
        cols = stage_cols.shape[2]
        by_rows = lambda n: [((pl.ds(r, rows), slice(None)),) * 2 for r in range(0, n, rows)]
        by_cols = lambda n: [((slice(None), pl.ds(c, cols)),) * 2 for c in range(0, n, cols)]
        _load_weight_bf16(wo_hbm, wo_ref, stage_rows, sem_rows, by_rows(wo_ref.shape[0]))
        _load_weight_bf16(w1_hbm, w1_ref, stage_cols, sem_cols, by_cols(w1_ref.shape[1]))
        _load_weight_bf16(w2_hbm, w2_ref, stage_rows, sem_rows, by_rows(w2_ref.shape[0]))

    o = of_ref[0] + ob_ref[0]
    gate = gate_ref[0].astype(F32)
    hgw = hgw_ref[...]
    parts = []
    for h in range(HG_HEADS):
        sl = slice(h * HG_DK, (h + 1) * HG_DK)
        parts.append((_rms(o[:, sl], hgw) * gate[:, sl]).astype(BF16))
    y_hg = jnp.concatenate(parts, axis=-1)
    y = _dot(yna_ref[0], wo_ref[:NA_WIDTH, :]) + _dot(y_hg, wo_ref[NA_WIDTH:, :])
    x1 = x_ref[0] + gtm_ref[0] * _rms(y, nmix_ref[...])
    d_ff = w2_ref.shape[0]
    hb = (_rms(x1, npre_ref[...]) * (1.0 + scf_ref[0]) + shf_ref[0]).astype(BF16)
    for j in range(0, d_ff, FFN_COLS):
        gate_j = _dot(hb, w1_ref[:, j:j + FFN_COLS])
        up_j = _dot(hb, w1_ref[:, d_ff + j:d_ff + j + FFN_COLS])
        act_ref[:, j:j + FFN_COLS] = (_silu(gate_j) * up_j).astype(BF16)
    z = _dot(act_ref[...], w2_ref[...])
    o_ref[0] = x1 + gtf_ref[0] * _rms(z, npost_ref[...])


def _post_mix(y_na, o_fw, o_bw, gate, hgw, wo, x, gt_m, nmix, sc_f, sh_f, gt_f, npre, npost, w1, w2, tm):
    B, T, _ = x.shape
    d_ff = w2.shape[0]
    stage_rows, stage_cols = 256, 512
    assert d_ff % FFN_COLS == 0 and d_ff % stage_rows == 0 and wo.shape[0] % stage_rows == 0
    assert w1.shape[1] % stage_cols == 0 and wo.shape[1] == w2.shape[1]
    hbm = pl.BlockSpec(memory_space=pl.ANY)
    tok = lambda n: pl.BlockSpec((1, tm, n), lambda b, i: (b, i, 0))
    mod = pl.BlockSpec((1, 1, D_MODEL), lambda b, i: (b, 0, 0))
    const = lambda shape: pl.BlockSpec(shape, lambda b, i: (0,) * len(shape), pipeline_mode=pl.Buffered(1))
    return pl.pallas_call(
        _post_mix_kernel,
        grid=(B, T // tm),
        in_specs=[tok(NA_WIDTH), tok(HG_WIDTH), tok(HG_WIDTH), tok(HG_WIDTH), const((1, HG_DK)),
                  hbm, tok(D_MODEL), mod, const((1, D_MODEL)),
                  mod, mod, mod, const((1, D_MODEL)), const((1, D_MODEL)),
                  hbm, hbm],
        out_specs=tok(D_MODEL),
        out_shape=jax.ShapeDtypeStruct(x.shape, F32),
        scratch_shapes=[pltpu.VMEM((tm, d_ff), BF16),
                        pltpu.VMEM(wo.shape, BF16), pltpu.VMEM(w1.shape, BF16), pltpu.VMEM(w2.shape, BF16),
                        pltpu.VMEM((2, w1.shape[0], stage_cols), F32), pltpu.VMEM((2, stage_rows, w2.shape[1]), F32),
                        pltpu.SemaphoreType.DMA((2,)), pltpu.SemaphoreType.DMA((2,))],
        compiler_params=pltpu.CompilerParams(
            dimension_semantics=("arbitrary", "arbitrary"), vmem_limit_bytes=VMEM_LIMIT),
        name="post_mix",
    )(y_na, o_fw, o_bw, gate, hgw, wo, x, gt_m, nmix, sc_f, sh_f, gt_f, npre, npost, w1, w2)


def kernel(x, c, ctx, c_ctx, w_ada, b_ada, norm_mix_pre, norm_mix_post, norm_ffn_pre, norm_ffn_post,
           w_in, na_rpb, hg_lb_logits, hg_norm_w, w_out, w_ffn_in, w_ffn_out):
    B, T, D = x.shape
    assert w_ada.shape[0] == 1, "single-layer stack"
    rows = T // GRID_W

    cv = jnp.zeros((8, D), F32).at[:B].set(c).at[B].set(c_ctx)
    mod = _ada(cv, w_ada[0], b_ada[0][None, :])
    sh_m, sc_m, gt_m, sh_f, sc_f, gt_f = [mod[:, i * D:(i + 1) * D] for i in range(N_MOD)]
    lat = lambda m: m[:B, None, :]
    cx = lambda m: jnp.broadcast_to(m[B][None, None, :], (B, 1, D))

    w_in_bf = w_in[0].astype(BF16)
    lbl = hg_lb_logits.reshape(hg_lb_logits.shape[0], 2 * HG_WIDTH)
    nw_pre = norm_mix_pre[0][None, :]

    q_na, kt_na, v_na, q_hg, g_fw, g_bw, v_hg, gate = _in_proj(x, lat(sc_m), lat(sh_m), nw_pre, w_in_bf, lbl, 1024)
    _, kt_c, v_c, _, g_cfw, g_cbw, vhg_c, _ = _in_proj(ctx, cx(sc_m), cx(sh_m), nw_pre, w_in_bf, lbl,
                                                       ctx.shape[1])

    s_fw, s_bw = _ctx_state(g_cfw, g_cbw, vhg_c)
    o_fw, o_bw = _hgrn(q_hg, v_hg, g_fw, g_bw, s_fw, s_bw, 2048)

    y_na = _natten(q_na, kt_na, v_na, kt_c, v_c, _na_bias(na_rpb[0], rows))

    return _post_mix(y_na, o_fw, o_bw, gate, hg_norm_w[0][None, :], w_out[0], x, lat(gt_m),
                     norm_mix_post[0][None, :], lat(sc_f), lat(sh_f), lat(gt_f), norm_ffn_pre[0][None, :],
                     norm_ffn_post[0][None, :], w_ffn_in[0], w_ffn_out[0], 512)
```

```python
import functools

import jax
import jax.numpy as jnp
import numpy as np
from jax import lax
from jax.experimental import pallas as pl
from jax.experimental.pallas import tpu as pltpu

D_MODEL = 1024
GRID_W = 64
NA_HEADS = 8
NA_HEAD_DIM = 64
NA_WIDTH = NA_HEADS * NA_HEAD_DIM
NA_KR = 8
NA_KC = 16
HG_HEADS = 4
HG_DK = 128
HG_WIDTH = HG_HEADS * HG_DK
HG_CHUNK = 64
N_LEVELS = 6
N_MOD = 6
EPS = 1e-6
NEG = -1e30
LOG2E = 1.4426950408889634

NA_GROUP = 4
NA_GW = NA_GROUP * NA_HEAD_DIM
NA_STEP_ROWS = 2
NA_TOK = NA_STEP_ROWS * GRID_W
NA_WIN_ROWS = NA_KR + NA_STEP_ROWS
NA_STEP_PAIRS = 2

F32 = jnp.float32
BF16 = jnp.bfloat16

VMEM_LIMIT = 56 * 1024 * 1024
FFN_COLS = 256


def _silu(x):
    return x * jax.nn.sigmoid(x)


def _dot(a, b):
    return jnp.dot(a, b, preferred_element_type=F32)


def _dot_nt(a, b):
    return lax.dot_general(a, b, (((1,), (1,)), ((), ())), preferred_element_type=F32)


def _dot_tn(a, b):
    return lax.dot_general(a, b, (((0,), (0,)), ((), ())), preferred_element_type=F32)


def _split3(x):
    x1 = x.astype(BF16)
    r1 = x - x1.astype(F32)
    x2 = r1.astype(BF16)
    r2 = r1 - x2.astype(F32)
    return x1, x2, r2.astype(BF16)


def _dot_exact_lhs(t, x):
    x1, x2, x3 = _split3(x)
    return _dot(t, x1) + _dot(t, x2) + _dot(t, x3)


def _rms(x, w):
    return x * lax.rsqrt(jnp.mean(x * x, axis=-1, keepdims=True) + EPS) * w


def _ada_kernel(cv_ref, w_ref, b_ref, o_ref):
    s = _silu(cv_ref[...])
    s1, s2, s3 = _split3(s)
    w1, w2, w3 = _split3(w_ref[...])
    acc = _dot(s1, w1) + (_dot(s1, w2) + _dot(s2, w1)) + (_dot(s1, w3) + _dot(s2, w2) + _dot(s3, w1))
    o_ref[...] = acc + b_ref[...]


def _ada(cv, w_ada, b_ada):
    n = w_ada.shape[1]
    tn = 1536
    return pl.pallas_call(
        _ada_kernel,
        grid=(n // tn,),
        in_specs=[
            pl.BlockSpec((8, D_MODEL), lambda j: (0, 0)),
            pl.BlockSpec((D_MODEL, tn), lambda j: (0, j)),
            pl.BlockSpec((1, tn), lambda j: (0, j)),
        ],
        out_specs=pl.BlockSpec((8, tn), lambda j: (0, j)),
        out_shape=jax.ShapeDtypeStruct((8, n), F32),
        compiler_params=pltpu.CompilerParams(vmem_limit_bytes=VMEM_LIMIT),
        name="ada",
    )(cv, w_ada, b_ada)


def _in_proj_kernel(x_ref, sc_ref, sh_ref, nw_ref, w_ref, lbl_ref,
                    qna_ref, knat_ref, vna_ref, qhg_ref, gfw_ref, gbw_ref, vhg_ref, gate_ref):
    x = x_ref[0]
    h = _rms(x, nw_ref[...]) * (1.0 + sc_ref[0]) + sh_ref[0]
    hb = h.astype(BF16)

    def proj(i):
        return _dot(hb, w_ref[:, i * 512:(i + 1) * 512])

    lbl = lbl_ref[...]
    e = jnp.exp(lbl - jnp.max(lbl, axis=0, keepdims=True))
    lb = e[0:1] / jnp.sum(e, axis=0, keepdims=True)

    qna_ref[0] = (proj(0) * (NA_HEAD_DIM ** -0.5)).astype(BF16)
    k_t = proj(1).astype(BF16).T
    for j in range(knat_ref.shape[1]):
        knat_ref[0, j] = k_t[:, j * NA_TOK:(j + 1) * NA_TOK]
    vna_ref[0] = proj(2).astype(BF16)
    qhg_ref[0] = _silu(proj(3)).astype(BF16)
    lb_f = lb[:, :HG_WIDTH]
    lb_b = lb[:, HG_WIDTH:]
    gfw_ref[0] = jnp.log(lb_f + (1.0 - lb_f) * jax.nn.sigmoid(proj(4)))
    gbw_ref[0] = jnp.log(lb_b + (1.0 - lb_b) * jax.nn.sigmoid(proj(5)))
    vhg_ref[0] = proj(6).astype(BF16)
    gate_ref[0] = _silu(proj(7)).astype(BF16)


def _in_proj(x, sc, sh, nw, w_bf, lbl, tm):
    B, T, _ = x.shape
    tok = lambda b, i: (b, i, 0)
    out_bf = jax.ShapeDtypeStruct((B, T, 512), BF16)
    out_f = jax.ShapeDtypeStruct((B, T, 512), F32)
    ospec = pl.BlockSpec((1, tm, 512), tok)
    return pl.pallas_call(
        _in_proj_kernel,
        grid=(B, T // tm),
        in_specs=[
            pl.BlockSpec((1, tm, D_MODEL), tok),
            pl.BlockSpec((1, 1, D_MODEL), lambda b, i: (b, 0, 0)),
            pl.BlockSpec((1, 1, D_MODEL), lambda b, i: (b, 0, 0)),
            pl.BlockSpec((1, D_MODEL), lambda b, i: (0, 0)),
            pl.BlockSpec(w_bf.shape, lambda b, i: (0, 0)),
            pl.BlockSpec(lbl.shape, lambda b, i: (0, 0)),
        ],
        out_specs=[ospec, pl.BlockSpec((1, tm // NA_TOK, NA_WIDTH, NA_TOK), lambda b, i: (b, i, 0, 0))]
        + [ospec] * 6,
        out_shape=[out_bf, jax.ShapeDtypeStruct((B, T // NA_TOK, NA_WIDTH, NA_TOK), BF16),
                   out_bf, out_bf, out_f, out_f, out_bf, out_bf],
        compiler_params=pltpu.CompilerParams(
            dimension_semantics=("parallel", "parallel"), vmem_limit_bytes=VMEM_LIMIT),
        name="in_proj",
    )(x, sc, sh, nw, w_bf, lbl)


def _ctx_state_kernel(gfw_ref, gbw_ref, v_ref, sfw_ref, sbw_ref):
    L = gfw_ref.shape[1]
    r = lax.broadcasted_iota(jnp.int32, (L, L), 0)
    c = lax.broadcasted_iota(jnp.int32, (L, L), 1)
    upper = jnp.where(c > r, 1.0, 0.0).astype(BF16)
    lower = jnp.where(c < r, 1.0, 0.0).astype(BF16)
    v = v_ref[0]
    for g_ref, tri, s_ref in ((gfw_ref, upper, sfw_ref), (gbw_ref, lower, sbw_ref)):
        g = g_ref[0]
        kw = ((1.0 - jnp.exp(g)) * jnp.exp(_dot_exact_lhs(tri, g))).astype(BF16)
        for h in range(HG_HEADS):
            sl = slice(h * HG_DK, (h + 1) * HG_DK)
            s_ref[0, h] = _dot_tn(kw[:, sl], v[:, sl])


def _ctx_state(g_cfw, g_cbw, v_c):
    B, L, _ = g_cfw.shape
    tok = pl.BlockSpec((1, L, HG_WIDTH), lambda b: (b, 0, 0))
    st = pl.BlockSpec((1, HG_HEADS, HG_DK, HG_DK), lambda b: (b, 0, 0, 0))
    sshape = jax.ShapeDtypeStruct((B, HG_HEADS, HG_DK, HG_DK), F32)
    return pl.pallas_call(
        _ctx_state_kernel,
        grid=(B,),
        in_specs=[tok, tok, tok],
        out_specs=[st, st],
        out_shape=[sshape, sshape],
        compiler_params=pltpu.CompilerParams(vmem_limit_bytes=VMEM_LIMIT),
        name="ctx_state",
    )(g_cfw, g_cbw, v_c)


SUBLANES = 8
HG_OUTPUT_LAG = 2


def _hg_masks():
    C = HG_CHUNK
    t = np.arange(C)[:, None]
    s = np.arange(C)[None, :]
    masks = []
    for j in range(N_LEVELS):
        hs = 1 << j
        same = (t // (2 * hs)) == (s // (2 * hs))
        masks.append(same & ((t % (2 * hs)) >= hs) & ((s % (2 * hs)) < hs))
    masks.append(t == s)
    masks = np.stack(masks).astype(np.float32)
    return np.stack([masks, masks[:, ::-1, ::-1]])


class _HgDirection:
    def __init__(self, q_ref, v_ref, g_ref, st_ref, o_ref, mask_ref, rev):
        C = HG_CHUNK
        self.q_ref, self.v_ref, self.st_ref, self.o_ref, self.rev = q_ref, v_ref, st_ref, o_ref, rev
        d = 1 if rev else 0
        self.g = g = g_ref[0] * LOG2E
        pos = lax.broadcasted_iota(jnp.int32, (C, HG_DK), 0) % SUBLANES
        self.scan_keep = {s: jnp.where(pos < SUBLANES - s if rev else pos >= s, 1.0, 0.0)
                          for s in (1 << i for i in range(SUBLANES.bit_length() - 1))}
        self.f = jnp.exp2(g)
        self.k = 1.0 - self.f
        self.level_mask = [mask_ref[d, j] for j in range(N_LEVELS + 1)]
        self.lane = lax.broadcasted_iota(jnp.int32, (SUBLANES, C), 1)
        t = lax.broadcasted_iota(jnp.int32, (C, HG_DK), 0)
        self.query_rows = {j: ((t % (2 << j)) < (1 << j)) if rev else ((t % (2 << j)) >= (1 << j))
                           for j in range(1, N_LEVELS) if (1 << j) < SUBLANES}
        n_pairs = q_ref.shape[1] // (2 * C)
        self.pairs = list(range(n_pairs - 1, -1, -1) if rev else range(n_pairs))
        self.st = st_ref[...]

    def decays(self, p):
        C, S, L = HG_CHUNK, SUBLANES, 2 * HG_DK
        rev = self.rev
        pos_s = lax.broadcasted_iota(jnp.int32, (S, L), 0)
        row = lambda a, r: jnp.broadcast_to(a[r:r + 1, :], (S, L))
        group = lambda a, i: a[i * S:(i + 1) * S]

        halves = []
        for ch in (2 * p, 2 * p + 1):
            c = self.g[ch * C:(ch + 1) * C]
            for s, keep in self.scan_keep.items():
                c = c + keep * pltpu.roll(c, C - s if rev else s, axis=0)
            halves.append(c)
        c = jnp.concatenate(halves, axis=1)

        def split_in_group(hs):
            blk = 2 * hs
            query = (pos_s % blk) < hs if rev else (pos_s % blk) >= hs
            side = jnp.where(query, 1.0, -1.0)
            out = []
            for i in range(C // S):
                mid = row(c, i * S + (hs if rev else hs - 1))
                for a in range(1, S // blk):
                    mid = jnp.where(pos_s >= a * blk, row(c, i * S + a * blk + (hs if rev else hs - 1)), mid)
                out.append((group(c, i) - mid) * side)
            return jnp.concatenate(out, axis=0)

        parts = [split_in_group(1 << j) for j in range(1, N_LEVELS) if 2 << j <= S]

        b_groups, carried, through = [None] * (C // S), None, {}
        for i in (range(C // S - 1, -1, -1) if rev else range(C // S)):
            b_groups[i] = group(c, i) if carried is None else group(c, i) + carried
            total = row(c, i * S + (0 if rev else S - 1))
            carried = total if carried is None else carried + total
            through[i] = carried
        b = jnp.concatenate(b_groups, axis=0)

        def b_row(m):
            assert m % S == (0 if rev else S - 1)
            return through[m // S]

        for j in range(1, N_LEVELS):
            hs = 1 << j
            if 2 * hs <= S:
                continue
            groups = []
            for r in range(0, C, S):
                start = r // (2 * hs) * (2 * hs)
                b_mid = b_row(start + (hs if rev else hs - 1))
                query_side = (r - start >= hs) != rev
                groups.append(b[r:r + S] - b_mid if query_side else b_mid - b[r:r + S])
            parts.append(jnp.concatenate(groups, axis=0))
        parts.append(b)
        parts.append(jnp.concatenate([b_row(0 if rev else C - 1)] * (C // S), axis=0) - b)
        return [jnp.exp2(x) for x in parts]

    def scores(self, p, e2):
        C = HG_CHUNK
        out = []
        for c in ((2 * p + 1, 2 * p) if self.rev else (2 * p, 2 * p + 1)):
            rows = slice(c * C, (c + 1) * C)
            lanes = slice((c - 2 * p) * HG_DK, (c - 2 * p + 1) * HG_DK)
            qc = self.q_ref[0, rows, :].astype(F32)
            kc = self.k[rows]
            k_next = pltpu.roll(kc, C - 1 if self.rev else 1, axis=0)
            a = (jnp.sum(qc * kc, axis=1, keepdims=True) * self.level_mask[N_LEVELS]
                 + jnp.sum(qc * self.f[rows] * k_next, axis=1, keepdims=True) * self.level_mask[0])
            for j in range(1, N_LEVELS):
                hs = 1 << j
                if hs < SUBLANES:
                    z = (jnp.where(self.query_rows[j], qc, kc) * e2[j - 1][:, lanes]).astype(BF16)
                    a = a + _dot_nt(z, z) * self.level_mask[j]
                    continue
                is_query = lambda r: ((r % (2 * hs)) >= hs) != self.rev
                starts = range(0, C, SUBLANES)
                z = jnp.concatenate([(qc if is_query(r) else kc)[r:r + SUBLANES] for r in starts], axis=0)
                z = (z * e2[j - 1][:, lanes]).astype(BF16)
                q_starts = [r for r in starts if is_query(r)]
                x = _dot_nt(jnp.concatenate([z[r:r + SUBLANES] for r in q_starts], axis=0), z)
                pieces = []
                for r in starts:
                    if not is_query(r):
                        pieces.append(a[r:r + SUBLANES])
                        continue
                    key_lo = r // (2 * hs) * (2 * hs) + (hs if self.rev else 0)
                    on_keys = (self.lane >= key_lo) & (self.lane < key_lo + hs)
                    i = q_starts.index(r) * SUBLANES
                    pieces.append(jnp.where(on_keys, x[i:i + SUBLANES], a[r:r + SUBLANES]))
                a = jnp.concatenate(pieces, axis=0)
            e_b = e2[N_LEVELS - 1][:, lanes]
            total = e_b[0:1] if self.rev else e_b[C - 1:C]
            q_dec = jnp.concatenate([(qc * e_b).astype(BF16), a.astype(BF16)], axis=1)
            out.append((rows, q_dec, (kc * e2[N_LEVELS][:, lanes]).astype(BF16), total))
        return out

    def outputs(self, chunk_terms):
        for rows, q_dec, k_dec, total in chunk_terms:
            vc = self.v_ref[0, rows, :]
            self.o_ref[0, rows, :] = _dot(q_dec, jnp.concatenate([self.st.astype(BF16), vc], axis=0))
            total_col = jnp.transpose(jnp.broadcast_to(total, (SUBLANES, HG_DK)))[:, 0:1]
            self.st = total_col * self.st + _dot_tn(k_dec, vc)

    def finish(self):
        self.st_ref[...] = self.st


def _hgrn_kernel(qf_ref, vf_ref, gf_ref, qb_ref, vb_ref, gb_ref, s0f_ref, s0b_ref, mask_ref,
                 of_ref, ob_ref, stf, stb):
    @pl.when(pl.program_id(2) == 0)
    def _():
        stf[...] = s0f_ref[0, 0]
        stb[...] = s0b_ref[0, 0]

    fw = _HgDirection(qf_ref, vf_ref, gf_ref, stf, of_ref, mask_ref, False)
    bw = _HgDirection(qb_ref, vb_ref, gb_ref, stb, ob_ref, mask_ref, True)
    units = [(dirn, p) for pf, pb in zip(fw.pairs, bw.pairs) for dirn, p in ((fw, pf), (bw, pb))]
    e_next = units[0][0].decays(units[0][1])
    pending = []
    for i, (dirn, p) in enumerate(units):
        e_cur = e_next
        if i + 1 < len(units):
            e_next = units[i + 1][0].decays(units[i + 1][1])
        pending.append((dirn, dirn.scores(p, e_cur)))
        if len(pending) > HG_OUTPUT_LAG:
            done, terms = pending.pop(0)
            done.outputs(terms)
    for done, terms in pending:
        done.outputs(terms)
    fw.finish()
    bw.finish()


def _hgrn(q_hg, v_hg, g_fw, g_bw, s_fw, s_bw, tb):
    B, T, _ = q_hg.shape
    nb = T // tb
    fwd = pl.BlockSpec((1, tb, HG_DK), lambda b, h, i: (b, i, h))
    bwd = pl.BlockSpec((1, tb, HG_DK), lambda b, h, i: (b, nb - 1 - i, h))
    st = pl.BlockSpec((1, 1, HG_DK, HG_DK), lambda b, h, i: (b, h, 0, 0))
    masks = jnp.asarray(_hg_masks())
    oshape = jax.ShapeDtypeStruct((B, T, HG_WIDTH), F32)
    return pl.pallas_call(
        _hgrn_kernel,
        grid=(B, HG_HEADS, nb),
        in_specs=[fwd, fwd, fwd, bwd, bwd, bwd, st, st,
                  pl.BlockSpec(masks.shape, lambda b, h, i: (0, 0, 0, 0))],
        out_specs=[fwd, bwd],
        out_shape=[oshape, oshape],
        scratch_shapes=[pltpu.VMEM((HG_DK, HG_DK), F32), pltpu.VMEM((HG_DK, HG_DK), F32)],
        compiler_params=pltpu.CompilerParams(
            dimension_semantics=("parallel", "parallel", "arbitrary"), vmem_limit_bytes=VMEM_LIMIT),
        name="hgrn",
    )(q_hg, v_hg, g_fw, q_hg, v_hg, g_bw, s_fw, s_bw, masks)


def _na_bias_kernel(taps_ref, valid_ref, o_ref):
    W = GRID_W
    for i in range(taps_ref.shape[2]):
        taps = jnp.broadcast_to(taps_ref[0, 0, i:i + 1, :], (W, taps_ref.shape[3]))
        shifted = pltpu.roll(taps, 0, axis=1, stride=1, stride_axis=0)
        o_ref[0, 0, i * W:(i + 1) * W, :] = jnp.where(valid_ref[0, i % NA_STEP_ROWS] > 0.0, shifted, NEG)


def _na_plan(rows):
    assert rows % NA_STEP_ROWS == 0 and rows >= NA_WIN_ROWS and (rows - NA_WIN_ROWS) % NA_STEP_ROWS == 0
    variants, starts = [], []
    for i in range(rows // NA_STEP_ROWS):
        w0 = int(np.clip(NA_STEP_ROWS * i - NA_KR // 2, 0, rows - NA_WIN_ROWS))
        geom = tuple((int(np.clip(r - NA_KR // 2, 0, rows - NA_KR)) - w0, r - w0)
                     for r in range(NA_STEP_ROWS * i, NA_STEP_ROWS * (i + 1)))
        if not variants or variants[-1] != geom:
            assert geom not in variants
            variants.append(geom)
            starts.append(i)
    return variants, starts


def _na_bias(rpb, rows):
    W = GRID_W
    n_dr = 2 * NA_KR - 1
    variants, _ = _na_plan(rows)
    dr = np.array([[[j - qrow + NA_KR - 1 if first <= j < first + NA_KR else n_dr for j in range(NA_WIN_ROWS)]
                    for first, qrow in geom] for geom in variants])
    cols = np.arange(W)
    c0 = np.clip(cols - NA_KC // 2, 0, W - NA_KC)
    col_ok = (cols[None, :] >= c0[:, None]) & (cols[None, :] < c0[:, None] + NA_KC)
    valid = (dr < n_dr)[:, :, None, :, None] & col_ok[None, None, :, None, :]
    valid = valid.reshape(len(variants), NA_STEP_ROWS, W, NA_WIN_ROWS * W).astype(np.float32)

    rpb_ext = jnp.zeros((NA_HEADS, n_dr + 1, 2 * NA_KC - 1), F32).at[:, :n_dr].set(rpb.astype(F32))
    picked = jnp.take(rpb_ext, jnp.asarray(dr.reshape(-1), dtype=jnp.int32), axis=1)
    picked = picked.reshape((NA_HEADS,) + dr.shape + (2 * NA_KC - 1,))
    ahead = picked[..., NA_KC - 1:]
    behind = jnp.roll(picked, -1, axis=3)[..., :NA_KC - 1]
    gap = jnp.zeros(picked.shape[:-1] + (W - 2 * NA_KC + 1,), F32)
    taps = jnp.concatenate([ahead, gap, behind], axis=-1)
    taps = taps.reshape(NA_HEADS // NA_GROUP, NA_GROUP, len(variants), NA_STEP_ROWS, NA_WIN_ROWS * W)
    taps = taps.transpose(2, 0, 1, 3, 4).reshape(len(variants), NA_HEADS // NA_GROUP, NA_GROUP * NA_STEP_ROWS,
                                                 NA_WIN_ROWS * W)
    out_block = (1, 1, NA_GROUP * NA_TOK, NA_WIN_ROWS * W)
    return pl.pallas_call(
        _na_bias_kernel,
        grid=(len(variants), NA_HEADS // NA_GROUP),
        in_specs=[pl.BlockSpec((1, 1) + taps.shape[2:], lambda v, g: (v, g, 0, 0)),
                  pl.BlockSpec((1,) + valid.shape[1:], lambda v, g: (v, 0, 0, 0))],
        out_specs=pl.BlockSpec(out_block, lambda v, g: (v, g, 0, 0)),
        out_shape=jax.ShapeDtypeStruct((len(variants), NA_HEADS // NA_GROUP) + out_block[2:], F32),
        name="na_bias",
    )(taps, jnp.asarray(valid))


def _natten_kernel(q_ref, kt_ref, v_ref, kct_ref, vc_ref, *rest):
    bias_refs, o_ref = rest[:-1], rest[-1]
    W = GRID_W
    rows = v_ref.shape[1] // W
    n_kblk = NA_WIN_ROWS // NA_STEP_ROWS
    rb = lax.broadcasted_iota(jnp.int32, (NA_GROUP * NA_TOK, NA_GW), 0) // NA_TOK
    cb = lax.broadcasted_iota(jnp.int32, (NA_GROUP * NA_TOK, NA_GW), 1) // NA_HEAD_DIM
    diag = rb == cb
    groups = [slice(grp * NA_GW, (grp + 1) * NA_GW) for grp in range(NA_HEADS // NA_GROUP)]

    units = []
    for u, bias_ref in enumerate(bias_refs):
        pair = len(bias_refs) * pl.program_id(1) + u
        w0 = jnp.clip(NA_STEP_ROWS * pair - NA_KR // 2, 0, rows - NA_WIN_ROWS)
        blk0 = w0 // NA_STEP_ROWS
        v_rows = pl.ds(pl.multiple_of(w0 * W, NA_TOK), NA_WIN_ROWS * W)
        for grp, gs in enumerate(groups):
            qg = q_ref[0, u * NA_TOK:(u + 1) * NA_TOK, gs]
            kt = jnp.concatenate([kt_ref[0, blk0 + j, gs, :] for j in range(n_kblk)], axis=1)
            kct = jnp.concatenate([kct_ref[0, j, gs, :] for j in range(kct_ref.shape[1])], axis=1)
            qbd = jnp.where(diag, jnp.concatenate([qg] * NA_GROUP, axis=0), jnp.zeros_like(qg[:1]))
            units.append((u, gs, v_rows, _dot(qbd, kt) + bias_ref[0, grp], _dot(qbd, kct)))
    for u, gs, v_rows, s_win, s_ctx in units:
        vw = v_ref[0, v_rows, gs]
        m = jnp.maximum(jnp.max(s_win, axis=-1, keepdims=True), jnp.max(s_ctx, axis=-1, keepdims=True))
        p_win = jnp.exp(s_win - m)
        p_ctx = jnp.exp(s_ctx - m)
        denom = jnp.sum(p_win, axis=-1, keepdims=True) + jnp.sum(p_ctx, axis=-1, keepdims=True)
        of = (_dot(p_win.astype(BF16), vw) + _dot(p_ctx.astype(BF16), vc_ref[0, :, gs])) / denom
        of = jnp.where(diag, of, 0.0)
        og = of[0:NA_TOK]
        for h in range(1, NA_GROUP):
            og = og + of[h * NA_TOK:(h + 1) * NA_TOK]
        o_ref[0, u * NA_TOK:(u + 1) * NA_TOK, gs] = og.astype(o_ref.dtype)


def _natten(q, kt, v, kct, vc, bias):
    B, T, _ = q.shape
    rows = T // GRID_W
    _, starts = _na_plan(rows)
    n_pairs = rows // NA_STEP_ROWS
    assert n_pairs % NA_STEP_PAIRS == 0
    tok = pl.BlockSpec((1, NA_STEP_PAIRS * NA_TOK, NA_WIDTH), lambda b, i: (b, i, 0))
    whole = lambda a: pl.BlockSpec((1,) + a.shape[1:], lambda b, i: (b,) + (0,) * (a.ndim - 1))

    def bias_spec(u):
        def index_map(b, i):
            pair = NA_STEP_PAIRS * i + u
            return (sum((pair >= s).astype(jnp.int32) for s in starts[1:]), 0, 0, 0)
        return pl.BlockSpec((1,) + bias.shape[1:], index_map)

    return pl.pallas_call(
        _natten_kernel,
        grid=(B, n_pairs // NA_STEP_PAIRS),
        in_specs=[tok, whole(kt), whole(v), whole(kct), whole(vc)] + [bias_spec(u) for u in range(NA_STEP_PAIRS)],
        out_specs=tok,
        out_shape=jax.ShapeDtypeStruct((B, T, NA_WIDTH), BF16),
        compiler_params=pltpu.CompilerParams(
            dimension_semantics=("parallel", "arbitrary"), vmem_limit_bytes=VMEM_LIMIT),
        name="natten",
    )(q, kt, v, kct, vc, *([bias] * NA_STEP_PAIRS))


def _post_mix_kernel(yna_ref, of_ref, ob_ref, gate_ref, hgw_ref, wo_ref, x_ref, gtm_ref, nmix_ref,
                     scf_ref, shf_ref, gtf_ref, npre_ref, npost_ref, w1_ref, w2_ref, o_ref, act_ref):
    o = of_ref[0] + ob_ref[0]
    gate = gate_ref[0].astype(F32)
    hgw = hgw_ref[...]
    parts = []
    for h in range(HG_HEADS):
        sl = slice(h * HG_DK, (h + 1) * HG_DK)
        parts.append((_rms(o[:, sl], hgw) * gate[:, sl]).astype(BF16))
    y_hg = jnp.concatenate(parts, axis=-1)
    y = _dot(yna_ref[0], wo_ref[:NA_WIDTH, :]) + _dot(y_hg, wo_ref[NA_WIDTH:, :])
    x1 = x_ref[0] + gtm_ref[0] * _rms(y, nmix_ref[...])
    d_ff = w2_ref.shape[0]
    hb = (_rms(x1, npre_ref[...]) * (1.0 + scf_ref[0]) + shf_ref[0]).astype(BF16)
    for j in range(0, d_ff, FFN_COLS):
        gate_j = _dot(hb, w1_ref[:, j:j + FFN_COLS])
        up_j = _dot(hb, w1_ref[:, d_ff + j:d_ff + j + FFN_COLS])
        act_ref[:, j:j + FFN_COLS] = (_silu(gate_j) * up_j).astype(BF16)
    z = _dot(act_ref[...], w2_ref[...])
    o_ref[0] = x1 + gtf_ref[0] * _rms(z, npost_ref[...])


def _post_mix(y_na, o_fw, o_bw, gate, hgw, wo_bf, x, gt_m, nmix, sc_f, sh_f, gt_f, npre, npost, w1_bf, w2_bf, tm):
    B, T, _ = x.shape
    d_ff = w2_bf.shape[0]
    assert d_ff % FFN_COLS == 0
    tok = lambda n: pl.BlockSpec((1, tm, n), lambda b, i: (b, i, 0))
    mod = pl.BlockSpec((1, 1, D_MODEL), lambda b, i: (b, 0, 0))
    const = lambda shape: pl.BlockSpec(shape, lambda b, i: (0,) * len(shape), pipeline_mode=pl.Buffered(1))
    return pl.pallas_call(
        _post_mix_kernel,
        grid=(B, T // tm),
        in_specs=[tok(NA_WIDTH), tok(HG_WIDTH), tok(HG_WIDTH), tok(HG_WIDTH), const((1, HG_DK)),
                  const(wo_bf.shape), tok(D_MODEL), mod, const((1, D_MODEL)),
                  mod, mod, mod, const((1, D_MODEL)), const((1, D_MODEL)),
                  const(w1_bf.shape), const(w2_bf.shape)],
        out_specs=tok(D_MODEL),
        out_shape=jax.ShapeDtypeStruct(x.shape, F32),
        scratch_shapes=[pltpu.VMEM((tm, d_ff), BF16)],
        compiler_params=pltpu.CompilerParams(
            dimension_semantics=("parallel", "parallel"), vmem_limit_bytes=VMEM_LIMIT),
        name="post_mix",
    )(y_na, o_fw, o_bw, gate, hgw, wo_bf, x, gt_m, nmix, sc_f, sh_f, gt_f, npre, npost, w1_bf, w2_bf)


def kernel(x, c, ctx, c_ctx, w_ada, b_ada, norm_mix_pre, norm_mix_post, norm_ffn_pre, norm_ffn_post,
           w_in, na_rpb, hg_lb_logits, hg_norm_w, w_out, w_ffn_in, w_ffn_out):
    B, T, D = x.shape
    assert w_ada.shape[0] == 1, "single-layer stack"
    rows = T // GRID_W

    cv = jnp.zeros((8, D), F32).at[:B].set(c).at[B].set(c_ctx)
    mod = _ada(cv, w_ada[0], b_ada[0][None, :])
    sh_m, sc_m, gt_m, sh_f, sc_f, gt_f = [mod[:, i * D:(i + 1) * D] for i in range(N_MOD)]
    lat = lambda m: m[:B, None, :]
    cx = lambda m: jnp.broadcast_to(m[B][None, None, :], (B, 1, D))

    w_in_bf = w_in[0].astype(BF16)
    lbl = hg_lb_logits.reshape(hg_lb_logits.shape[0], 2 * HG_WIDTH)
    nw_pre = norm_mix_pre[0][None, :]

    q_na, kt_na, v_na, q_hg, g_fw, g_bw, v_hg, gate = _in_proj(x, lat(sc_m), lat(sh_m), nw_pre, w_in_bf, lbl, 1024)
    _, kt_c, v_c, _, g_cfw, g_cbw, vhg_c, _ = _in_proj(ctx, cx(sc_m), cx(sh_m), nw_pre, w_in_bf, lbl,
                                                       ctx.shape[1])

    s_fw, s_bw = _ctx_state(g_cfw, g_cbw, vhg_c)
    o_fw, o_bw = _hgrn(q_hg, v_hg, g_fw, g_bw, s_fw, s_bw, 2048)

    y_na = _natten(q_na, kt_na, v_na, kt_c, v_c, _na_bias(na_rpb[0], rows))

    return _post_mix(y_na, o_fw, o_bw, gate, hg_norm_w[0][None, :], w_out[0].astype(BF16), x, lat(gt_m),
                     norm_mix_post[0][None, :], lat(sc_f), lat(sh_f), lat(gt_f), norm_ffn_pre[0][None, :],
                     norm_ffn_post[0][None, :], w_ffn_in[0].astype(BF16), w_ffn_out[0].astype(BF16), 512)
```

```python
import functools

import jax
import jax.numpy as jnp
import numpy as np
from jax import lax
from jax.experimental import pallas as pl
from jax.experimental.pallas import tpu as pltpu

D_MODEL = 1024
GRID_W = 64
NA_HEADS = 8
NA_HEAD_DIM = 64
NA_WIDTH = NA_HEADS * NA_HEAD_DIM
NA_KR = 8
NA_KC = 16
HG_HEADS = 4
HG_DK = 128
HG_WIDTH = HG_HEADS * HG_DK
HG_CHUNK = 64
N_LEVELS = 6
N_MOD = 6
EPS = 1e-6
NEG = -1e30
LOG2E = 1.4426950408889634

NA_GROUP = 4
NA_GW = NA_GROUP * NA_HEAD_DIM
NA_STEP_ROWS = 2
NA_TOK = NA_STEP_ROWS * GRID_W
NA_WIN_ROWS = NA_KR + NA_STEP_ROWS
NA_STEP_PAIRS = 2

F32 = jnp.float32
BF16 = jnp.bfloat16

VMEM_LIMIT = 56 * 1024 * 1024
FFN_COLS = 256


def _sigmoid(x):
    return 0.5 * jnp.tanh(0.5 * x) + 0.5


def _silu(x):
    half = 0.5 * x
    return half * jnp.tanh(half) + half


def _dot(a, b):
    return jnp.dot(a, b, preferred_element_type=F32)


def _dot_nt(a, b):
    return lax.dot_general(a, b, (((1,), (1,)), ((), ())), preferred_element_type=F32)


def _dot_tn(a, b):
    return lax.dot_general(a, b, (((0,), (0,)), ((), ())), preferred_element_type=F32)


def _split3(x):
    x1 = x.astype(BF16)
    r1 = x - x1.astype(F32)
    x2 = r1.astype(BF16)
    r2 = r1 - x2.astype(F32)
    return x1, x2, r2.astype(BF16)


def _dot_exact_lhs(t, x):
    x1, x2, x3 = _split3(x)
    return _dot(t, x1) + _dot(t, x2) + _dot(t, x3)


def _rms(x, w):
    return x * lax.rsqrt(jnp.mean(x * x, axis=-1, keepdims=True) + EPS) * w


def _ada_kernel(cv_ref, w_ref, b_ref, o_ref):
    s = _silu(cv_ref[...])
    s1, s2, s3 = _split3(s)
    w1, w2, w3 = _split3(w_ref[...])
    acc = _dot(s1, w1) + (_dot(s1, w2) + _dot(s2, w1)) + (_dot(s1, w3) + _dot(s2, w2) + _dot(s3, w1))
    o_ref[...] = acc + b_ref[...]


def _ada(cv, w_ada, b_ada):
    n = w_ada.shape[1]
    tn = 1536
    return pl.pallas_call(
        _ada_kernel,
        grid=(n // tn,),
        in_specs=[
            pl.BlockSpec((8, D_MODEL), lambda j: (0, 0)),
            pl.BlockSpec((D_MODEL, tn), lambda j: (0, j)),
            pl.BlockSpec((1, tn), lambda j: (0, j)),
        ],
        out_specs=pl.BlockSpec((8, tn), lambda j: (0, j)),
        out_shape=jax.ShapeDtypeStruct((8, n), F32),
        compiler_params=pltpu.CompilerParams(vmem_limit_bytes=VMEM_LIMIT),
        name="ada",
    )(cv, w_ada, b_ada)


def _in_proj_kernel(x_ref, sc_ref, sh_ref, nw_ref, w_ref, lbl_ref,
                    qna_ref, knat_ref, vna_ref, qhg_ref, gfw_ref, gbw_ref, vhg_ref, gate_ref):
    x = x_ref[0]
    h = _rms(x, nw_ref[...]) * (1.0 + sc_ref[0]) + sh_ref[0]
    hb = h.astype(BF16)

    def proj(i):
        return _dot(hb, w_ref[:, i * 512:(i + 1) * 512])

    lbl = lbl_ref[...]
    e = jnp.exp(lbl - jnp.max(lbl, axis=0, keepdims=True))
    lb = e[0:1] / jnp.sum(e, axis=0, keepdims=True)

    qna_ref[0] = (proj(0) * (NA_HEAD_DIM ** -0.5)).astype(BF16)
    k_t = proj(1).astype(BF16).T
    for j in range(knat_ref.shape[1]):
        knat_ref[0, j] = k_t[:, j * NA_TOK:(j + 1) * NA_TOK]
    vna_ref[0] = proj(2).astype(BF16)
    qhg_ref[0] = _silu(proj(3)).astype(BF16)
    lb_f = lb[:, :HG_WIDTH]
    lb_b = lb[:, HG_WIDTH:]
    gfw_ref[0] = jnp.log(lb_f + (1.0 - lb_f) * _sigmoid(proj(4)))
    gbw_ref[0] = jnp.log(lb_b + (1.0 - lb_b) * _sigmoid(proj(5)))
    vhg_ref[0] = proj(6).astype(BF16)
    gate_ref[0] = _silu(proj(7)).astype(BF16)


def _in_proj(x, sc, sh, nw, w_bf, lbl, tm):
    B, T, _ = x.shape
    tok = lambda b, i: (b, i, 0)
    out_bf = jax.ShapeDtypeStruct((B, T, 512), BF16)
    out_f = jax.ShapeDtypeStruct((B, T, 512), F32)
    ospec = pl.BlockSpec((1, tm, 512), tok)
    return pl.pallas_call(
        _in_proj_kernel,
        grid=(B, T // tm),
        in_specs=[
            pl.BlockSpec((1, tm, D_MODEL), tok),
            pl.BlockSpec((1, 1, D_MODEL), lambda b, i: (b, 0, 0)),
            pl.BlockSpec((1, 1, D_MODEL), lambda b, i: (b, 0, 0)),
            pl.BlockSpec((1, D_MODEL), lambda b, i: (0, 0)),
            pl.BlockSpec(w_bf.shape, lambda b, i: (0, 0)),
            pl.BlockSpec(lbl.shape, lambda b, i: (0, 0)),
        ],
        out_specs=[ospec, pl.BlockSpec((1, tm // NA_TOK, NA_WIDTH, NA_TOK), lambda b, i: (b, i, 0, 0))]
        + [ospec] * 6,
        out_shape=[out_bf, jax.ShapeDtypeStruct((B, T // NA_TOK, NA_WIDTH, NA_TOK), BF16),
                   out_bf, out_bf, out_f, out_f, out_bf, out_bf],
        compiler_params=pltpu.CompilerParams(
            dimension_semantics=("parallel", "parallel"), vmem_limit_bytes=VMEM_LIMIT),
        name="in_proj",
    )(x, sc, sh, nw, w_bf, lbl)


def _ctx_state_kernel(gfw_ref, gbw_ref, v_ref, sfw_ref, sbw_ref):
    L = gfw_ref.shape[1]
    r = lax.broadcasted_iota(jnp.int32, (L, L), 0)
    c = lax.broadcasted_iota(jnp.int32, (L, L), 1)
    upper = jnp.where(c > r, 1.0, 0.0).astype(BF16)
    lower = jnp.where(c < r, 1.0, 0.0).astype(BF16)
    v = v_ref[0]
    for g_ref, tri, s_ref in ((gfw_ref, upper, sfw_ref), (gbw_ref, lower, sbw_ref)):
        g = g_ref[0]
        kw = ((1.0 - jnp.exp(g)) * jnp.exp(_dot_exact_lhs(tri, g))).astype(BF16)
        for h in range(HG_HEADS):
            sl = slice(h * HG_DK, (h + 1) * HG_DK)
            s_ref[0, h] = _dot_tn(kw[:, sl], v[:, sl])


def _ctx_state(g_cfw, g_cbw, v_c):
    B, L, _ = g_cfw.shape
    tok = pl.BlockSpec((1, L, HG_WIDTH), lambda b: (b, 0, 0))
    st = pl.BlockSpec((1, HG_HEADS, HG_DK, HG_DK), lambda b: (b, 0, 0, 0))
    sshape = jax.ShapeDtypeStruct((B, HG_HEADS, HG_DK, HG_DK), F32)
    return pl.pallas_call(
        _ctx_state_kernel,
        grid=(B,),
        in_specs=[tok, tok, tok],
        out_specs=[st, st],
        out_shape=[sshape, sshape],
        compiler_params=pltpu.CompilerParams(vmem_limit_bytes=VMEM_LIMIT),
        name="ctx_state",
    )(g_cfw, g_cbw, v_c)


SUBLANES = 8
HG_OUTPUT_LAG = 2


def _hg_masks():
    C = HG_CHUNK
    t = np.arange(C)[:, None]
    s = np.arange(C)[None, :]
    masks = []
    for j in range(N_LEVELS):
        hs = 1 << j
        same = (t // (2 * hs)) == (s // (2 * hs))
        masks.append(same & ((t % (2 * hs)) >= hs) & ((s % (2 * hs)) < hs))
    masks.append(t == s)
    masks = np.stack(masks).astype(np.float32)
    return np.stack([masks, masks[:, ::-1, ::-1]])


class _HgDirection:
    def __init__(self, q_ref, v_ref, g_ref, st_ref, o_ref, mask_ref, rev):
        C = HG_CHUNK
        self.q_ref, self.v_ref, self.st_ref, self.o_ref, self.rev = q_ref, v_ref, st_ref, o_ref, rev
        d = 1 if rev else 0
        self.g = g = g_ref[0] * LOG2E
        pos = lax.broadcasted_iota(jnp.int32, (C, HG_DK), 0) % SUBLANES
        self.scan_keep = {s: jnp.where(pos < SUBLANES - s if rev else pos >= s, 1.0, 0.0)
                          for s in (1 << i for i in range(SUBLANES.bit_length() - 1))}
        self.f = jnp.exp2(g)
        self.k = 1.0 - self.f
        self.level_mask = [mask_ref[d, j] for j in range(N_LEVELS + 1)]
        self.lane = lax.broadcasted_iota(jnp.int32, (SUBLANES, C), 1)
        t = lax.broadcasted_iota(jnp.int32, (C, HG_DK), 0)
        self.query_rows = {j: ((t % (2 << j)) < (1 << j)) if rev else ((t % (2 << j)) >= (1 << j))
                           for j in range(1, N_LEVELS) if (1 << j) < SUBLANES}
        n_pairs = q_ref.shape[1] // (2 * C)
        self.pairs = list(range(n_pairs - 1, -1, -1) if rev else range(n_pairs))
        self.st = st_ref[...]

    def decays(self, p):
        C, S, L = HG_CHUNK, SUBLANES, 2 * HG_DK
        rev = self.rev
        pos_s = lax.broadcasted_iota(jnp.int32, (S, L), 0)
        row = lambda a, r: jnp.broadcast_to(a[r:r + 1, :], (S, L))
        group = lambda a, i: a[i * S:(i + 1) * S]

        halves = []
        for ch in (2 * p, 2 * p + 1):
            c = self.g[ch * C:(ch + 1) * C]
            for s, keep in self.scan_keep.items():
                c = c + keep * pltpu.roll(c, C - s if rev else s, axis=0)
            halves.append(c)
        c = jnp.concatenate(halves, axis=1)

        def split_in_group(hs):
            blk = 2 * hs
            query = (pos_s % blk) < hs if rev else (pos_s % blk) >= hs
            side = jnp.where(query, 1.0, -1.0)
            out = []
            for i in range(C // S):
                mid = row(c, i * S + (hs if rev else hs - 1))
                for a in range(1, S // blk):
                    mid = jnp.where(pos_s >= a * blk, row(c, i * S + a * blk + (hs if rev else hs - 1)), mid)
                out.append((group(c, i) - mid) * side)
            return jnp.concatenate(out, axis=0)

        parts = [split_in_group(1 << j) for j in range(1, N_LEVELS) if 2 << j <= S]

        b_groups, carried, through = [None] * (C // S), None, {}
        for i in (range(C // S - 1, -1, -1) if rev else range(C // S)):
            b_groups[i] = group(c, i) if carried is None else group(c, i) + carried
            total = row(c, i * S + (0 if rev else S - 1))
            carried = total if carried is None else carried + total
            through[i] = carried
        b = jnp.concatenate(b_groups, axis=0)

        def b_row(m):
            assert m % S == (0 if rev else S - 1)
            return through[m // S]

        for j in range(1, N_LEVELS):
            hs = 1 << j
            if 2 * hs <= S:
                continue
            groups = []
            for r in range(0, C, S):
                start = r // (2 * hs) * (2 * hs)
                b_mid = b_row(start + (hs if rev else hs - 1))
                query_side = (r - start >= hs) != rev
                groups.append(b[r:r + S] - b_mid if query_side else b_mid - b[r:r + S])
            parts.append(jnp.concatenate(groups, axis=0))
        parts.append(b)
        parts.append(jnp.concatenate([b_row(0 if rev else C - 1)] * (C // S), axis=0) - b)
        return [jnp.exp2(x) for x in parts]

    def scores(self, p, e2):
        C = HG_CHUNK
        out = []
        for c in ((2 * p + 1, 2 * p) if self.rev else (2 * p, 2 * p + 1)):
            rows = slice(c * C, (c + 1) * C)
            lanes = slice((c - 2 * p) * HG_DK, (c - 2 * p + 1) * HG_DK)
            qc = self.q_ref[0, rows, :].astype(F32)
            kc = self.k[rows]
            k_next = pltpu.roll(kc, C - 1 if self.rev else 1, axis=0)
            a = (jnp.sum(qc * kc, axis=1, keepdims=True) * self.level_mask[N_LEVELS]
                 + jnp.sum(qc * self.f[rows] * k_next, axis=1, keepdims=True) * self.level_mask[0])
            for j in range(1, N_LEVELS):
                hs = 1 << j
                if hs < SUBLANES:
                    z = (jnp.where(self.query_rows[j], qc, kc) * e2[j - 1][:, lanes]).astype(BF16)
                    a = a + _dot_nt(z, z) * self.level_mask[j]
                    continue
                is_query = lambda r: ((r % (2 * hs)) >= hs) != self.rev
                starts = range(0, C, SUBLANES)
                z = jnp.concatenate([(qc if is_query(r) else kc)[r:r + SUBLANES] for r in starts], axis=0)
                z = (z * e2[j - 1][:, lanes]).astype(BF16)
                q_starts = [r for r in starts if is_query(r)]
                x = _dot_nt(jnp.concatenate([z[r:r + SUBLANES] for r in q_starts], axis=0), z)
                pieces = []
                for r in starts:
                    if not is_query(r):
                        pieces.append(a[r:r + SUBLANES])
                        continue
                    key_lo = r // (2 * hs) * (2 * hs) + (hs if self.rev else 0)
                    on_keys = (self.lane >= key_lo) & (self.lane < key_lo + hs)
                    i = q_starts.index(r) * SUBLANES
                    pieces.append(jnp.where(on_keys, x[i:i + SUBLANES], a[r:r + SUBLANES]))
                a = jnp.concatenate(pieces, axis=0)
            e_b = e2[N_LEVELS - 1][:, lanes]
            total = e_b[0:1] if self.rev else e_b[C - 1:C]
            q_dec = jnp.concatenate([(qc * e_b).astype(BF16), a.astype(BF16)], axis=1)
            out.append((rows, q_dec, (kc * e2[N_LEVELS][:, lanes]).astype(BF16), total))
        return out

    def outputs(self, chunk_terms):
        for rows, q_dec, k_dec, total in chunk_terms:
            vc = self.v_ref[0, rows, :]
            self.o_ref[0, rows, :] = _dot(q_dec, jnp.concatenate([self.st.astype(BF16), vc], axis=0))
            total_col = jnp.transpose(jnp.broadcast_to(total, (SUBLANES, HG_DK)))[:, 0:1]
            self.st = total_col * self.st + _dot_tn(k_dec, vc)

    def finish(self):
        self.st_ref[...] = self.st


def _hgrn_kernel(qf_ref, vf_ref, gf_ref, qb_ref, vb_ref, gb_ref, s0f_ref, s0b_ref, mask_ref,
                 of_ref, ob_ref, stf, stb):
    @pl.when(pl.program_id(2) == 0)
    def _():
        stf[...] = s0f_ref[0, 0]
        stb[...] = s0b_ref[0, 0]

    fw = _HgDirection(qf_ref, vf_ref, gf_ref, stf, of_ref, mask_ref, False)
    bw = _HgDirection(qb_ref, vb_ref, gb_ref, stb, ob_ref, mask_ref, True)
    units = [(dirn, p) for pf, pb in zip(fw.pairs, bw.pairs) for dirn, p in ((fw, pf), (bw, pb))]
    e_next = units[0][0].decays(units[0][1])
    pending = []
    for i, (dirn, p) in enumerate(units):
        e_cur = e_next
        if i + 1 < len(units):
            e_next = units[i + 1][0].decays(units[i + 1][1])
        pending.append((dirn, dirn.scores(p, e_cur)))
        if len(pending) > HG_OUTPUT_LAG:
            done, terms = pending.pop(0)
            done.outputs(terms)
    for done, terms in pending:
        done.outputs(terms)
    fw.finish()
    bw.finish()


def _hgrn(q_hg, v_hg, g_fw, g_bw, s_fw, s_bw, tb):
    B, T, _ = q_hg.shape
    nb = T // tb
    fwd = pl.BlockSpec((1, tb, HG_DK), lambda b, h, i: (b, i, h))
    bwd = pl.BlockSpec((1, tb, HG_DK), lambda b, h, i: (b, nb - 1 - i, h))
    st = pl.BlockSpec((1, 1, HG_DK, HG_DK), lambda b, h, i: (b, h, 0, 0))
    masks = jnp.asarray(_hg_masks())
    oshape = jax.ShapeDtypeStruct((B, T, HG_WIDTH), F32)
    return pl.pallas_call(
        _hgrn_kernel,
        grid=(B, HG_HEADS, nb),
        in_specs=[fwd, fwd, fwd, bwd, bwd, bwd, st, st,
                  pl.BlockSpec(masks.shape, lambda b, h, i: (0, 0, 0, 0))],
        out_specs=[fwd, bwd],
        out_shape=[oshape, oshape],
        scratch_shapes=[pltpu.VMEM((HG_DK, HG_DK), F32), pltpu.VMEM((HG_DK, HG_DK), F32)],
        compiler_params=pltpu.CompilerParams(
            dimension_semantics=("parallel", "parallel", "arbitrary"), vmem_limit_bytes=VMEM_LIMIT),
        name="hgrn",
    )(q_hg, v_hg, g_fw, q_hg, v_hg, g_bw, s_fw, s_bw, masks)


def _na_bias_kernel(taps_ref, valid_ref, o_ref):
    W = GRID_W
    for i in range(taps_ref.shape[2]):
        taps = jnp.broadcast_to(taps_ref[0, 0, i:i + 1, :], (W, taps_ref.shape[3]))
        shifted = pltpu.roll(taps, 0, axis=1, stride=1, stride_axis=0)
        o_ref[0, 0, i * W:(i + 1) * W, :] = jnp.where(valid_ref[0, i % NA_STEP_ROWS] > 0.0, shifted, NEG)


def _na_plan(rows):
    assert rows % NA_STEP_ROWS == 0 and rows >= NA_WIN_ROWS and (rows - NA_WIN_ROWS) % NA_STEP_ROWS == 0
    variants, starts = [], []
    for i in range(rows // NA_STEP_ROWS):
        w0 = int(np.clip(NA_STEP_ROWS * i - NA_KR // 2, 0, rows - NA_WIN_ROWS))
        geom = tuple((int(np.clip(r - NA_KR // 2, 0, rows - NA_KR)) - w0, r - w0)
                     for r in range(NA_STEP_ROWS * i, NA_STEP_ROWS * (i + 1)))
        if not variants or variants[-1] != geom:
            assert geom not in variants
            variants.append(geom)
            starts.append(i)
    return variants, starts


def _na_bias(rpb, rows):
    W = GRID_W
    n_dr = 2 * NA_KR - 1
    variants, _ = _na_plan(rows)
    dr = np.array([[[j - qrow + NA_KR - 1 if first <= j < first + NA_KR else n_dr for j in range(NA_WIN_ROWS)]
                    for first, qrow in geom] for geom in variants])
    cols = np.arange(W)
    c0 = np.clip(cols - NA_KC // 2, 0, W - NA_KC)
    col_ok = (cols[None, :] >= c0[:, None]) & (cols[None, :] < c0[:, None] + NA_KC)
    valid = (dr < n_dr)[:, :, None, :, None] & col_ok[None, None, :, None, :]
    valid = valid.reshape(len(variants), NA_STEP_ROWS, W, NA_WIN_ROWS * W).astype(np.float32)

    rpb_ext = jnp.zeros((NA_HEADS, n_dr + 1, 2 * NA_KC - 1), F32).at[:, :n_dr].set(rpb.astype(F32))
    picked = jnp.take(rpb_ext, jnp.asarray(dr.reshape(-1), dtype=jnp.int32), axis=1)
    picked = picked.reshape((NA_HEADS,) + dr.shape + (2 * NA_KC - 1,))
    ahead = picked[..., NA_KC - 1:]
    behind = jnp.roll(picked, -1, axis=3)[..., :NA_KC - 1]
    gap = jnp.zeros(picked.shape[:-1] + (W - 2 * NA_KC + 1,), F32)
    taps = jnp.concatenate([ahead, gap, behind], axis=-1)
    taps = taps.reshape(NA_HEADS // NA_GROUP, NA_GROUP, len(variants), NA_STEP_ROWS, NA_WIN_ROWS * W)
    taps = taps.transpose(2, 0, 1, 3, 4).reshape(len(variants), NA_HEADS // NA_GROUP, NA_GROUP * NA_STEP_ROWS,
                                                 NA_WIN_ROWS * W)
    out_block = (1, 1, NA_GROUP * NA_TOK, NA_WIN_ROWS * W)
    return pl.pallas_call(
        _na_bias_kernel,
        grid=(len(variants), NA_HEADS // NA_GROUP),
        in_specs=[pl.BlockSpec((1, 1) + taps.shape[2:], lambda v, g: (v, g, 0, 0)),
                  pl.BlockSpec((1,) + valid.shape[1:], lambda v, g: (v, 0, 0, 0))],
        out_specs=pl.BlockSpec(out_block, lambda v, g: (v, g, 0, 0)),
        out_shape=jax.ShapeDtypeStruct((len(variants), NA_HEADS // NA_GROUP) + out_block[2:], F32),
        name="na_bias",
    )(taps, jnp.asarray(valid))


def _natten_kernel(q_ref, kt_ref, v_ref, kct_ref, vc_ref, *rest):
    bias_refs, o_ref = rest[:-1], rest[-1]
    W = GRID_W
    rows = v_ref.shape[1] // W
    n_kblk = NA_WIN_ROWS // NA_STEP_ROWS
    rb = lax.broadcasted_iota(jnp.int32, (NA_GROUP * NA_TOK, NA_GW), 0) // NA_TOK
    cb = lax.broadcasted_iota(jnp.int32, (NA_GROUP * NA_TOK, NA_GW), 1) // NA_HEAD_DIM
    diag = rb == cb
    groups = [slice(grp * NA_GW, (grp + 1) * NA_GW) for grp in range(NA_HEADS // NA_GROUP)]

    units = []
    for u, bias_ref in enumerate(bias_refs):
        pair = len(bias_refs) * pl.program_id(1) + u
        w0 = jnp.clip(NA_STEP_ROWS * pair - NA_KR // 2, 0, rows - NA_WIN_ROWS)
        blk0 = w0 // NA_STEP_ROWS
        v_rows = pl.ds(pl.multiple_of(w0 * W, NA_TOK), NA_WIN_ROWS * W)
        for grp, gs in enumerate(groups):
            qg = q_ref[0, u * NA_TOK:(u + 1) * NA_TOK, gs]
            kt = jnp.concatenate([kt_ref[0, blk0 + j, gs, :] for j in range(n_kblk)], axis=1)
            kct = jnp.concatenate([kct_ref[0, j, gs, :] for j in range(kct_ref.shape[1])], axis=1)
            qbd = jnp.where(diag, jnp.concatenate([qg] * NA_GROUP, axis=0), jnp.zeros_like(qg[:1]))
            units.append((u, gs, v_rows, _dot(qbd, kt) + bias_ref[0, grp], _dot(qbd, kct)))
    for u, gs, v_rows, s_win, s_ctx in units:
        vw = v_ref[0, v_rows, gs]
        m = jnp.maximum(jnp.max(s_win, axis=-1, keepdims=True), jnp.max(s_ctx, axis=-1, keepdims=True))
        p_win = jnp.exp(s_win - m)
        p_ctx = jnp.exp(s_ctx - m)
        denom = jnp.sum(p_win, axis=-1, keepdims=True) + jnp.sum(p_ctx, axis=-1, keepdims=True)
        of = (_dot(p_win.astype(BF16), vw) + _dot(p_ctx.astype(BF16), vc_ref[0, :, gs])) / denom
        of = jnp.where(diag, of, 0.0)
        og = of[0:NA_TOK]
        for h in range(1, NA_GROUP):
            og = og + of[h * NA_TOK:(h + 1) * NA_TOK]
        o_ref[0, u * NA_TOK:(u + 1) * NA_TOK, gs] = og.astype(o_ref.dtype)


def _natten(q, kt, v, kct, vc, bias):
    B, T, _ = q.shape
    rows = T // GRID_W
    _, starts = _na_plan(rows)
    n_pairs = rows // NA_STEP_ROWS
    assert n_pairs % NA_STEP_PAIRS == 0
    tok = pl.BlockSpec((1, NA_STEP_PAIRS * NA_TOK, NA_WIDTH), lambda b, i: (b, i, 0))
    whole = lambda a: pl.BlockSpec((1,) + a.shape[1:], lambda b, i: (b,) + (0,) * (a.ndim - 1))

    def bias_spec(u):
        def index_map(b, i):
            pair = NA_STEP_PAIRS * i + u
            return (sum((pair >= s).astype(jnp.int32) for s in starts[1:]), 0, 0, 0)
        return pl.BlockSpec((1,) + bias.shape[1:], index_map)

    return pl.pallas_call(
        _natten_kernel,
        grid=(B, n_pairs // NA_STEP_PAIRS),
        in_specs=[tok, whole(kt), whole(v), whole(kct), whole(vc)] + [bias_spec(u) for u in range(NA_STEP_PAIRS)],
        out_specs=tok,
        out_shape=jax.ShapeDtypeStruct((B, T, NA_WIDTH), BF16),
        compiler_params=pltpu.CompilerParams(
            dimension_semantics=("parallel", "arbitrary"), vmem_limit_bytes=VMEM_LIMIT),
        name="natten",
    )(q, kt, v, kct, vc, *([bias] * NA_STEP_PAIRS))


def _post_mix_kernel(yna_ref, of_ref, ob_ref, gate_ref, hgw_ref, wo_ref, x_ref, gtm_ref, nmix_ref,
                     scf_ref, shf_ref, gtf_ref, npre_ref, npost_ref, w1_ref, w2_ref, o_ref, act_ref):
    o = of_ref[0] + ob_ref[0]
    gate = gate_ref[0].astype(F32)
    hgw = hgw_ref[...]
    parts = []
    for h in range(HG_HEADS):
        sl = slice(h * HG_DK, (h + 1) * HG_DK)
        parts.append((_rms(o[:, sl], hgw) * gate[:, sl]).astype(BF16))
    y_hg = jnp.concatenate(parts, axis=-1)
    y = _dot(yna_ref[0], wo_ref[:NA_WIDTH, :]) + _dot(y_hg, wo_ref[NA_WIDTH:, :])
    x1 = x_ref[0] + gtm_ref[0] * _rms(y, nmix_ref[...])
    d_ff = w2_ref.shape[0]
    hb = (_rms(x1, npre_ref[...]) * (1.0 + scf_ref[0]) + shf_ref[0]).astype(BF16)
    for j in range(0, d_ff, FFN_COLS):
        gate_j = _dot(hb, w1_ref[:, j:j + FFN_COLS])
        up_j = _dot(hb, w1_ref[:, d_ff + j:d_ff + j + FFN_COLS])
        act_ref[:, j:j + FFN_COLS] = (_silu(gate_j) * up_j).astype(BF16)
    z = _dot(act_ref[...], w2_ref[...])
    o_ref[0] = x1 + gtf_ref[0] * _rms(z, npost_ref[...])


def _post_mix(y_na, o_fw, o_bw, gate, hgw, wo_bf, x, gt_m, nmix, sc_f, sh_f, gt_f, npre, npost, w1_bf, w2_bf, tm):
    B, T, _ = x.shape
    d_ff = w2_bf.shape[0]
    assert d_ff % FFN_COLS == 0
    tok = lambda n: pl.BlockSpec((1, tm, n), lambda b, i: (b, i, 0))
    mod = pl.BlockSpec((1, 1, D_MODEL), lambda b, i: (b, 0, 0))
    const = lambda shape: pl.BlockSpec(shape, lambda b, i: (0,) * len(shape), pipeline_mode=pl.Buffered(1))
    return pl.pallas_call(
        _post_mix_kernel,
        grid=(B, T // tm),
        in_specs=[tok(NA_WIDTH), tok(HG_WIDTH), tok(HG_WIDTH), tok(HG_WIDTH), const((1, HG_DK)),
                  const(wo_bf.shape), tok(D_MODEL), mod, const((1, D_MODEL)),
                  mod, mod, mod, const((1, D_MODEL)), const((1, D_MODEL)),
                  const(w1_bf.shape), const(w2_bf.shape)],
        out_specs=tok(D_MODEL),
        out_shape=jax.ShapeDtypeStruct(x.shape, F32),
        scratch_shapes=[pltpu.VMEM((tm, d_ff), BF16)],
        compiler_params=pltpu.CompilerParams(
            dimension_semantics=("parallel", "parallel"), vmem_limit_bytes=VMEM_LIMIT),
        name="post_mix",
    )(y_na, o_fw, o_bw, gate, hgw, wo_bf, x, gt_m, nmix, sc_f, sh_f, gt_f, npre, npost, w1_bf, w2_bf)


def kernel(x, c, ctx, c_ctx, w_ada, b_ada, norm_mix_pre, norm_mix_post, norm_ffn_pre, norm_ffn_post,
           w_in, na_rpb, hg_lb_logits, hg_norm_w, w_out, w_ffn_in, w_ffn_out):
    B, T, D = x.shape
    assert w_ada.shape[0] == 1, "single-layer stack"
    rows = T // GRID_W

    cv = jnp.zeros((8, D), F32).at[:B].set(c).at[B].set(c_ctx)
    mod = _ada(cv, w_ada[0], b_ada[0][None, :])
    sh_m, sc_m, gt_m, sh_f, sc_f, gt_f = [mod[:, i * D:(i + 1) * D] for i in range(N_MOD)]
    lat = lambda m: m[:B, None, :]
    cx = lambda m: jnp.broadcast_to(m[B][None, None, :], (B, 1, D))

    w_in_bf = w_in[0].astype(BF16)
    lbl = hg_lb_logits.reshape(hg_lb_logits.shape[0], 2 * HG_WIDTH)
    nw_pre = norm_mix_pre[0][None, :]

    q_na, kt_na, v_na, q_hg, g_fw, g_bw, v_hg, gate = _in_proj(x, lat(sc_m), lat(sh_m), nw_pre, w_in_bf, lbl, 1024)
    _, kt_c, v_c, _, g_cfw, g_cbw, vhg_c, _ = _in_proj(ctx, cx(sc_m), cx(sh_m), nw_pre, w_in_bf, lbl,
                                                       ctx.shape[1])

    s_fw, s_bw = _ctx_state(g_cfw, g_cbw, vhg_c)
    o_fw, o_bw = _hgrn(q_hg, v_hg, g_fw, g_bw, s_fw, s_bw, 2048)

    y_na = _natten(q_na, kt_na, v_na, kt_c, v_c, _na_bias(na_rpb[0], rows))

    return _post_mix(y_na, o_fw, o_bw, gate, hg_norm_w[0][None, :], w_out[0].astype(BF16), x, lat(gt_m),
                     norm_mix_post[0][None, :], lat(sc_f), lat(sh_f), lat(gt_f), norm_ffn_pre[0][None, :],
                     norm_ffn_post[0][None, :], w_ffn_in[0].astype(BF16), w_ffn_out[0].astype(BF16), 512)
```

```python
import functools

import jax
import jax.numpy as jnp
import numpy as np
from jax import lax
from jax.experimental import pallas as pl
from jax.experimental.pallas import tpu as pltpu

D_MODEL = 1024
GRID_W = 64
NA_HEADS = 8
NA_HEAD_DIM = 64
NA_WIDTH = NA_HEADS * NA_HEAD_DIM
NA_KR = 8
NA_KC = 16
HG_HEADS = 4
HG_DK = 128
HG_WIDTH = HG_HEADS * HG_DK
HG_CHUNK = 64
N_LEVELS = 6
N_MOD = 6
EPS = 1e-6
NEG = -1e30
LOG2E = 1.4426950408889634

NA_GROUP = 4
NA_GW = NA_GROUP * NA_HEAD_DIM
NA_STEP_ROWS = 2
NA_TOK = NA_STEP_ROWS * GRID_W
NA_WIN_ROWS = NA_KR + NA_STEP_ROWS
NA_STEP_PAIRS = 2

F32 = jnp.float32
BF16 = jnp.bfloat16

VMEM_LIMIT = 56 * 1024 * 1024
FFN_COLS = 256


def _sigmoid(x):
    return 0.5 * jnp.tanh(0.5 * x) + 0.5


def _silu(x):
    half = 0.5 * x
    return half * jnp.tanh(half) + half


def _dot(a, b):
    return jnp.dot(a, b, preferred_element_type=F32)


def _dot_nt(a, b):
    return lax.dot_general(a, b, (((1,), (1,)), ((), ())), preferred_element_type=F32)


def _dot_tn(a, b):
    return lax.dot_general(a, b, (((0,), (0,)), ((), ())), preferred_element_type=F32)


def _split3(x):
    x1 = x.astype(BF16)
    r1 = x - x1.astype(F32)
    x2 = r1.astype(BF16)
    r2 = r1 - x2.astype(F32)
    return x1, x2, r2.astype(BF16)


def _dot_exact_lhs(t, x):
    x1, x2, x3 = _split3(x)
    return _dot(t, x1) + _dot(t, x2) + _dot(t, x3)


def _rms(x, w):
    return x * lax.rsqrt(jnp.mean(x * x, axis=-1, keepdims=True) + EPS) * w


def _ada_kernel(cv_ref, w_ref, b_ref, o_ref):
    s = _silu(cv_ref[...])
    s1, s2, s3 = _split3(s)
    w1, w2, w3 = _split3(w_ref[...])
    acc = _dot(s1, w1) + (_dot(s1, w2) + _dot(s2, w1)) + (_dot(s1, w3) + _dot(s2, w2) + _dot(s3, w1))
    o_ref[...] = acc + b_ref[...]


def _ada(cv, w_ada, b_ada):
    n = w_ada.shape[1]
    tn = 1536
    return pl.pallas_call(
        _ada_kernel,
        grid=(n // tn,),
        in_specs=[
            pl.BlockSpec((8, D_MODEL), lambda j: (0, 0)),
            pl.BlockSpec((D_MODEL, tn), lambda j: (0, j)),
            pl.BlockSpec((1, tn), lambda j: (0, j)),
        ],
        out_specs=pl.BlockSpec((8, tn), lambda j: (0, j)),
        out_shape=jax.ShapeDtypeStruct((8, n), F32),
        compiler_params=pltpu.CompilerParams(vmem_limit_bytes=VMEM_LIMIT),
        name="ada",
    )(cv, w_ada, b_ada)


def _in_proj_kernel(x_ref, sc_ref, sh_ref, nw_ref, w_ref, lbl_ref,
                    qna_ref, knat_ref, vna_ref, qhg_ref, gfw_ref, gbw_ref, vhg_ref, gate_ref):
    x = x_ref[0]
    h = _rms(x, nw_ref[...]) * (1.0 + sc_ref[0]) + sh_ref[0]
    hb = h.astype(BF16)

    def proj(i):
        return _dot(hb, w_ref[:, i * 512:(i + 1) * 512])

    lbl = lbl_ref[...]
    e = jnp.exp(lbl - jnp.max(lbl, axis=0, keepdims=True))
    lb = e[0:1] / jnp.sum(e, axis=0, keepdims=True)

    qna_ref[0] = (proj(0) * (NA_HEAD_DIM ** -0.5)).astype(BF16)
    k_t = proj(1).astype(BF16).T
    for j in range(knat_ref.shape[1]):
        knat_ref[0, j] = k_t[:, j * NA_TOK:(j + 1) * NA_TOK]
    vna_ref[0] = proj(2).astype(BF16)
    qhg_ref[0] = _silu(proj(3)).astype(BF16)
    lb_f = lb[:, :HG_WIDTH]
    lb_b = lb[:, HG_WIDTH:]
    gfw_ref[0] = jnp.log(lb_f + (1.0 - lb_f) * _sigmoid(proj(4)))
    gbw_ref[0] = jnp.log(lb_b + (1.0 - lb_b) * _sigmoid(proj(5)))
    vhg_ref[0] = proj(6).astype(BF16)
    gate_ref[0] = _silu(proj(7)).astype(BF16)


def _in_proj(x, sc, sh, nw, w_bf, lbl, tm):
    B, T, _ = x.shape
    tok = lambda b, i: (b, i, 0)
    out_bf = jax.ShapeDtypeStruct((B, T, 512), BF16)
    out_f = jax.ShapeDtypeStruct((B, T, 512), F32)
    ospec = pl.BlockSpec((1, tm, 512), tok)
    return pl.pallas_call(
        _in_proj_kernel,
        grid=(B, T // tm),
        in_specs=[
            pl.BlockSpec((1, tm, D_MODEL), tok),
            pl.BlockSpec((1, 1, D_MODEL), lambda b, i: (b, 0, 0)),
            pl.BlockSpec((1, 1, D_MODEL), lambda b, i: (b, 0, 0)),
            pl.BlockSpec((1, D_MODEL), lambda b, i: (0, 0)),
            pl.BlockSpec(w_bf.shape, lambda b, i: (0, 0)),
            pl.BlockSpec(lbl.shape, lambda b, i: (0, 0)),
        ],
        out_specs=[ospec, pl.BlockSpec((1, tm // NA_TOK, NA_WIDTH, NA_TOK), lambda b, i: (b, i, 0, 0))]
        + [ospec] * 6,
        out_shape=[out_bf, jax.ShapeDtypeStruct((B, T // NA_TOK, NA_WIDTH, NA_TOK), BF16),
                   out_bf, out_bf, out_f, out_f, out_bf, out_bf],
        compiler_params=pltpu.CompilerParams(
            dimension_semantics=("parallel", "parallel"), vmem_limit_bytes=VMEM_LIMIT),
        name="in_proj",
    )(x, sc, sh, nw, w_bf, lbl)


def _ctx_state_kernel(gfw_ref, gbw_ref, v_ref, sfw_ref, sbw_ref):
    L = gfw_ref.shape[1]
    r = lax.broadcasted_iota(jnp.int32, (L, L), 0)
    c = lax.broadcasted_iota(jnp.int32, (L, L), 1)
    upper = jnp.where(c > r, 1.0, 0.0).astype(BF16)
    lower = jnp.where(c < r, 1.0, 0.0).astype(BF16)
    v = v_ref[0]
    for g_ref, tri, s_ref in ((gfw_ref, upper, sfw_ref), (gbw_ref, lower, sbw_ref)):
        g = g_ref[0]
        kw = ((1.0 - jnp.exp(g)) * jnp.exp(_dot_exact_lhs(tri, g))).astype(BF16)
        for h in range(HG_HEADS):
            sl = slice(h * HG_DK, (h + 1) * HG_DK)
            s_ref[0, h] = _dot_tn(kw[:, sl], v[:, sl])


def _ctx_state(g_cfw, g_cbw, v_c):
    B, L, _ = g_cfw.shape
    tok = pl.BlockSpec((1, L, HG_WIDTH), lambda b: (b, 0, 0))
    st = pl.BlockSpec((1, HG_HEADS, HG_DK, HG_DK), lambda b: (b, 0, 0, 0))
    sshape = jax.ShapeDtypeStruct((B, HG_HEADS, HG_DK, HG_DK), F32)
    return pl.pallas_call(
        _ctx_state_kernel,
        grid=(B,),
        in_specs=[tok, tok, tok],
        out_specs=[st, st],
        out_shape=[sshape, sshape],
        compiler_params=pltpu.CompilerParams(vmem_limit_bytes=VMEM_LIMIT),
        name="ctx_state",
    )(g_cfw, g_cbw, v_c)


SUBLANES = 8
HG_OUTPUT_LAG = 1
HG_DECAY_LEAD = 1


def _hg_masks():
    C = HG_CHUNK
    t = np.arange(C)[:, None]
    s = np.arange(C)[None, :]
    masks = []
    for j in range(N_LEVELS):
        hs = 1 << j
        same = (t // (2 * hs)) == (s // (2 * hs))
        masks.append(same & ((t % (2 * hs)) >= hs) & ((s % (2 * hs)) < hs))
    masks.append(t == s)
    masks = np.stack(masks).astype(np.float32)
    return np.stack([masks, masks[:, ::-1, ::-1]])


class _HgDirection:
    def __init__(self, q_ref, v_ref, g_ref, st_ref, o_ref, mask_ref, rev):
        C = HG_CHUNK
        self.q_ref, self.v_ref, self.st_ref, self.o_ref, self.rev = q_ref, v_ref, st_ref, o_ref, rev
        d = 1 if rev else 0
        self.g = g = g_ref[0] * LOG2E
        pos = lax.broadcasted_iota(jnp.int32, (C, HG_DK), 0) % SUBLANES
        self.scan_keep = {s: jnp.where(pos < SUBLANES - s if rev else pos >= s, 1.0, 0.0)
                          for s in (1 << i for i in range(SUBLANES.bit_length() - 1))}
        self.f = jnp.exp2(g)
        self.k = 1.0 - self.f
        self.level_mask = [mask_ref[d, j] for j in range(N_LEVELS + 1)]
        self.lane = lax.broadcasted_iota(jnp.int32, (SUBLANES, C), 1)
        t = lax.broadcasted_iota(jnp.int32, (C, HG_DK), 0)
        self.query_rows = {j: ((t % (2 << j)) < (1 << j)) if rev else ((t % (2 << j)) >= (1 << j))
                           for j in range(1, N_LEVELS) if (1 << j) < SUBLANES}
        n_pairs = q_ref.shape[1] // (2 * C)
        self.pairs = list(range(n_pairs - 1, -1, -1) if rev else range(n_pairs))
        self.st = st_ref[...]

    def decays(self, p):
        C, S, L = HG_CHUNK, SUBLANES, 2 * HG_DK
        rev = self.rev
        pos_s = lax.broadcasted_iota(jnp.int32, (S, L), 0)
        row = lambda a, r: jnp.broadcast_to(a[r:r + 1, :], (S, L))
        group = lambda a, i: a[i * S:(i + 1) * S]

        halves = []
        for ch in (2 * p, 2 * p + 1):
            c = self.g[ch * C:(ch + 1) * C]
            for s, keep in self.scan_keep.items():
                c = c + keep * pltpu.roll(c, C - s if rev else s, axis=0)
            halves.append(c)
        c = jnp.concatenate(halves, axis=1)

        def split_in_group(hs):
            blk = 2 * hs
            query = (pos_s % blk) < hs if rev else (pos_s % blk) >= hs
            side = jnp.where(query, 1.0, -1.0)
            out = []
            for i in range(C // S):
                mid = row(c, i * S + (hs if rev else hs - 1))
                for a in range(1, S // blk):
                    mid = jnp.where(pos_s >= a * blk, row(c, i * S + a * blk + (hs if rev else hs - 1)), mid)
                out.append((group(c, i) - mid) * side)
            return jnp.concatenate(out, axis=0)

        parts = [split_in_group(1 << j) for j in range(1, N_LEVELS) if 2 << j <= S]

        b_groups, carried, through = [None] * (C // S), None, {}
        for i in (range(C // S - 1, -1, -1) if rev else range(C // S)):
            b_groups[i] = group(c, i) if carried is None else group(c, i) + carried
            total = row(c, i * S + (0 if rev else S - 1))
            carried = total if carried is None else carried + total
            through[i] = carried
        b = jnp.concatenate(b_groups, axis=0)

        def b_row(m):
            assert m % S == (0 if rev else S - 1)
            return through[m // S]

        for j in range(1, N_LEVELS):
            hs = 1 << j
            if 2 * hs <= S:
                continue
            groups = []
            for r in range(0, C, S):
                start = r // (2 * hs) * (2 * hs)
                b_mid = b_row(start + (hs if rev else hs - 1))
                query_side = (r - start >= hs) != rev
                groups.append(b[r:r + S] - b_mid if query_side else b_mid - b[r:r + S])
            parts.append(jnp.concatenate(groups, axis=0))
        parts.append(b)
        parts.append(jnp.concatenate([b_row(0 if rev else C - 1)] * (C // S), axis=0) - b)
        return [jnp.exp2(x) for x in parts]

    def scores(self, p, e2):
        C = HG_CHUNK
        out = []
        for c in ((2 * p + 1, 2 * p) if self.rev else (2 * p, 2 * p + 1)):
            rows = slice(c * C, (c + 1) * C)
            lanes = slice((c - 2 * p) * HG_DK, (c - 2 * p + 1) * HG_DK)
            qc = self.q_ref[0, rows, :].astype(F32)
            kc = self.k[rows]
            k_next = pltpu.roll(kc, C - 1 if self.rev else 1, axis=0)
            a = (jnp.sum(qc * kc, axis=1, keepdims=True) * self.level_mask[N_LEVELS]
                 + jnp.sum(qc * self.f[rows] * k_next, axis=1, keepdims=True) * self.level_mask[0])
            for j in range(1, N_LEVELS):
                hs = 1 << j
                if hs < SUBLANES:
                    z = (jnp.where(self.query_rows[j], qc, kc) * e2[j - 1][:, lanes]).astype(BF16)
                    a = a + _dot_nt(z, z) * self.level_mask[j]
                    continue
                is_query = lambda r: ((r % (2 * hs)) >= hs) != self.rev
                starts = range(0, C, SUBLANES)
                z = jnp.concatenate([(qc if is_query(r) else kc)[r:r + SUBLANES] for r in starts], axis=0)
                z = (z * e2[j - 1][:, lanes]).astype(BF16)
                q_starts = [r for r in starts if is_query(r)]
                x = _dot_nt(jnp.concatenate([z[r:r + SUBLANES] for r in q_starts], axis=0), z)
                pieces = []
                for r in starts:
                    if not is_query(r):
                        pieces.append(a[r:r + SUBLANES])
                        continue
                    key_lo = r // (2 * hs) * (2 * hs) + (hs if self.rev else 0)
                    on_keys = (self.lane >= key_lo) & (self.lane < key_lo + hs)
                    i = q_starts.index(r) * SUBLANES
                    pieces.append(jnp.where(on_keys, x[i:i + SUBLANES], a[r:r + SUBLANES]))
                a = jnp.concatenate(pieces, axis=0)
            e_b = e2[N_LEVELS - 1][:, lanes]
            total = e_b[0:1] if self.rev else e_b[C - 1:C]
            q_dec = jnp.concatenate([(qc * e_b).astype(BF16), a.astype(BF16)], axis=1)
            out.append((rows, q_dec, (kc * e2[N_LEVELS][:, lanes]).astype(BF16), total))
        return out

    def outputs(self, chunk_terms):
        for rows, q_dec, k_dec, total in chunk_terms:
            vc = self.v_ref[0, rows, :]
            self.o_ref[0, rows, :] = _dot(q_dec, jnp.concatenate([self.st.astype(BF16), vc], axis=0))
            total_col = jnp.transpose(jnp.broadcast_to(total, (SUBLANES, HG_DK)))[:, 0:1]
            self.st = total_col * self.st + _dot_tn(k_dec, vc)

    def finish(self):
        self.st_ref[...] = self.st


def _hgrn_kernel(qf_ref, vf_ref, gf_ref, qb_ref, vb_ref, gb_ref, s0f_ref, s0b_ref, mask_ref,
                 of_ref, ob_ref, stf, stb):
    @pl.when(pl.program_id(2) == 0)
    def _():
        stf[...] = s0f_ref[0, 0]
        stb[...] = s0b_ref[0, 0]

    fw = _HgDirection(qf_ref, vf_ref, gf_ref, stf, of_ref, mask_ref, False)
    bw = _HgDirection(qb_ref, vb_ref, gb_ref, stb, ob_ref, mask_ref, True)
    units = [(dirn, p) for pf, pb in zip(fw.pairs, bw.pairs) for dirn, p in ((fw, pf), (bw, pb))]
    ready = [dirn.decays(p) for dirn, p in units[:HG_DECAY_LEAD]]
    pending = []
    for i, (dirn, p) in enumerate(units):
        if i + HG_DECAY_LEAD < len(units):
            ahead, p_ahead = units[i + HG_DECAY_LEAD]
            ready.append(ahead.decays(p_ahead))
        pending.append((dirn, dirn.scores(p, ready.pop(0))))
        if len(pending) > HG_OUTPUT_LAG:
            done, terms = pending.pop(0)
            done.outputs(terms)
    for done, terms in pending:
        done.outputs(terms)
    fw.finish()
    bw.finish()


def _hgrn(q_hg, v_hg, g_fw, g_bw, s_fw, s_bw, tb):
    B, T, _ = q_hg.shape
    nb = T // tb
    fwd = pl.BlockSpec((1, tb, HG_DK), lambda b, h, i: (b, i, h))
    bwd = pl.BlockSpec((1, tb, HG_DK), lambda b, h, i: (b, nb - 1 - i, h))
    st = pl.BlockSpec((1, 1, HG_DK, HG_DK), lambda b, h, i: (b, h, 0, 0))
    masks = jnp.asarray(_hg_masks())
    oshape = jax.ShapeDtypeStruct((B, T, HG_WIDTH), F32)
    return pl.pallas_call(
        _hgrn_kernel,
        grid=(B, HG_HEADS, nb),
        in_specs=[fwd, fwd, fwd, bwd, bwd, bwd, st, st,
                  pl.BlockSpec(masks.shape, lambda b, h, i: (0, 0, 0, 0))],
        out_specs=[fwd, bwd],
        out_shape=[oshape, oshape],
        scratch_shapes=[pltpu.VMEM((HG_DK, HG_DK), F32), pltpu.VMEM((HG_DK, HG_DK), F32)],
        compiler_params=pltpu.CompilerParams(
            dimension_semantics=("parallel", "parallel", "arbitrary"), vmem_limit_bytes=VMEM_LIMIT),
        name="hgrn",
    )(q_hg, v_hg, g_fw, q_hg, v_hg, g_bw, s_fw, s_bw, masks)


def _na_bias_kernel(taps_ref, valid_ref, o_ref):
    W = GRID_W
    for i in range(taps_ref.shape[2]):
        taps = jnp.broadcast_to(taps_ref[0, 0, i:i + 1, :], (W, taps_ref.shape[3]))
        shifted = pltpu.roll(taps, 0, axis=1, stride=1, stride_axis=0)
        o_ref[0, 0, i * W:(i + 1) * W, :] = jnp.where(valid_ref[0, i % NA_STEP_ROWS] > 0.0, shifted, NEG)


def _na_plan(rows):
    assert rows % NA_STEP_ROWS == 0 and rows >= NA_WIN_ROWS and (rows - NA_WIN_ROWS) % NA_STEP_ROWS == 0
    variants, starts = [], []
    for i in range(rows // NA_STEP_ROWS):
        w0 = int(np.clip(NA_STEP_ROWS * i - NA_KR // 2, 0, rows - NA_WIN_ROWS))
        geom = tuple((int(np.clip(r - NA_KR // 2, 0, rows - NA_KR)) - w0, r - w0)
                     for r in range(NA_STEP_ROWS * i, NA_STEP_ROWS * (i + 1)))
        if not variants or variants[-1] != geom:
            assert geom not in variants
            variants.append(geom)
            starts.append(i)
    return variants, starts


def _na_bias(rpb, rows):
    W = GRID_W
    n_dr = 2 * NA_KR - 1
    variants, _ = _na_plan(rows)
    dr = np.array([[[j - qrow + NA_KR - 1 if first <= j < first + NA_KR else n_dr for j in range(NA_WIN_ROWS)]
                    for first, qrow in geom] for geom in variants])
    cols = np.arange(W)
    c0 = np.clip(cols - NA_KC // 2, 0, W - NA_KC)
    col_ok = (cols[None, :] >= c0[:, None]) & (cols[None, :] < c0[:, None] + NA_KC)
    valid = (dr < n_dr)[:, :, None, :, None] & col_ok[None, None, :, None, :]
    valid = valid.reshape(len(variants), NA_STEP_ROWS, W, NA_WIN_ROWS * W).astype(np.float32)

    rpb_ext = jnp.zeros((NA_HEADS, n_dr + 1, 2 * NA_KC - 1), F32).at[:, :n_dr].set(rpb.astype(F32))
    picked = jnp.take(rpb_ext, jnp.asarray(dr.reshape(-1), dtype=jnp.int32), axis=1)
    picked = picked.reshape((NA_HEADS,) + dr.shape + (2 * NA_KC - 1,))
    ahead = picked[..., NA_KC - 1:]
    behind = jnp.roll(picked, -1, axis=3)[..., :NA_KC - 1]
    gap = jnp.zeros(picked.shape[:-1] + (W - 2 * NA_KC + 1,), F32)
    taps = jnp.concatenate([ahead, gap, behind], axis=-1)
    taps = taps.reshape(NA_HEADS // NA_GROUP, NA_GROUP, len(variants), NA_STEP_ROWS, NA_WIN_ROWS * W)
    taps = taps.transpose(2, 0, 1, 3, 4).reshape(len(variants), NA_HEADS // NA_GROUP, NA_GROUP * NA_STEP_ROWS,
                                                 NA_WIN_ROWS * W)
    out_block = (1, 1, NA_GROUP * NA_TOK, NA_WIN_ROWS * W)
    return pl.pallas_call(
        _na_bias_kernel,
        grid=(len(variants), NA_HEADS // NA_GROUP),
        in_specs=[pl.BlockSpec((1, 1) + taps.shape[2:], lambda v, g: (v, g, 0, 0)),
                  pl.BlockSpec((1,) + valid.shape[1:], lambda v, g: (v, 0, 0, 0))],
        out_specs=pl.BlockSpec(out_block, lambda v, g: (v, g, 0, 0)),
        out_shape=jax.ShapeDtypeStruct((len(variants), NA_HEADS // NA_GROUP) + out_block[2:], F32),
        name="na_bias",
    )(taps, jnp.asarray(valid))


def _natten_kernel(q_ref, kt_ref, v_ref, kct_ref, vc_ref, *rest):
    bias_refs, o_ref = rest[:-1], rest[-1]
    W = GRID_W
    rows = v_ref.shape[1] // W
    n_kblk = NA_WIN_ROWS // NA_STEP_ROWS
    rb = lax.broadcasted_iota(jnp.int32, (NA_GROUP * NA_TOK, NA_GW), 0) // NA_TOK
    cb = lax.broadcasted_iota(jnp.int32, (NA_GROUP * NA_TOK, NA_GW), 1) // NA_HEAD_DIM
    diag = rb == cb
    groups = [slice(grp * NA_GW, (grp + 1) * NA_GW) for grp in range(NA_HEADS // NA_GROUP)]

    units = []
    for u, bias_ref in enumerate(bias_refs):
        pair = len(bias_refs) * pl.program_id(1) + u
        w0 = jnp.clip(NA_STEP_ROWS * pair - NA_KR // 2, 0, rows - NA_WIN_ROWS)
        blk0 = w0 // NA_STEP_ROWS
        v_rows = pl.ds(pl.multiple_of(w0 * W, NA_TOK), NA_WIN_ROWS * W)
        for grp, gs in enumerate(groups):
            qg = q_ref[0, u * NA_TOK:(u + 1) * NA_TOK, gs]
            kt = jnp.concatenate([kt_ref[0, blk0 + j, gs, :] for j in range(n_kblk)], axis=1)
            kct = jnp.concatenate([kct_ref[0, j, gs, :] for j in range(kct_ref.shape[1])], axis=1)
            qbd = jnp.where(diag, jnp.concatenate([qg] * NA_GROUP, axis=0), jnp.zeros_like(qg[:1]))
            units.append((u, gs, v_rows, _dot(qbd, kt) + bias_ref[0, grp], _dot(qbd, kct)))
    for u, gs, v_rows, s_win, s_ctx in units:
        vw = v_ref[0, v_rows, gs]
        m = jnp.maximum(jnp.max(s_win, axis=-1, keepdims=True), jnp.max(s_ctx, axis=-1, keepdims=True))
        p_win = jnp.exp(s_win - m)
        p_ctx = jnp.exp(s_ctx - m)
        denom = jnp.sum(p_win, axis=-1, keepdims=True) + jnp.sum(p_ctx, axis=-1, keepdims=True)
        of = (_dot(p_win.astype(BF16), vw) + _dot(p_ctx.astype(BF16), vc_ref[0, :, gs])) / denom
        of = jnp.where(diag, of, 0.0)
        og = of[0:NA_TOK]
        for h in range(1, NA_GROUP):
            og = og + of[h * NA_TOK:(h + 1) * NA_TOK]
        o_ref[0, u * NA_TOK:(u + 1) * NA_TOK, gs] = og.astype(o_ref.dtype)


def _natten(q, kt, v, kct, vc, bias):
    B, T, _ = q.shape
    rows = T // GRID_W
    _, starts = _na_plan(rows)
    n_pairs = rows // NA_STEP_ROWS
    assert n_pairs % NA_STEP_PAIRS == 0
    tok = pl.BlockSpec((1, NA_STEP_PAIRS * NA_TOK, NA_WIDTH), lambda b, i: (b, i, 0))
    whole = lambda a: pl.BlockSpec((1,) + a.shape[1:], lambda b, i: (b,) + (0,) * (a.ndim - 1))

    def bias_spec(u):
        def index_map(b, i):
            pair = NA_STEP_PAIRS * i + u
            return (sum((pair >= s).astype(jnp.int32) for s in starts[1:]), 0, 0, 0)
        return pl.BlockSpec((1,) + bias.shape[1:], index_map)

    return pl.pallas_call(
        _natten_kernel,
        grid=(B, n_pairs // NA_STEP_PAIRS),
        in_specs=[tok, whole(kt), whole(v), whole(kct), whole(vc)] + [bias_spec(u) for u in range(NA_STEP_PAIRS)],
        out_specs=tok,
        out_shape=jax.ShapeDtypeStruct((B, T, NA_WIDTH), BF16),
        compiler_params=pltpu.CompilerParams(
            dimension_semantics=("parallel", "arbitrary"), vmem_limit_bytes=VMEM_LIMIT),
        name="natten",
    )(q, kt, v, kct, vc, *([bias] * NA_STEP_PAIRS))


def _post_mix_kernel(yna_ref, of_ref, ob_ref, gate_ref, hgw_ref, wo_ref, x_ref, gtm_ref, nmix_ref,
                     scf_ref, shf_ref, gtf_ref, npre_ref, npost_ref, w1_ref, w2_ref, o_ref, act_ref):
    o = of_ref[0] + ob_ref[0]
    gate = gate_ref[0].astype(F32)
    hgw = hgw_ref[...]
    parts = []
    for h in range(HG_HEADS):
        sl = slice(h * HG_DK, (h + 1) * HG_DK)
        parts.append((_rms(o[:, sl], hgw) * gate[:, sl]).astype(BF16))
    y_hg = jnp.concatenate(parts, axis=-1)
    y = _dot(yna_ref[0], wo_ref[:NA_WIDTH, :]) + _dot(y_hg, wo_ref[NA_WIDTH:, :])
    x1 = x_ref[0] + gtm_ref[0] * _rms(y, nmix_ref[...])
    d_ff = w2_ref.shape[0]
    hb = (_rms(x1, npre_ref[...]) * (1.0 + scf_ref[0]) + shf_ref[0]).astype(BF16)
    for j in range(0, d_ff, FFN_COLS):
        gate_j = _dot(hb, w1_ref[:, j:j + FFN_COLS])
        up_j = _dot(hb, w1_ref[:, d_ff + j:d_ff + j + FFN_COLS])
        act_ref[:, j:j + FFN_COLS] = (_silu(gate_j) * up_j).astype(BF16)
    z = _dot(act_ref[...], w2_ref[...])
    o_ref[0] = x1 + gtf_ref[0] * _rms(z, npost_ref[...])


def _post_mix(y_na, o_fw, o_bw, gate, hgw, wo_bf, x, gt_m, nmix, sc_f, sh_f, gt_f, npre, npost, w1_bf, w2_bf, tm):
    B, T, _ = x.shape
    d_ff = w2_bf.shape[0]
    assert d_ff % FFN_COLS == 0
    tok = lambda n: pl.BlockSpec((1, tm, n), lambda b, i: (b, i, 0))
    mod = pl.BlockSpec((1, 1, D_MODEL), lambda b, i: (b, 0, 0))
    const = lambda shape: pl.BlockSpec(shape, lambda b, i: (0,) * len(shape), pipeline_mode=pl.Buffered(1))
    return pl.pallas_call(
        _post_mix_kernel,
        grid=(B, T // tm),
        in_specs=[tok(NA_WIDTH), tok(HG_WIDTH), tok(HG_WIDTH), tok(HG_WIDTH), const((1, HG_DK)),
                  const(wo_bf.shape), tok(D_MODEL), mod, const((1, D_MODEL)),
                  mod, mod, mod, const((1, D_MODEL)), const((1, D_MODEL)),
                  const(w1_bf.shape), const(w2_bf.shape)],
        out_specs=tok(D_MODEL),
        out_shape=jax.ShapeDtypeStruct(x.shape, F32),
        scratch_shapes=[pltpu.VMEM((tm, d_ff), BF16)],
        compiler_params=pltpu.CompilerParams(
            dimension_semantics=("parallel", "parallel"), vmem_limit_bytes=VMEM_LIMIT),
        name="post_mix",
    )(y_na, o_fw, o_bw, gate, hgw, wo_bf, x, gt_m, nmix, sc_f, sh_f, gt_f, npre, npost, w1_bf, w2_bf)


def kernel(x, c, ctx, c_ctx, w_ada, b_ada, norm_mix_pre, norm_mix_post, norm_ffn_pre, norm_ffn_post,
           w_in, na_rpb, hg_lb_logits, hg_norm_w, w_out, w_ffn_in, w_ffn_out):
    B, T, D = x.shape
    assert w_ada.shape[0] == 1, "single-layer stack"
    rows = T // GRID_W

    cv = jnp.zeros((8, D), F32).at[:B].set(c).at[B].set(c_ctx)
    mod = _ada(cv, w_ada[0], b_ada[0][None, :])
    sh_m, sc_m, gt_m, sh_f, sc_f, gt_f = [mod[:, i * D:(i + 1) * D] for i in range(N_MOD)]
    lat = lambda m: m[:B, None, :]
    cx = lambda m: jnp.broadcast_to(m[B][None, None, :], (B, 1, D))

    w_in_bf = w_in[0].astype(BF16)
    lbl = hg_lb_logits.reshape(hg_lb_logits.shape[0], 2 * HG_WIDTH)
    nw_pre = norm_mix_pre[0][None, :]

    q_na, kt_na, v_na, q_hg, g_fw, g_bw, v_hg, gate = _in_proj(x, lat(sc_m), lat(sh_m), nw_pre, w_in_bf, lbl, 1024)
    _, kt_c, v_c, _, g_cfw, g_cbw, vhg_c, _ = _in_proj(ctx, cx(sc_m), cx(sh_m), nw_pre, w_in_bf, lbl,
                                                       ctx.shape[1])

    s_fw, s_bw = _ctx_state(g_cfw, g_cbw, vhg_c)
    o_fw, o_bw = _hgrn(q_hg, v_hg, g_fw, g_bw, s_fw, s_bw, 2048)

    y_na = _natten(q_na, kt_na, v_na, kt_c, v_c, _na_bias(na_rpb[0], rows))

    return _post_mix(y_na, o_fw, o_bw, gate, hg_norm_w[0][None, :], w_out[0].astype(BF16), x, lat(gt_m),
                     norm_mix_post[0][None, :], lat(sc_f), lat(sh_f), lat(gt_f), norm_ffn_pre[0][None, :],
                     norm_ffn_post[0][None, :], w_ffn_in[0].astype(BF16), w_ffn_out[0].astype(BF16), 512)
```

```python
import jax
import jax.numpy as jnp
import numpy as np
from jax import lax
from jax.experimental import pallas as pl
from jax.experimental.pallas import tpu as pltpu

D_MODEL = 1024
GRID_W = 64
NA_HEADS = 8
NA_HEAD_DIM = 64
NA_WIDTH = NA_HEADS * NA_HEAD_DIM
NA_KR = 8
NA_KC = 16
HG_HEADS = 4
HG_DK = 128
HG_WIDTH = HG_HEADS * HG_DK
HG_CHUNK = 64
N_LEVELS = 6
PROJ_GROUP = NA_WIDTH
assert HG_WIDTH == PROJ_GROUP
N_MOD = 6
EPS = 1e-6
NEG = -1e30
LOG2E = 1.4426950408889634

NA_GROUP = 4
NA_GW = NA_GROUP * NA_HEAD_DIM
NA_STEP_ROWS = 2
NA_TOK = NA_STEP_ROWS * GRID_W
NA_WIN_ROWS = NA_KR + NA_STEP_ROWS
NA_STEP_PAIRS = 2
NA_LOGIT_LEAD = 3

F32 = jnp.float32
BF16 = jnp.bfloat16

VMEM_LIMIT = 56 * 1024 * 1024
IN_PROJ_ROWS = 1024
POST_MIX_ROWS = 512
HG_BLOCK = 4096
ADA_COLS = 1536
FFN_COLS = 256


def _sigmoid(x):
    return 0.5 * jnp.tanh(0.5 * x) + 0.5


def _silu(x):
    half = 0.5 * x
    return half * jnp.tanh(half) + half


def _dot(a, b):
    return jnp.dot(a, b, preferred_element_type=F32)


def _dot_nt(a, b):
    return lax.dot_general(a, b, (((1,), (1,)), ((), ())), preferred_element_type=F32)


def _dot_tn(a, b):
    return lax.dot_general(a, b, (((0,), (0,)), ((), ())), preferred_element_type=F32)


def _split3(x):
    x1 = x.astype(BF16)
    r1 = x - x1.astype(F32)
    x2 = r1.astype(BF16)
    r2 = r1 - x2.astype(F32)
    return x1, x2, r2.astype(BF16)


def _dot_exact_lhs(t, x):
    x1, x2, x3 = _split3(x)
    return _dot(t, x1) + _dot(t, x2) + _dot(t, x3)


def _rms(x, w):
    return x * lax.rsqrt(jnp.mean(x * x, axis=-1, keepdims=True) + EPS) * w


def _ada_kernel(cv_ref, w_ref, b_ref, o_ref):
    s = _silu(cv_ref[...])
    s1, s2, s3 = _split3(s)
    w1, w2, w3 = _split3(w_ref[...])
    acc = _dot(s1, w1) + (_dot(s1, w2) + _dot(s2, w1)) + (_dot(s1, w3) + _dot(s2, w2) + _dot(s3, w1))
    o_ref[...] = acc + b_ref[...]


def _ada(cv, w_ada, b_ada):
    m, n = cv.shape[0], w_ada.shape[1]
    tn = ADA_COLS
    assert n % tn == 0
    return pl.pallas_call(
        _ada_kernel,
        grid=(n // tn,),
        in_specs=[
            pl.BlockSpec((m, D_MODEL), lambda j: (0, 0)),
            pl.BlockSpec((D_MODEL, tn), lambda j: (0, j)),
            pl.BlockSpec((1, tn), lambda j: (0, j)),
        ],
        out_specs=pl.BlockSpec((m, tn), lambda j: (0, j)),
        out_shape=jax.ShapeDtypeStruct((m, n), F32),
        compiler_params=pltpu.CompilerParams(vmem_limit_bytes=VMEM_LIMIT),
        name="ada",
    )(cv, w_ada, b_ada)


def _in_proj_kernel(x_ref, sc_ref, sh_ref, nw_ref, w_ref, lbl_ref,
                    qna_ref, knat_ref, vna_ref, qhg_ref, gfw_ref, gbw_ref, vhg_ref, gate_ref):
    x = x_ref[0]
    h = _rms(x, nw_ref[...]) * (1.0 + sc_ref[0]) + sh_ref[0]
    hb = h.astype(BF16)

    def proj(i):
        return _dot(hb, w_ref[:, i * PROJ_GROUP:(i + 1) * PROJ_GROUP])

    lbl = lbl_ref[...]
    e = jnp.exp(lbl - jnp.max(lbl, axis=0, keepdims=True))
    lb = e[0:1] / jnp.sum(e, axis=0, keepdims=True)

    qna_ref[0] = (proj(0) * (NA_HEAD_DIM ** -0.5)).astype(BF16)
    k_t = proj(1).astype(BF16).T
    for j in range(knat_ref.shape[1]):
        knat_ref[0, j] = k_t[:, j * NA_TOK:(j + 1) * NA_TOK]
    vna_ref[0] = proj(2).astype(BF16)
    qhg_ref[0] = _silu(proj(3)).astype(BF16)
    lb_f = lb[:, :HG_WIDTH]
    lb_b = lb[:, HG_WIDTH:]
    gfw_ref[0] = jnp.log(lb_f + (1.0 - lb_f) * _sigmoid(proj(4)))
    gbw_ref[0] = jnp.log(lb_b + (1.0 - lb_b) * _sigmoid(proj(5)))
    vhg_ref[0] = proj(6).astype(BF16)
    gate_ref[0] = _silu(proj(7)).astype(BF16)


def _in_proj(x, sc, sh, nw, w_bf, lbl, tm):
    B, T, _ = x.shape
    tok = lambda b, i: (b, i, 0)
    assert T % tm == 0 and tm % NA_TOK == 0 and w_bf.shape[1] == 8 * PROJ_GROUP
    out_bf = jax.ShapeDtypeStruct((B, T, PROJ_GROUP), BF16)
    out_f = jax.ShapeDtypeStruct((B, T, PROJ_GROUP), F32)
    ospec = pl.BlockSpec((1, tm, PROJ_GROUP), tok)
    return pl.pallas_call(
        _in_proj_kernel,
        grid=(B, T // tm),
        in_specs=[
            pl.BlockSpec((1, tm, D_MODEL), tok),
            pl.BlockSpec((1, 1, D_MODEL), lambda b, i: (b, 0, 0)),
            pl.BlockSpec((1, 1, D_MODEL), lambda b, i: (b, 0, 0)),
            pl.BlockSpec((1, D_MODEL), lambda b, i: (0, 0)),
            pl.BlockSpec(w_bf.shape, lambda b, i: (0, 0)),
            pl.BlockSpec(lbl.shape, lambda b, i: (0, 0)),
        ],
        out_specs=[ospec, pl.BlockSpec((1, tm // NA_TOK, NA_WIDTH, NA_TOK), lambda b, i: (b, i, 0, 0))]
        + [ospec] * 6,
        out_shape=[out_bf, jax.ShapeDtypeStruct((B, T // NA_TOK, NA_WIDTH, NA_TOK), BF16),
                   out_bf, out_bf, out_f, out_f, out_bf, out_bf],
        compiler_params=pltpu.CompilerParams(
            dimension_semantics=("parallel", "parallel"), vmem_limit_bytes=VMEM_LIMIT),
        name="in_proj",
    )(x, sc, sh, nw, w_bf, lbl)


def _ctx_state_kernel(gfw_ref, gbw_ref, v_ref, sfw_ref, sbw_ref):
    L = gfw_ref.shape[1]
    r = lax.broadcasted_iota(jnp.int32, (L, L), 0)
    c = lax.broadcasted_iota(jnp.int32, (L, L), 1)
    upper = jnp.where(c > r, 1.0, 0.0).astype(BF16)
    lower = jnp.where(c < r, 1.0, 0.0).astype(BF16)
    v = v_ref[0]
    for g_ref, tri, s_ref in ((gfw_ref, upper, sfw_ref), (gbw_ref, lower, sbw_ref)):
        g = g_ref[0]
        kw = ((1.0 - jnp.exp(g)) * jnp.exp(_dot_exact_lhs(tri, g))).astype(BF16)
        for h in range(HG_HEADS):
            sl = slice(h * HG_DK, (h + 1) * HG_DK)
            s_ref[0, h] = _dot_tn(kw[:, sl], v[:, sl])


def _ctx_state(g_cfw, g_cbw, v_c):
    B, L, _ = g_cfw.shape
    tok = pl.BlockSpec((1, L, HG_WIDTH), lambda b: (b, 0, 0))
    st = pl.BlockSpec((1, HG_HEADS, HG_DK, HG_DK), lambda b: (b, 0, 0, 0))
    sshape = jax.ShapeDtypeStruct((B, HG_HEADS, HG_DK, HG_DK), F32)
    return pl.pallas_call(
        _ctx_state_kernel,
        grid=(B,),
        in_specs=[tok, tok, tok],
        out_specs=[st, st],
        out_shape=[sshape, sshape],
        compiler_params=pltpu.CompilerParams(vmem_limit_bytes=VMEM_LIMIT),
        name="ctx_state",
    )(g_cfw, g_cbw, v_c)


SUBLANES = 8
HG_OUTPUT_LAG = 1
HG_DECAY_LEAD = 1


def _hg_masks():
    C = HG_CHUNK
    t = np.arange(C)[:, None]
    s = np.arange(C)[None, :]
    masks = []
    for j in range(N_LEVELS):
        hs = 1 << j
        same = (t // (2 * hs)) == (s // (2 * hs))
        masks.append(same & ((t % (2 * hs)) >= hs) & ((s % (2 * hs)) < hs))
    masks.append(t == s)
    masks = np.stack(masks).astype(np.float32)
    return np.stack([masks, masks[:, ::-1, ::-1]])


class _HgDirection:
    def __init__(self, q_ref, v_ref, g_ref, st_ref, o_ref, mask_ref, rev):
        C = HG_CHUNK
        self.q_ref, self.v_ref, self.st_ref, self.o_ref, self.rev = q_ref, v_ref, st_ref, o_ref, rev
        d = 1 if rev else 0
        self.g = g = g_ref[0] * LOG2E
        pos = lax.broadcasted_iota(jnp.int32, (C, HG_DK), 0) % SUBLANES
        self.scan_keep = {s: jnp.where(pos < SUBLANES - s if rev else pos >= s, 1.0, 0.0)
                          for s in (1 << i for i in range(SUBLANES.bit_length() - 1))}
        self.f = jnp.exp2(g)
        self.k = 1.0 - self.f
        self.level_mask = [mask_ref[d, j] for j in range(N_LEVELS + 1)]
        self.lane = lax.broadcasted_iota(jnp.int32, (SUBLANES, C), 1)
        t = lax.broadcasted_iota(jnp.int32, (C, HG_DK), 0)
        self.query_rows = {j: ((t % (2 << j)) < (1 << j)) if rev else ((t % (2 << j)) >= (1 << j))
                           for j in range(1, N_LEVELS) if (1 << j) < SUBLANES}
        n_pairs = q_ref.shape[1] // (2 * C)
        self.pairs = list(range(n_pairs - 1, -1, -1) if rev else range(n_pairs))
        self.st = st_ref[...]

    def decays(self, p):
        C, S, L = HG_CHUNK, SUBLANES, 2 * HG_DK
        rev = self.rev
        pos_s = lax.broadcasted_iota(jnp.int32, (S, L), 0)
        row = lambda a, r: jnp.broadcast_to(a[r:r + 1, :], (S, L))
        group = lambda a, i: a[i * S:(i + 1) * S]

        halves = []
        for ch in (2 * p, 2 * p + 1):
            c = self.g[ch * C:(ch + 1) * C]
            for s, keep in self.scan_keep.items():
                c = c + keep * pltpu.roll(c, C - s if rev else s, axis=0)
            halves.append(c)
        c = jnp.concatenate(halves, axis=1)

        def split_in_group(hs):
            blk = 2 * hs
            query = (pos_s % blk) < hs if rev else (pos_s % blk) >= hs
            side = jnp.where(query, 1.0, -1.0)
            out = []
            for i in range(C // S):
                mid = row(c, i * S + (hs if rev else hs - 1))
                for a in range(1, S // blk):
                    mid = jnp.where(pos_s >= a * blk, row(c, i * S + a * blk + (hs if rev else hs - 1)), mid)
                out.append((group(c, i) - mid) * side)
            return jnp.concatenate(out, axis=0)

        parts = [split_in_group(1 << j) for j in range(1, N_LEVELS) if 2 << j <= S]

        b_groups, carried, through = [None] * (C // S), None, {}
        for i in (range(C // S - 1, -1, -1) if rev else range(C // S)):
            b_groups[i] = group(c, i) if carried is None else group(c, i) + carried
            total = row(c, i * S + (0 if rev else S - 1))
            carried = total if carried is None else carried + total
            through[i] = carried
        b = jnp.concatenate(b_groups, axis=0)

        def b_row(m):
            assert m % S == (0 if rev else S - 1)
            return through[m // S]

        for j in range(1, N_LEVELS):
            hs = 1 << j
            if 2 * hs <= S:
                continue
            groups = []
            for r in range(0, C, S):
                start = r // (2 * hs) * (2 * hs)
                b_mid = b_row(start + (hs if rev else hs - 1))
                query_side = (r - start >= hs) != rev
                groups.append(b[r:r + S] - b_mid if query_side else b_mid - b[r:r + S])
            parts.append(jnp.concatenate(groups, axis=0))
        parts.append(b)
        parts.append(jnp.concatenate([b_row(0 if rev else C - 1)] * (C // S), axis=0) - b)
        return [jnp.exp2(x) for x in parts]

    def scores(self, p, e2):
        C = HG_CHUNK
        out = []
        for c in ((2 * p + 1, 2 * p) if self.rev else (2 * p, 2 * p + 1)):
            rows = slice(c * C, (c + 1) * C)
            lanes = slice((c - 2 * p) * HG_DK, (c - 2 * p + 1) * HG_DK)
            qc = self.q_ref[0, rows, :].astype(F32)
            kc = self.k[rows]
            k_next = pltpu.roll(kc, C - 1 if self.rev else 1, axis=0)
            a = (jnp.sum(qc * kc, axis=1, keepdims=True) * self.level_mask[N_LEVELS]
                 + jnp.sum(qc * self.f[rows] * k_next, axis=1, keepdims=True) * self.level_mask[0])
            for j in range(1, N_LEVELS):
                hs = 1 << j
                if hs < SUBLANES:
                    z = (jnp.where(self.query_rows[j], qc, kc) * e2[j - 1][:, lanes]).astype(BF16)
                    a = a + _dot_nt(z, z) * self.level_mask[j]
                    continue
                is_query = lambda r: ((r % (2 * hs)) >= hs) != self.rev
                starts = range(0, C, SUBLANES)
                z = jnp.concatenate([(qc if is_query(r) else kc)[r:r + SUBLANES] for r in starts], axis=0)
                z = (z * e2[j - 1][:, lanes]).astype(BF16)
                q_starts = [r for r in starts if is_query(r)]
                x = _dot_nt(jnp.concatenate([z[r:r + SUBLANES] for r in q_starts], axis=0), z)
                pieces = []
                for r in starts:
                    if not is_query(r):
                        pieces.append(a[r:r + SUBLANES])
                        continue
                    key_lo = r // (2 * hs) * (2 * hs) + (hs if self.rev else 0)
                    on_keys = (self.lane >= key_lo) & (self.lane < key_lo + hs)
                    i = q_starts.index(r) * SUBLANES
                    pieces.append(jnp.where(on_keys, x[i:i + SUBLANES], a[r:r + SUBLANES]))
                a = jnp.concatenate(pieces, axis=0)
            e_b = e2[N_LEVELS - 1][:, lanes]
            total = e_b[0:1] if self.rev else e_b[C - 1:C]
            q_dec = jnp.concatenate([(qc * e_b).astype(BF16), a.astype(BF16)], axis=1)
            out.append((rows, q_dec, (kc * e2[N_LEVELS][:, lanes]).astype(BF16), total))
        return out

    def outputs(self, chunk_terms):
        for rows, q_dec, k_dec, total in chunk_terms:
            vc = self.v_ref[0, rows, :]
            self.o_ref[0, rows, :] = _dot(q_dec, jnp.concatenate([self.st.astype(BF16), vc], axis=0))
            total_col = jnp.transpose(jnp.broadcast_to(total, (SUBLANES, HG_DK)))[:, 0:1]
            self.st = total_col * self.st + _dot_tn(k_dec, vc)

    def finish(self):
        self.st_ref[...] = self.st


def _hgrn_kernel(qf_ref, vf_ref, gf_ref, qb_ref, vb_ref, gb_ref, s0f_ref, s0b_ref, mask_ref,
                 of_ref, ob_ref, stf, stb):
    @pl.when(pl.program_id(2) == 0)
    def _():
        stf[...] = s0f_ref[0, 0]
        stb[...] = s0b_ref[0, 0]

    fw = _HgDirection(qf_ref, vf_ref, gf_ref, stf, of_ref, mask_ref, False)
    bw = _HgDirection(qb_ref, vb_ref, gb_ref, stb, ob_ref, mask_ref, True)
    units = [(dirn, p) for pf, pb in zip(fw.pairs, bw.pairs) for dirn, p in ((fw, pf), (bw, pb))]
    ready = [dirn.decays(p) for dirn, p in units[:HG_DECAY_LEAD]]
    pending = []
    for i, (dirn, p) in enumerate(units):
        if i + HG_DECAY_LEAD < len(units):
            ahead, p_ahead = units[i + HG_DECAY_LEAD]
            ready.append(ahead.decays(p_ahead))
        pending.append((dirn, dirn.scores(p, ready.pop(0))))
        if len(pending) > HG_OUTPUT_LAG:
            done, terms = pending.pop(0)
            done.outputs(terms)
    for done, terms in pending:
        done.outputs(terms)
    fw.finish()
    bw.finish()


def _hgrn(q_hg, v_hg, g_fw, g_bw, s_fw, s_bw, tb):
    B, T, _ = q_hg.shape
    nb = T // tb
    fwd = pl.BlockSpec((1, tb, HG_DK), lambda b, h, i: (b, i, h))
    bwd = pl.BlockSpec((1, tb, HG_DK), lambda b, h, i: (b, nb - 1 - i, h))
    st = pl.BlockSpec((1, 1, HG_DK, HG_DK), lambda b, h, i: (b, h, 0, 0))
    masks = jnp.asarray(_hg_masks())
    oshape = jax.ShapeDtypeStruct((B, T, HG_WIDTH), F32)
    return pl.pallas_call(
        _hgrn_kernel,
        grid=(B, HG_HEADS, nb),
        in_specs=[fwd, fwd, fwd, bwd, bwd, bwd, st, st,
                  pl.BlockSpec(masks.shape, lambda b, h, i: (0, 0, 0, 0))],
        out_specs=[fwd, bwd],
        out_shape=[oshape, oshape],
        scratch_shapes=[pltpu.VMEM((HG_DK, HG_DK), F32), pltpu.VMEM((HG_DK, HG_DK), F32)],
        compiler_params=pltpu.CompilerParams(
            dimension_semantics=("parallel", "parallel", "arbitrary"), vmem_limit_bytes=VMEM_LIMIT),
        name="hgrn",
    )(q_hg, v_hg, g_fw, q_hg, v_hg, g_bw, s_fw, s_bw, masks)


def _na_bias_kernel(taps_ref, valid_ref, o_ref):
    W = GRID_W
    for i in range(taps_ref.shape[2]):
        taps = jnp.broadcast_to(taps_ref[0, 0, i:i + 1, :], (W, taps_ref.shape[3]))
        shifted = pltpu.roll(taps, 0, axis=1, stride=1, stride_axis=0)
        o_ref[0, 0, i * W:(i + 1) * W, :] = jnp.where(valid_ref[0, i % NA_STEP_ROWS] > 0.0, shifted, NEG)


def _na_plan(rows):
    assert rows % NA_STEP_ROWS == 0 and rows >= NA_WIN_ROWS and (rows - NA_WIN_ROWS) % NA_STEP_ROWS == 0
    variants, starts = [], []
    for i in range(rows // NA_STEP_ROWS):
        w0 = int(np.clip(NA_STEP_ROWS * i - NA_KR // 2, 0, rows - NA_WIN_ROWS))
        geom = tuple((int(np.clip(r - NA_KR // 2, 0, rows - NA_KR)) - w0, r - w0)
                     for r in range(NA_STEP_ROWS * i, NA_STEP_ROWS * (i + 1)))
        if not variants or variants[-1] != geom:
            assert geom not in variants
            variants.append(geom)
            starts.append(i)
    return variants, starts


def _na_bias(rpb, rows):
    W = GRID_W
    n_dr = 2 * NA_KR - 1
    variants, _ = _na_plan(rows)
    dr = np.array([[[j - qrow + NA_KR - 1 if first <= j < first + NA_KR else n_dr for j in range(NA_WIN_ROWS)]
                    for first, qrow in geom] for geom in variants])
    cols = np.arange(W)
    c0 = np.clip(cols - NA_KC // 2, 0, W - NA_KC)
    col_ok = (cols[None, :] >= c0[:, None]) & (cols[None, :] < c0[:, None] + NA_KC)
    valid = (dr < n_dr)[:, :, None, :, None] & col_ok[None, None, :, None, :]
    valid = valid.reshape(len(variants), NA_STEP_ROWS, W, NA_WIN_ROWS * W).astype(np.float32)

    rpb_ext = jnp.zeros((NA_HEADS, n_dr + 1, 2 * NA_KC - 1), F32).at[:, :n_dr].set(rpb.astype(F32))
    picked = jnp.take(rpb_ext, jnp.asarray(dr.reshape(-1), dtype=jnp.int32), axis=1)
    picked = picked.reshape((NA_HEADS,) + dr.shape + (2 * NA_KC - 1,))
    ahead = picked[..., NA_KC - 1:]
    behind = jnp.roll(picked, -1, axis=3)[..., :NA_KC - 1]
    gap = jnp.zeros(picked.shape[:-1] + (W - 2 * NA_KC + 1,), F32)
    taps = jnp.concatenate([ahead, gap, behind], axis=-1)
    taps = taps.reshape(NA_HEADS // NA_GROUP, NA_GROUP, len(variants), NA_STEP_ROWS, NA_WIN_ROWS * W)
    taps = taps.transpose(2, 0, 1, 3, 4).reshape(len(variants), NA_HEADS // NA_GROUP, NA_GROUP * NA_STEP_ROWS,
                                                 NA_WIN_ROWS * W)
    out_block = (1, 1, NA_GROUP * NA_TOK, NA_WIN_ROWS * W)
    return pl.pallas_call(
        _na_bias_kernel,
        grid=(len(variants), NA_HEADS // NA_GROUP),
        in_specs=[pl.BlockSpec((1, 1) + taps.shape[2:], lambda v, g: (v, g, 0, 0)),
                  pl.BlockSpec((1,) + valid.shape[1:], lambda v, g: (v, 0, 0, 0))],
        out_specs=pl.BlockSpec(out_block, lambda v, g: (v, g, 0, 0)),
        out_shape=jax.ShapeDtypeStruct((len(variants), NA_HEADS // NA_GROUP) + out_block[2:], F32),
        name="na_bias",
    )(taps, jnp.asarray(valid))


def _natten_kernel(q_ref, kt_ref, v_ref, kct_ref, vc_ref, *rest):
    bias_refs, o_ref = rest[:-1], rest[-1]
    W = GRID_W
    rows = v_ref.shape[1] // W
    n_kblk = NA_WIN_ROWS // NA_STEP_ROWS
    rb = lax.broadcasted_iota(jnp.int32, (NA_GROUP * NA_TOK, NA_GW), 0) // NA_TOK
    cb = lax.broadcasted_iota(jnp.int32, (NA_GROUP * NA_TOK, NA_GW), 1) // NA_HEAD_DIM
    diag = rb == cb
    groups = [slice(grp * NA_GW, (grp + 1) * NA_GW) for grp in range(NA_HEADS // NA_GROUP)]

    def logits(u, grp):
        gs = groups[grp]
        pair = len(bias_refs) * pl.program_id(1) + u
        w0 = jnp.clip(NA_STEP_ROWS * pair - NA_KR // 2, 0, rows - NA_WIN_ROWS)
        blk0 = w0 // NA_STEP_ROWS
        v_rows = pl.ds(pl.multiple_of(w0 * W, NA_TOK), NA_WIN_ROWS * W)
        qg = q_ref[0, u * NA_TOK:(u + 1) * NA_TOK, gs]
        kt = jnp.concatenate([kt_ref[0, blk0 + j, gs, :] for j in range(n_kblk)], axis=1)
        kct = jnp.concatenate([kct_ref[0, j, gs, :] for j in range(kct_ref.shape[1])], axis=1)
        qbd = jnp.where(diag, jnp.concatenate([qg] * NA_GROUP, axis=0), jnp.zeros_like(qg[:1]))
        return u, gs, v_rows, _dot(qbd, kt) + bias_refs[u][0, grp], _dot(qbd, kct)

    def attend(u, gs, v_rows, s_win, s_ctx):
        vw = v_ref[0, v_rows, gs]
        m = jnp.maximum(jnp.max(s_win, axis=-1, keepdims=True), jnp.max(s_ctx, axis=-1, keepdims=True))
        p_win = jnp.exp(s_win - m)
        p_ctx = jnp.exp(s_ctx - m)
        denom = jnp.sum(p_win, axis=-1, keepdims=True) + jnp.sum(p_ctx, axis=-1, keepdims=True)
        of = (_dot(p_win.astype(BF16), vw) + _dot(p_ctx.astype(BF16), vc_ref[0, :, gs])) / denom
        of = jnp.where(diag, of, 0.0)
        og = of[0:NA_TOK]
        for h in range(1, NA_GROUP):
            og = og + of[h * NA_TOK:(h + 1) * NA_TOK]
        o_ref[0, u * NA_TOK:(u + 1) * NA_TOK, gs] = og.astype(o_ref.dtype)

    order = [(u, grp) for u in range(len(bias_refs)) for grp in range(len(groups))]
    ready = [logits(*unit) for unit in order[:NA_LOGIT_LEAD]]
    for i in range(len(order)):
        if i + NA_LOGIT_LEAD < len(order):
            ready.append(logits(*order[i + NA_LOGIT_LEAD]))
        attend(*ready.pop(0))


def _natten(q, kt, v, kct, vc, bias):
    B, T, _ = q.shape
    rows = T // GRID_W
    _, starts = _na_plan(rows)
    n_pairs = rows // NA_STEP_ROWS
    assert n_pairs % NA_STEP_PAIRS == 0
    tok = pl.BlockSpec((1, NA_STEP_PAIRS * NA_TOK, NA_WIDTH), lambda b, i: (b, i, 0))
    whole = lambda a: pl.BlockSpec((1,) + a.shape[1:], lambda b, i: (b,) + (0,) * (a.ndim - 1))

    def bias_spec(u):
        def index_map(b, i):
            pair = NA_STEP_PAIRS * i + u
            return (sum((pair >= s).astype(jnp.int32) for s in starts[1:]), 0, 0, 0)
        return pl.BlockSpec((1,) + bias.shape[1:], index_map)

    return pl.pallas_call(
        _natten_kernel,
        grid=(B, n_pairs // NA_STEP_PAIRS),
        in_specs=[tok, whole(kt), whole(v), whole(kct), whole(vc)] + [bias_spec(u) for u in range(NA_STEP_PAIRS)],
        out_specs=tok,
        out_shape=jax.ShapeDtypeStruct((B, T, NA_WIDTH), BF16),
        compiler_params=pltpu.CompilerParams(
            dimension_semantics=("parallel", "arbitrary"), vmem_limit_bytes=VMEM_LIMIT),
        name="natten",
    )(q, kt, v, kct, vc, *([bias] * NA_STEP_PAIRS))


def _post_mix_kernel(yna_ref, of_ref, ob_ref, gate_ref, hgw_ref, wo_ref, x_ref, gtm_ref, nmix_ref,
                     scf_ref, shf_ref, gtf_ref, npre_ref, npost_ref, w1_ref, w2_ref, o_ref, act_ref):
    o = of_ref[0] + ob_ref[0]
    gate = gate_ref[0].astype(F32)
    hgw = hgw_ref[...]
    parts = []
    for h in range(HG_HEADS):
        sl = slice(h * HG_DK, (h + 1) * HG_DK)
        parts.append((_rms(o[:, sl], hgw) * gate[:, sl]).astype(BF16))
    y_hg = jnp.concatenate(parts, axis=-1)
    y = _dot(yna_ref[0], wo_ref[:NA_WIDTH, :]) + _dot(y_hg, wo_ref[NA_WIDTH:, :])
    x1 = x_ref[0] + gtm_ref[0] * _rms(y, nmix_ref[...])
    d_ff = w2_ref.shape[0]
    hb = (_rms(x1, npre_ref[...]) * (1.0 + scf_ref[0]) + shf_ref[0]).astype(BF16)
    for j in range(0, d_ff, FFN_COLS):
        gate_j = _dot(hb, w1_ref[:, j:j + FFN_COLS])
        up_j = _dot(hb, w1_ref[:, d_ff + j:d_ff + j + FFN_COLS])
        act_ref[:, j:j + FFN_COLS] = (_silu(gate_j) * up_j).astype(BF16)
    z = _dot(act_ref[...], w2_ref[...])
    o_ref[0] = x1 + gtf_ref[0] * _rms(z, npost_ref[...])


def _post_mix(y_na, o_fw, o_bw, gate, hgw, wo_bf, x, gt_m, nmix, sc_f, sh_f, gt_f, npre, npost, w1_bf, w2_bf, tm):
    B, T, _ = x.shape
    d_ff = w2_bf.shape[0]
    assert d_ff % FFN_COLS == 0
    tok = lambda n: pl.BlockSpec((1, tm, n), lambda b, i: (b, i, 0))
    mod = pl.BlockSpec((1, 1, D_MODEL), lambda b, i: (b, 0, 0))
    const = lambda shape: pl.BlockSpec(shape, lambda b, i: (0,) * len(shape), pipeline_mode=pl.Buffered(1))
    return pl.pallas_call(
        _post_mix_kernel,
        grid=(B, T // tm),
        in_specs=[tok(NA_WIDTH), tok(HG_WIDTH), tok(HG_WIDTH), tok(HG_WIDTH), const((1, HG_DK)),
                  const(wo_bf.shape), tok(D_MODEL), mod, const((1, D_MODEL)),
                  mod, mod, mod, const((1, D_MODEL)), const((1, D_MODEL)),
                  const(w1_bf.shape), const(w2_bf.shape)],
        out_specs=tok(D_MODEL),
        out_shape=jax.ShapeDtypeStruct(x.shape, F32),
        scratch_shapes=[pltpu.VMEM((tm, d_ff), BF16)],
        compiler_params=pltpu.CompilerParams(
            dimension_semantics=("parallel", "parallel"), vmem_limit_bytes=VMEM_LIMIT),
        name="post_mix",
    )(y_na, o_fw, o_bw, gate, hgw, wo_bf, x, gt_m, nmix, sc_f, sh_f, gt_f, npre, npost, w1_bf, w2_bf)


def kernel(x, c, ctx, c_ctx, w_ada, b_ada, norm_mix_pre, norm_mix_post, norm_ffn_pre, norm_ffn_post,
           w_in, na_rpb, hg_lb_logits, hg_norm_w, w_out, w_ffn_in, w_ffn_out):
    B, T, D = x.shape
    assert w_ada.shape[0] == 1, "single-layer stack"
    assert D == D_MODEL and B < SUBLANES and T % GRID_W == 0 and T % HG_BLOCK == 0
    assert ctx.shape[1] % NA_TOK == 0
    rows = T // GRID_W

    cv = jnp.zeros((SUBLANES, D), F32).at[:B].set(c).at[B].set(c_ctx)
    mod = _ada(cv, w_ada[0], b_ada[0][None, :])
    sh_m, sc_m, gt_m, sh_f, sc_f, gt_f = [mod[:, i * D:(i + 1) * D] for i in range(N_MOD)]
    lat = lambda m: m[:B, None, :]
    cx = lambda m: jnp.broadcast_to(m[B][None, None, :], (B, 1, D))

    w_in_bf = w_in[0].astype(BF16)
    lbl = hg_lb_logits.reshape(hg_lb_logits.shape[0], 2 * HG_WIDTH)
    nw_pre = norm_mix_pre[0][None, :]

    q_na, kt_na, v_na, q_hg, g_fw, g_bw, v_hg, gate = _in_proj(x, lat(sc_m), lat(sh_m), nw_pre, w_in_bf, lbl,
                                                                IN_PROJ_ROWS)
    _, kt_c, v_c, _, g_cfw, g_cbw, vhg_c, _ = _in_proj(ctx, cx(sc_m), cx(sh_m), nw_pre, w_in_bf, lbl,
                                                       ctx.shape[1])

    s_fw, s_bw = _ctx_state(g_cfw, g_cbw, vhg_c)
    o_fw, o_bw = _hgrn(q_hg, v_hg, g_fw, g_bw, s_fw, s_bw, HG_BLOCK)

    y_na = _natten(q_na, kt_na, v_na, kt_c, v_c, _na_bias(na_rpb[0], rows))

    return _post_mix(y_na, o_fw, o_bw, gate, hg_norm_w[0][None, :], w_out[0].astype(BF16), x, lat(gt_m),
                     norm_mix_post[0][None, :], lat(sc_f), lat(sh_f), lat(gt_f), norm_ffn_pre[0][None, :],
                     norm_ffn_post[0][None, :], w_ffn_in[0].astype(BF16), w_ffn_out[0].astype(BF16), POST_MIX_ROWS)
```

```python
import jax
import jax.numpy as jnp
import numpy as np
from jax import lax
from jax.experimental import pallas as pl
from jax.experimental.pallas import tpu as pltpu

D_MODEL = 1024
GRID_W = 64
NA_HEADS = 8
NA_HEAD_DIM = 64
NA_WIDTH = NA_HEADS * NA_HEAD_DIM
NA_KR = 8
NA_KC = 16
HG_HEADS = 4
HG_DK = 128
HG_WIDTH = HG_HEADS * HG_DK
HG_CHUNK = 64
N_LEVELS = 6
PROJ_GROUP = NA_WIDTH
assert HG_WIDTH == PROJ_GROUP
N_MOD = 6
EPS = 1e-6
NEG = -1e30
LOG2E = 1.4426950408889634

NA_GROUP = 4
NA_GW = NA_GROUP * NA_HEAD_DIM
NA_STEP_ROWS = 2
NA_TOK = NA_STEP_ROWS * GRID_W
NA_WIN_ROWS = NA_KR + NA_STEP_ROWS
NA_STEP_PAIRS = 2
NA_LOGIT_LEAD = 3

F32 = jnp.float32
BF16 = jnp.bfloat16

VMEM_LIMIT = 56 * 1024 * 1024
IN_PROJ_ROWS = 1024
POST_MIX_ROWS = 512
HG_BLOCK = 4096
ADA_COLS = 1536
FFN_COLS = 256


def _sigmoid(x):
    return 0.5 * jnp.tanh(0.5 * x) + 0.5


def _silu(x):
    half = 0.5 * x
    return half * jnp.tanh(half) + half


def _dot(a, b):
    return jnp.dot(a, b, preferred_element_type=F32)


def _dot_nt(a, b):
    return lax.dot_general(a, b, (((1,), (1,)), ((), ())), preferred_element_type=F32)


def _dot_tn(a, b):
    return lax.dot_general(a, b, (((0,), (0,)), ((), ())), preferred_element_type=F32)


def _split3(x):
    x1 = x.astype(BF16)
    r1 = x - x1.astype(F32)
    x2 = r1.astype(BF16)
    r2 = r1 - x2.astype(F32)
    return x1, x2, r2.astype(BF16)


def _dot_exact_lhs(t, x):
    x1, x2, x3 = _split3(x)
    return _dot(t, x1) + _dot(t, x2) + _dot(t, x3)


def _rms(x, w):
    return x * lax.rsqrt(jnp.mean(x * x, axis=-1, keepdims=True) + EPS) * w


def _ada_kernel(cv_ref, w_ref, b_ref, o_ref):
    s = _silu(cv_ref[...])
    s1, s2, s3 = _split3(s)
    w1, w2, w3 = _split3(w_ref[...])
    acc = _dot(s1, w1) + (_dot(s1, w2) + _dot(s2, w1)) + (_dot(s1, w3) + _dot(s2, w2) + _dot(s3, w1))
    o_ref[...] = acc + b_ref[...]


def _ada(cv, w_ada, b_ada):
    m, n = cv.shape[0], w_ada.shape[1]
    tn = ADA_COLS
    assert n % tn == 0
    return pl.pallas_call(
        _ada_kernel,
        grid=(n // tn,),
        in_specs=[
            pl.BlockSpec((m, D_MODEL), lambda j: (0, 0)),
            pl.BlockSpec((D_MODEL, tn), lambda j: (0, j)),
            pl.BlockSpec((1, tn), lambda j: (0, j)),
        ],
        out_specs=pl.BlockSpec((m, tn), lambda j: (0, j)),
        out_shape=jax.ShapeDtypeStruct((m, n), F32),
        compiler_params=pltpu.CompilerParams(vmem_limit_bytes=VMEM_LIMIT),
        name="ada",
    )(cv, w_ada, b_ada)


def _in_proj_kernel(x_ref, sc_ref, sh_ref, nw_ref, w_ref, lbl_ref,
                    qna_ref, knat_ref, vna_ref, qhg_ref, gfw_ref, gbw_ref, vhg_ref, gate_ref):
    x = x_ref[0]
    h = _rms(x, nw_ref[...]) * (1.0 + sc_ref[0]) + sh_ref[0]
    hb = h.astype(BF16)

    def proj(i):
        return _dot(hb, w_ref[:, i * PROJ_GROUP:(i + 1) * PROJ_GROUP])

    lbl = lbl_ref[...]
    e = jnp.exp(lbl - jnp.max(lbl, axis=0, keepdims=True))
    lb = e[0:1] / jnp.sum(e, axis=0, keepdims=True)

    qna_ref[0] = (proj(0) * (NA_HEAD_DIM ** -0.5)).astype(BF16)
    k_t = proj(1).astype(BF16).T
    for j in range(knat_ref.shape[1]):
        knat_ref[0, j] = k_t[:, j * NA_TOK:(j + 1) * NA_TOK]
    vna_ref[0] = proj(2).astype(BF16)
    qhg_ref[0] = _silu(proj(3)).astype(BF16)
    lb_f = lb[:, :HG_WIDTH]
    lb_b = lb[:, HG_WIDTH:]
    gfw_ref[0] = jnp.log(lb_f + (1.0 - lb_f) * _sigmoid(proj(4)))
    gbw_ref[0] = jnp.log(lb_b + (1.0 - lb_b) * _sigmoid(proj(5)))
    vhg_ref[0] = proj(6).astype(BF16)
    gate_ref[0] = _silu(proj(7)).astype(BF16)


def _in_proj(x, sc, sh, nw, w_bf, lbl, tm):
    B, T, _ = x.shape
    tok = lambda b, i: (b, i, 0)
    assert T % tm == 0 and tm % NA_TOK == 0 and w_bf.shape[1] == 8 * PROJ_GROUP
    out_bf = jax.ShapeDtypeStruct((B, T, PROJ_GROUP), BF16)
    out_f = jax.ShapeDtypeStruct((B, T, PROJ_GROUP), F32)
    ospec = pl.BlockSpec((1, tm, PROJ_GROUP), tok)
    return pl.pallas_call(
        _in_proj_kernel,
        grid=(B, T // tm),
        in_specs=[
            pl.BlockSpec((1, tm, D_MODEL), tok),
            pl.BlockSpec((1, 1, D_MODEL), lambda b, i: (b, 0, 0)),
            pl.BlockSpec((1, 1, D_MODEL), lambda b, i: (b, 0, 0)),
            pl.BlockSpec((1, D_MODEL), lambda b, i: (0, 0)),
            pl.BlockSpec(w_bf.shape, lambda b, i: (0, 0)),
            pl.BlockSpec(lbl.shape, lambda b, i: (0, 0)),
        ],
        out_specs=[ospec, pl.BlockSpec((1, tm // NA_TOK, NA_WIDTH, NA_TOK), lambda b, i: (b, i, 0, 0))]
        + [ospec] * 6,
        out_shape=[out_bf, jax.ShapeDtypeStruct((B, T // NA_TOK, NA_WIDTH, NA_TOK), BF16),
                   out_bf, out_bf, out_f, out_f, out_bf, out_bf],
        compiler_params=pltpu.CompilerParams(
            dimension_semantics=("parallel", "parallel"), vmem_limit_bytes=VMEM_LIMIT),
        name="in_proj",
    )(x, sc, sh, nw, w_bf, lbl)


def _ctx_state_kernel(gfw_ref, gbw_ref, v_ref, sfw_ref, sbw_ref):
    L = gfw_ref.shape[1]
    r = lax.broadcasted_iota(jnp.int32, (L, L), 0)
    c = lax.broadcasted_iota(jnp.int32, (L, L), 1)
    upper = jnp.where(c > r, 1.0, 0.0).astype(BF16)
    lower = jnp.where(c < r, 1.0, 0.0).astype(BF16)
    v = v_ref[0]
    for g_ref, tri, s_ref in ((gfw_ref, upper, sfw_ref), (gbw_ref, lower, sbw_ref)):
        g = g_ref[0]
        kw = ((1.0 - jnp.exp(g)) * jnp.exp(_dot_exact_lhs(tri, g))).astype(BF16)
        for h in range(HG_HEADS):
            sl = slice(h * HG_DK, (h + 1) * HG_DK)
            s_ref[0, h] = _dot_tn(kw[:, sl], v[:, sl])


def _ctx_state(g_cfw, g_cbw, v_c):
    B, L, _ = g_cfw.shape
    tok = pl.BlockSpec((1, L, HG_WIDTH), lambda b: (b, 0, 0))
    st = pl.BlockSpec((1, HG_HEADS, HG_DK, HG_DK), lambda b: (b, 0, 0, 0))
    sshape = jax.ShapeDtypeStruct((B, HG_HEADS, HG_DK, HG_DK), F32)
    return pl.pallas_call(
        _ctx_state_kernel,
        grid=(B,),
        in_specs=[tok, tok, tok],
        out_specs=[st, st],
        out_shape=[sshape, sshape],
        compiler_params=pltpu.CompilerParams(vmem_limit_bytes=VMEM_LIMIT),
        name="ctx_state",
    )(g_cfw, g_cbw, v_c)


SUBLANES = 8
HG_OUTPUT_LAG = 1
HG_DECAY_LEAD = 1


def _hg_masks():
    C = HG_CHUNK
    t = np.arange(C)[:, None]
    s = np.arange(C)[None, :]
    masks = []
    for j in range(N_LEVELS):
        hs = 1 << j
        same = (t // (2 * hs)) == (s // (2 * hs))
        masks.append(same & ((t % (2 * hs)) >= hs) & ((s % (2 * hs)) < hs))
    masks.append(t == s)
    masks = np.stack(masks).astype(np.float32)
    return np.stack([masks, masks[:, ::-1, ::-1]])


class _HgDirection:
    def __init__(self, q_ref, v_ref, g_ref, st_ref, o_ref, mask_ref, rev):
        C = HG_CHUNK
        self.q_ref, self.v_ref, self.st_ref, self.o_ref, self.rev = q_ref, v_ref, st_ref, o_ref, rev
        d = 1 if rev else 0
        self.g = g = g_ref[0] * LOG2E
        pos = lax.broadcasted_iota(jnp.int32, (C, HG_DK), 0) % SUBLANES
        self.scan_keep = {s: jnp.where(pos < SUBLANES - s if rev else pos >= s, 1.0, 0.0)
                          for s in (1 << i for i in range(SUBLANES.bit_length() - 1))}
        self.f = jnp.exp2(g)
        self.k = 1.0 - self.f
        self.level_mask = [mask_ref[d, j] for j in range(N_LEVELS + 1)]
        self.lane = lax.broadcasted_iota(jnp.int32, (SUBLANES, C), 1)
        t = lax.broadcasted_iota(jnp.int32, (C, HG_DK), 0)
        self.query_rows = {j: ((t % (2 << j)) < (1 << j)) if rev else ((t % (2 << j)) >= (1 << j))
                           for j in range(1, N_LEVELS) if (1 << j) < SUBLANES}
        n_pairs = q_ref.shape[1] // (2 * C)
        self.pairs = list(range(n_pairs - 1, -1, -1) if rev else range(n_pairs))
        self.st = st_ref[...]

    def decays(self, p):
        C, S, L = HG_CHUNK, SUBLANES, 2 * HG_DK
        rev = self.rev
        pos_s = lax.broadcasted_iota(jnp.int32, (S, L), 0)
        row = lambda a, r: jnp.broadcast_to(a[r:r + 1, :], (S, L))
        group = lambda a, i: a[i * S:(i + 1) * S]

        halves = []
        for ch in (2 * p, 2 * p + 1):
            c = self.g[ch * C:(ch + 1) * C]
            for s, keep in self.scan_keep.items():
                c = c + keep * pltpu.roll(c, C - s if rev else s, axis=0)
            halves.append(c)
        c = jnp.concatenate(halves, axis=1)

        def split_in_group(hs):
            blk = 2 * hs
            query = (pos_s % blk) < hs if rev else (pos_s % blk) >= hs
            side = jnp.where(query, 1.0, -1.0)
            out = []
            for i in range(C // S):
                mid = row(c, i * S + (hs if rev else hs - 1))
                for a in range(1, S // blk):
                    mid = jnp.where(pos_s >= a * blk, row(c, i * S + a * blk + (hs if rev else hs - 1)), mid)
                out.append((group(c, i) - mid) * side)
            return jnp.concatenate(out, axis=0)

        parts = [split_in_group(1 << j) for j in range(1, N_LEVELS) if 2 << j <= S]

        b_groups, carried, through = [None] * (C // S), None, {}
        for i in (range(C // S - 1, -1, -1) if rev else range(C // S)):
            b_groups[i] = group(c, i) if carried is None else group(c, i) + carried
            total = row(c, i * S + (0 if rev else S - 1))
            carried = total if carried is None else carried + total
            through[i] = carried
        b = jnp.concatenate(b_groups, axis=0)

        def b_row(m):
            assert m % S == (0 if rev else S - 1)
            return through[m // S]

        for j in range(1, N_LEVELS):
            hs = 1 << j
            if 2 * hs <= S:
                continue
            groups = []
            for r in range(0, C, S):
                start = r // (2 * hs) * (2 * hs)
                b_mid = b_row(start + (hs if rev else hs - 1))
                query_side = (r - start >= hs) != rev
                groups.append(b[r:r + S] - b_mid if query_side else b_mid - b[r:r + S])
            parts.append(jnp.concatenate(groups, axis=0))
        parts.append(b)
        parts.append(jnp.concatenate([b_row(0 if rev else C - 1)] * (C // S), axis=0) - b)
        return [jnp.exp2(x) for x in parts]

    def scores(self, p, e2):
        C = HG_CHUNK
        out = []
        for c in ((2 * p + 1, 2 * p) if self.rev else (2 * p, 2 * p + 1)):
            rows = slice(c * C, (c + 1) * C)
            lanes = slice((c - 2 * p) * HG_DK, (c - 2 * p + 1) * HG_DK)
            qc = self.q_ref[0, rows, :].astype(F32)
            kc = self.k[rows]
            k_next = pltpu.roll(kc, C - 1 if self.rev else 1, axis=0)
            a = (jnp.sum(qc * kc, axis=1, keepdims=True) * self.level_mask[N_LEVELS]
                 + jnp.sum(qc * self.f[rows] * k_next, axis=1, keepdims=True) * self.level_mask[0])
            for j in range(1, N_LEVELS):
                hs = 1 << j
                if hs < SUBLANES:
                    z = (jnp.where(self.query_rows[j], qc, kc) * e2[j - 1][:, lanes]).astype(BF16)
                    a = a + _dot_nt(z, z) * self.level_mask[j]
                    continue
                is_query = lambda r: ((r % (2 * hs)) >= hs) != self.rev
                starts = range(0, C, SUBLANES)
                z = jnp.concatenate([(qc if is_query(r) else kc)[r:r + SUBLANES] for r in starts], axis=0)
                z = (z * e2[j - 1][:, lanes]).astype(BF16)
                q_starts = [r for r in starts if is_query(r)]
                x = _dot_nt(jnp.concatenate([z[r:r + SUBLANES] for r in q_starts], axis=0), z)
                pieces = []
                for r in starts:
                    if not is_query(r):
                        pieces.append(a[r:r + SUBLANES])
                        continue
                    key_lo = r // (2 * hs) * (2 * hs) + (hs if self.rev else 0)
                    on_keys = (self.lane >= key_lo) & (self.lane < key_lo + hs)
                    i = q_starts.index(r) * SUBLANES
                    pieces.append(jnp.where(on_keys, x[i:i + SUBLANES], a[r:r + SUBLANES]))
                a = jnp.concatenate(pieces, axis=0)
            e_b = e2[N_LEVELS - 1][:, lanes]
            total = e_b[0:1] if self.rev else e_b[C - 1:C]
            q_dec = jnp.concatenate([(qc * e_b).astype(BF16), a.astype(BF16)], axis=1)
            out.append((rows, q_dec, (kc * e2[N_LEVELS][:, lanes]).astype(BF16), total))
        return out

    def outputs(self, chunk_terms):
        for rows, q_dec, k_dec, total in chunk_terms:
            vc = self.v_ref[0, rows, :]
            self.o_ref[0, rows, :] = _dot(q_dec, jnp.concatenate([self.st.astype(BF16), vc], axis=0))
            total_col = jnp.transpose(jnp.broadcast_to(total, (SUBLANES, HG_DK)))[:, 0:1]
            self.st = total_col * self.st + _dot_tn(k_dec, vc)

    def finish(self):
        self.st_ref[...] = self.st


def _hgrn_kernel(qf_ref, vf_ref, gf_ref, qb_ref, vb_ref, gb_ref, s0f_ref, s0b_ref, mask_ref,
                 of_ref, ob_ref, stf, stb):
    @pl.when(pl.program_id(2) == 0)
    def _():
        stf[...] = s0f_ref[0, 0]
        stb[...] = s0b_ref[0, 0]

    fw = _HgDirection(qf_ref, vf_ref, gf_ref, stf, of_ref, mask_ref, False)
    bw = _HgDirection(qb_ref, vb_ref, gb_ref, stb, ob_ref, mask_ref, True)
    units = [(dirn, p) for pf, pb in zip(fw.pairs, bw.pairs) for dirn, p in ((fw, pf), (bw, pb))]
    ready = [dirn.decays(p) for dirn, p in units[:HG_DECAY_LEAD]]
    pending = []
    for i, (dirn, p) in enumerate(units):
        if i + HG_DECAY_LEAD < len(units):
            ahead, p_ahead = units[i + HG_DECAY_LEAD]
            ready.append(ahead.decays(p_ahead))
        pending.append((dirn, dirn.scores(p, ready.pop(0))))
        if len(pending) > HG_OUTPUT_LAG:
            done, terms = pending.pop(0)
            done.outputs(terms)
    for done, terms in pending:
        done.outputs(terms)
    fw.finish()
    bw.finish()


def _hgrn(q_hg, v_hg, g_fw, g_bw, s_fw, s_bw, tb):
    B, T, _ = q_hg.shape
    nb = T // tb
    fwd = pl.BlockSpec((1, tb, HG_DK), lambda b, h, i: (b, i, h))
    bwd = pl.BlockSpec((1, tb, HG_DK), lambda b, h, i: (b, nb - 1 - i, h))
    st = pl.BlockSpec((1, 1, HG_DK, HG_DK), lambda b, h, i: (b, h, 0, 0))
    masks = jnp.asarray(_hg_masks())
    oshape = jax.ShapeDtypeStruct((B, T, HG_WIDTH), F32)
    return pl.pallas_call(
        _hgrn_kernel,
        grid=(B, HG_HEADS, nb),
        in_specs=[fwd, fwd, fwd, bwd, bwd, bwd, st, st,
                  pl.BlockSpec(masks.shape, lambda b, h, i: (0, 0, 0, 0))],
        out_specs=[fwd, bwd],
        out_shape=[oshape, oshape],
        scratch_shapes=[pltpu.VMEM((HG_DK, HG_DK), F32), pltpu.VMEM((HG_DK, HG_DK), F32)],
        compiler_params=pltpu.CompilerParams(
            dimension_semantics=("parallel", "parallel", "arbitrary"), vmem_limit_bytes=VMEM_LIMIT),
        name="hgrn",
    )(q_hg, v_hg, g_fw, q_hg, v_hg, g_bw, s_fw, s_bw, masks)


def _na_bias_kernel(taps_ref, valid_ref, o_ref):
    W = GRID_W
    for i in range(taps_ref.shape[2]):
        taps = jnp.broadcast_to(taps_ref[0, 0, i:i + 1, :], (W, taps_ref.shape[3]))
        shifted = pltpu.roll(taps, 0, axis=1, stride=1, stride_axis=0)
        o_ref[0, 0, i * W:(i + 1) * W, :] = jnp.where(valid_ref[0, i % NA_STEP_ROWS] > 0.0, shifted, NEG)


def _na_plan(rows):
    assert rows % NA_STEP_ROWS == 0 and rows >= NA_WIN_ROWS and (rows - NA_WIN_ROWS) % NA_STEP_ROWS == 0
    variants, starts = [], []
    for i in range(rows // NA_STEP_ROWS):
        w0 = int(np.clip(NA_STEP_ROWS * i - NA_KR // 2, 0, rows - NA_WIN_ROWS))
        geom = tuple((int(np.clip(r - NA_KR // 2, 0, rows - NA_KR)) - w0, r - w0)
                     for r in range(NA_STEP_ROWS * i, NA_STEP_ROWS * (i + 1)))
        if not variants or variants[-1] != geom:
            assert geom not in variants
            variants.append(geom)
            starts.append(i)
    return variants, starts


def _na_bias(rpb, rows):
    W = GRID_W
    n_dr = 2 * NA_KR - 1
    variants, _ = _na_plan(rows)
    dr = np.array([[[j - qrow + NA_KR - 1 if first <= j < first + NA_KR else n_dr for j in range(NA_WIN_ROWS)]
                    for first, qrow in geom] for geom in variants])
    cols = np.arange(W)
    c0 = np.clip(cols - NA_KC // 2, 0, W - NA_KC)
    col_ok = (cols[None, :] >= c0[:, None]) & (cols[None, :] < c0[:, None] + NA_KC)
    valid = (dr < n_dr)[:, :, None, :, None] & col_ok[None, None, :, None, :]
    valid = valid.reshape(len(variants), NA_STEP_ROWS, W, NA_WIN_ROWS * W).astype(np.float32)

    rpb_ext = jnp.zeros((NA_HEADS, n_dr + 1, 2 * NA_KC - 1), F32).at[:, :n_dr].set(rpb.astype(F32))
    picked = jnp.take(rpb_ext, jnp.asarray(dr.reshape(-1), dtype=jnp.int32), axis=1)
    picked = picked.reshape((NA_HEADS,) + dr.shape + (2 * NA_KC - 1,))
    ahead = picked[..., NA_KC - 1:]
    behind = jnp.roll(picked, -1, axis=3)[..., :NA_KC - 1]
    gap = jnp.zeros(picked.shape[:-1] + (W - 2 * NA_KC + 1,), F32)
    taps = jnp.concatenate([ahead, gap, behind], axis=-1)
    taps = taps.reshape(NA_HEADS // NA_GROUP, NA_GROUP, len(variants), NA_STEP_ROWS, NA_WIN_ROWS * W)
    taps = taps.transpose(2, 0, 1, 3, 4).reshape(len(variants), NA_HEADS // NA_GROUP, NA_GROUP * NA_STEP_ROWS,
                                                 NA_WIN_ROWS * W)
    out_block = (1, 1, NA_GROUP * NA_TOK, NA_WIN_ROWS * W)
    return pl.pallas_call(
        _na_bias_kernel,
        grid=(len(variants), NA_HEADS // NA_GROUP),
        in_specs=[pl.BlockSpec((1, 1) + taps.shape[2:], lambda v, g: (v, g, 0, 0)),
                  pl.BlockSpec((1,) + valid.shape[1:], lambda v, g: (v, 0, 0, 0))],
        out_specs=pl.BlockSpec(out_block, lambda v, g: (v, g, 0, 0)),
        out_shape=jax.ShapeDtypeStruct((len(variants), NA_HEADS // NA_GROUP) + out_block[2:], F32),
        name="na_bias",
    )(taps, jnp.asarray(valid))


def _natten_kernel(q_ref, kt_ref, v_ref, kct_ref, vc_ref, *rest):
    bias_refs, o_ref = rest[:-1], rest[-1]
    W = GRID_W
    rows = v_ref.shape[1] // W
    n_kblk = NA_WIN_ROWS // NA_STEP_ROWS
    rb = lax.broadcasted_iota(jnp.int32, (NA_GROUP * NA_TOK, NA_GW), 0) // NA_TOK
    cb = lax.broadcasted_iota(jnp.int32, (NA_GROUP * NA_TOK, NA_GW), 1) // NA_HEAD_DIM
    diag = rb == cb
    groups = [slice(grp * NA_GW, (grp + 1) * NA_GW) for grp in range(NA_HEADS // NA_GROUP)]

    def logits(u, grp):
        gs = groups[grp]
        pair = len(bias_refs) * pl.program_id(1) + u
        w0 = jnp.clip(NA_STEP_ROWS * pair - NA_KR // 2, 0, rows - NA_WIN_ROWS)
        blk0 = w0 // NA_STEP_ROWS
        v_rows = pl.ds(pl.multiple_of(w0 * W, NA_TOK), NA_WIN_ROWS * W)
        qg = q_ref[0, u * NA_TOK:(u + 1) * NA_TOK, gs]
        kt = jnp.concatenate([kt_ref[0, blk0 + j, gs, :] for j in range(n_kblk)], axis=1)
        kct = jnp.concatenate([kct_ref[0, j, gs, :] for j in range(kct_ref.shape[1])], axis=1)
        qbd = jnp.where(diag, jnp.concatenate([qg] * NA_GROUP, axis=0), jnp.zeros_like(qg[:1]))
        return u, gs, v_rows, _dot(qbd, kt) + bias_refs[u][0, grp], _dot(qbd, kct)

    def attend(u, gs, v_rows, s_win, s_ctx):
        vw = v_ref[0, v_rows, gs]
        m = jnp.maximum(jnp.max(s_win, axis=-1, keepdims=True), jnp.max(s_ctx, axis=-1, keepdims=True))
        p_win = jnp.exp(s_win - m)
        p_ctx = jnp.exp(s_ctx - m)
        denom = jnp.sum(p_win, axis=-1, keepdims=True) + jnp.sum(p_ctx, axis=-1, keepdims=True)
        of = (_dot(p_win.astype(BF16), vw) + _dot(p_ctx.astype(BF16), vc_ref[0, :, gs])) / denom
        of = jnp.where(diag, of, 0.0)
        og = of[0:NA_TOK]
        for h in range(1, NA_GROUP):
            og = og + of[h * NA_TOK:(h + 1) * NA_TOK]
        o_ref[0, u * NA_TOK:(u + 1) * NA_TOK, gs] = og.astype(o_ref.dtype)

    order = [(u, grp) for u in range(len(bias_refs)) for grp in range(len(groups))]
    ready = [logits(*unit) for unit in order[:NA_LOGIT_LEAD]]
    for i in range(len(order)):
        if i + NA_LOGIT_LEAD < len(order):
            ready.append(logits(*order[i + NA_LOGIT_LEAD]))
        attend(*ready.pop(0))


def _natten(q, kt, v, kct, vc, bias):
    B, T, _ = q.shape
    rows = T // GRID_W
    _, starts = _na_plan(rows)
    n_pairs = rows // NA_STEP_ROWS
    assert n_pairs % NA_STEP_PAIRS == 0
    tok = pl.BlockSpec((1, NA_STEP_PAIRS * NA_TOK, NA_WIDTH), lambda b, i: (b, i, 0))
    whole = lambda a: pl.BlockSpec((1,) + a.shape[1:], lambda b, i: (b,) + (0,) * (a.ndim - 1))

    def bias_spec(u):
        def index_map(b, i):
            pair = NA_STEP_PAIRS * i + u
            return (sum((pair >= s).astype(jnp.int32) for s in starts[1:]), 0, 0, 0)
        return pl.BlockSpec((1,) + bias.shape[1:], index_map)

    return pl.pallas_call(
        _natten_kernel,
        grid=(B, n_pairs // NA_STEP_PAIRS),
        in_specs=[tok, whole(kt), whole(v), whole(kct), whole(vc)] + [bias_spec(u) for u in range(NA_STEP_PAIRS)],
        out_specs=tok,
        out_shape=jax.ShapeDtypeStruct((B, T, NA_WIDTH), BF16),
        compiler_params=pltpu.CompilerParams(
            dimension_semantics=("parallel", "arbitrary"), vmem_limit_bytes=VMEM_LIMIT),
        name="natten",
    )(q, kt, v, kct, vc, *([bias] * NA_STEP_PAIRS))


class _CastRing:
    def __init__(self, stage_ref, sem, jobs):
        self.stage_ref, self.sem, self.jobs, self.next = stage_ref, sem, jobs, 0
        for k in range(min(2, len(jobs))):
            self._copy(k).start()

    def _copy(self, k):
        return pltpu.make_async_copy(self.jobs[k][0], self.stage_ref.at[k % 2], self.sem.at[k % 2])

    def consume(self):
        k = self.next
        _, dst_ref, dst = self.jobs[k]
        self._copy(k).wait()
        dst_ref[dst] = self.stage_ref[k % 2].astype(BF16)
        if k + 2 < len(self.jobs):
            self._copy(k + 2).start()
        self.next += 1


def _post_mix_kernel(yna_ref, of_ref, ob_ref, gate_ref, hgw_ref, wo_hbm, x_ref, gtm_ref, nmix_ref,
                     scf_ref, shf_ref, gtf_ref, npre_ref, npost_ref, w1_hbm, w2_hbm, o_ref,
                     act_ref, wo_ref, w1_ref, w2_ref, stage_cols, stage_rows, sem_cols, sem_rows):
    d_ff = w2_ref.shape[0]

    def body(need_wo, need_slab, after_slab):
        o = of_ref[0] + ob_ref[0]
        gate = gate_ref[0].astype(F32)
        hgw = hgw_ref[...]
        parts = []
        for h in range(HG_HEADS):
            sl = slice(h * HG_DK, (h + 1) * HG_DK)
            parts.append((_rms(o[:, sl], hgw) * gate[:, sl]).astype(BF16))
        y_hg = jnp.concatenate(parts, axis=-1)
        need_wo()
        y = _dot(yna_ref[0], wo_ref[:NA_WIDTH, :]) + _dot(y_hg, wo_ref[NA_WIDTH:, :])
        x1 = x_ref[0] + gtm_ref[0] * _rms(y, nmix_ref[...])
        hb = (_rms(x1, npre_ref[...]) * (1.0 + scf_ref[0]) + shf_ref[0]).astype(BF16)
        for j in range(0, d_ff, FFN_COLS):
            need_slab()
            gate_j = _dot(hb, w1_ref[:, j:j + FFN_COLS])
            up_j = _dot(hb, w1_ref[:, d_ff + j:d_ff + j + FFN_COLS])
            act_ref[:, j:j + FFN_COLS] = (_silu(gate_j) * up_j).astype(BF16)
            after_slab()
        z = _dot(act_ref[...], w2_ref[...])
        o_ref[0] = x1 + gtf_ref[0] * _rms(z, npost_ref[...])

    first = (pl.program_id(0) == 0) & (pl.program_id(1) == 0)

    @pl.when(first)
    def _():
        rows, cols = stage_rows.shape[1], stage_cols.shape[2]
        by_rows = lambda hbm, ref: [(hbm.at[pl.ds(r, rows), :], ref, (pl.ds(r, rows), slice(None)))
                                    for r in range(0, ref.shape[0], rows)]
        col_job = lambda c: (w1_hbm.at[:, pl.ds(c, cols)], w1_ref, (slice(None), pl.ds(c, cols)))
        wo_jobs, w2_jobs = by_rows(wo_hbm, wo_ref), by_rows(w2_hbm, w2_ref)
        row_ring = _CastRing(stage_rows, sem_rows, wo_jobs + w2_jobs)
        col_ring = _CastRing(stage_cols, sem_cols,
                             [col_job(c) for j in range(0, d_ff, FFN_COLS) for c in (j, d_ff + j)])

        def need_wo():
            for _ in wo_jobs:
                row_ring.consume()

        def need_slab():
            col_ring.consume()
            col_ring.consume()

        body(need_wo, need_slab, row_ring.consume)
        assert row_ring.next == len(row_ring.jobs) and col_ring.next == len(col_ring.jobs)

    @pl.when(jnp.logical_not(first))
    def _():
        nothing = lambda: None
        body(nothing, nothing, nothing)


def _post_mix(y_na, o_fw, o_bw, gate, hgw, wo, x, gt_m, nmix, sc_f, sh_f, gt_f, npre, npost, w1, w2, tm):
    B, T, _ = x.shape
    d_ff = w2.shape[0]
    stage_rows = FFN_COLS
    assert d_ff % FFN_COLS == 0 and wo.shape[0] % stage_rows == 0 and wo.shape[1] == w2.shape[1]
    hbm = pl.BlockSpec(memory_space=pl.ANY)
    tok = lambda n: pl.BlockSpec((1, tm, n), lambda b, i: (b, i, 0))
    mod = pl.BlockSpec((1, 1, D_MODEL), lambda b, i: (b, 0, 0))
    const = lambda shape: pl.BlockSpec(shape, lambda b, i: (0,) * len(shape), pipeline_mode=pl.Buffered(1))
    return pl.pallas_call(
        _post_mix_kernel,
        grid=(B, T // tm),
        in_specs=[tok(NA_WIDTH), tok(HG_WIDTH), tok(HG_WIDTH), tok(HG_WIDTH), const((1, HG_DK)),
                  hbm, tok(D_MODEL), mod, const((1, D_MODEL)),
                  mod, mod, mod, const((1, D_MODEL)), const((1, D_MODEL)),
                  hbm, hbm],
        out_specs=tok(D_MODEL),
        out_shape=jax.ShapeDtypeStruct(x.shape, F32),
        scratch_shapes=[pltpu.VMEM((tm, d_ff), BF16),
                        pltpu.VMEM(wo.shape, BF16), pltpu.VMEM(w1.shape, BF16), pltpu.VMEM(w2.shape, BF16),
                        pltpu.VMEM((2, w1.shape[0], FFN_COLS), F32), pltpu.VMEM((2, stage_rows, w2.shape[1]), F32),
                        pltpu.SemaphoreType.DMA((2,)), pltpu.SemaphoreType.DMA((2,))],
        compiler_params=pltpu.CompilerParams(
            dimension_semantics=("arbitrary", "arbitrary"), vmem_limit_bytes=VMEM_LIMIT),
        name="post_mix",
    )(y_na, o_fw, o_bw, gate, hgw, wo, x, gt_m, nmix, sc_f, sh_f, gt_f, npre, npost, w1, w2)


def kernel(x, c, ctx, c_ctx, w_ada, b_ada, norm_mix_pre, norm_mix_post, norm_ffn_pre, norm_ffn_post,
           w_in, na_rpb, hg_lb_logits, hg_norm_w, w_out, w_ffn_in, w_ffn_out):
    B, T, D = x.shape
    assert w_ada.shape[0] == 1, "single-layer stack"
    assert D == D_MODEL and B < SUBLANES and T % GRID_W == 0 and T % HG_BLOCK == 0
    assert ctx.shape[1] % NA_TOK == 0
    rows = T // GRID_W

    cv = jnp.zeros((SUBLANES, D), F32).at[:B].set(c).at[B].set(c_ctx)
    mod = _ada(cv, w_ada[0], b_ada[0][None, :])
    sh_m, sc_m, gt_m, sh_f, sc_f, gt_f = [mod[:, i * D:(i + 1) * D] for i in range(N_MOD)]
    lat = lambda m: m[:B, None, :]
    cx = lambda m: jnp.broadcast_to(m[B][None, None, :], (B, 1, D))

    w_in_bf = w_in[0].astype(BF16)
    lbl = hg_lb_logits.reshape(hg_lb_logits.shape[0], 2 * HG_WIDTH)
    nw_pre = norm_mix_pre[0][None, :]

    q_na, kt_na, v_na, q_hg, g_fw, g_bw, v_hg, gate = _in_proj(x, lat(sc_m), lat(sh_m), nw_pre, w_in_bf, lbl,
                                                                IN_PROJ_ROWS)
    _, kt_c, v_c, _, g_cfw, g_cbw, vhg_c, _ = _in_proj(ctx, cx(sc_m), cx(sh_m), nw_pre, w_in_bf, lbl,
                                                       ctx.shape[1])

    s_fw, s_bw = _ctx_state(g_cfw, g_cbw, vhg_c)
    o_fw, o_bw = _hgrn(q_hg, v_hg, g_fw, g_bw, s_fw, s_bw, HG_BLOCK)

    y_na = _natten(q_na, kt_na, v_na, kt_c, v_c, _na_bias(na_rpb[0], rows))

    return _post_mix(y_na, o_fw, o_bw, gate, hg_norm_w[0][None, :], w_out[0], x, lat(gt_m),
                     norm_mix_post[0][None, :], lat(sc_f), lat(sh_f), lat(gt_f), norm_ffn_pre[0][None, :],
                     norm_ffn_post[0][None, :], w_ffn_in[0], w_ffn_out[0], POST_MIX_ROWS)
```

```python
import jax
import jax.numpy as jnp
import numpy as np
from jax import lax
from jax.experimental import pallas as pl
from jax.experimental.pallas import tpu as pltpu

D_MODEL = 1024
GRID_W = 64
NA_HEADS = 8
NA_HEAD_DIM = 64
NA_WIDTH = NA_HEADS * NA_HEAD_DIM
NA_KR = 8
NA_KC = 16
HG_HEADS = 4
HG_DK = 128
HG_WIDTH = HG_HEADS * HG_DK
HG_CHUNK = 64
N_LEVELS = 6
PROJ_GROUP = NA_WIDTH
assert HG_WIDTH == PROJ_GROUP
N_MOD = 6
EPS = 1e-6
NEG = -1e30
LOG2E = 1.4426950408889634

NA_GROUP = 4
NA_GW = NA_GROUP * NA_HEAD_DIM
NA_STEP_ROWS = 2
NA_TOK = NA_STEP_ROWS * GRID_W
NA_WIN_ROWS = NA_KR + NA_STEP_ROWS
NA_STEP_PAIRS = 2
NA_LOGIT_LEAD = 3

F32 = jnp.float32
BF16 = jnp.bfloat16

VMEM_LIMIT = 56 * 1024 * 1024
IN_PROJ_ROWS = 1024
POST_MIX_ROWS = 512
HG_BLOCK = 4096
ADA_COLS = 1536
FFN_COLS = 256


def _sigmoid(x):
    return 0.5 * jnp.tanh(0.5 * x) + 0.5


def _silu(x):
    half = 0.5 * x
    return half * jnp.tanh(half) + half


def _dot(a, b):
    return jnp.dot(a, b, preferred_element_type=F32)


def _dot_nt(a, b):
    return lax.dot_general(a, b, (((1,), (1,)), ((), ())), preferred_element_type=F32)


def _dot_tn(a, b):
    return lax.dot_general(a, b, (((0,), (0,)), ((), ())), preferred_element_type=F32)


def _split3(x):
    x1 = x.astype(BF16)
    r1 = x - x1.astype(F32)
    x2 = r1.astype(BF16)
    r2 = r1 - x2.astype(F32)
    return x1, x2, r2.astype(BF16)


def _dot_exact_lhs(t, x):
    x1, x2, x3 = _split3(x)
    return _dot(t, x1) + _dot(t, x2) + _dot(t, x3)


def _rms(x, w):
    return x * lax.rsqrt(jnp.mean(x * x, axis=-1, keepdims=True) + EPS) * w


def _ada_kernel(cv_ref, w_ref, b_ref, o_ref):
    s = _silu(cv_ref[...])
    s1, s2, s3 = _split3(s)
    w1, w2, w3 = _split3(w_ref[...])
    acc = _dot(s1, w1) + (_dot(s1, w2) + _dot(s2, w1)) + (_dot(s1, w3) + _dot(s2, w2) + _dot(s3, w1))
    o_ref[...] = acc + b_ref[...]


def _ada(cv, w_ada, b_ada):
    m, n = cv.shape[0], w_ada.shape[1]
    tn = ADA_COLS
    assert n % tn == 0
    return pl.pallas_call(
        _ada_kernel,
        grid=(n // tn,),
        in_specs=[
            pl.BlockSpec((m, D_MODEL), lambda j: (0, 0)),
            pl.BlockSpec((D_MODEL, tn), lambda j: (0, j)),
            pl.BlockSpec((1, tn), lambda j: (0, j)),
        ],
        out_specs=pl.BlockSpec((m, tn), lambda j: (0, j)),
        out_shape=jax.ShapeDtypeStruct((m, n), F32),
        compiler_params=pltpu.CompilerParams(vmem_limit_bytes=VMEM_LIMIT),
        name="ada",
    )(cv, w_ada, b_ada)


def _in_proj_kernel(x_ref, sc_ref, sh_ref, nw_ref, w_hbm, lbl_ref,
                    qna_ref, knat_ref, vna_ref, qhg_ref, gfw_ref, gbw_ref, vhg_ref, gate_ref,
                    w_ref, stage_ref, sem):
    def body(need_group):
        x = x_ref[0]
        h = _rms(x, nw_ref[...]) * (1.0 + sc_ref[0]) + sh_ref[0]
        hb = h.astype(BF16)

        def proj(i):
            need_group()
            return _dot(hb, w_ref[:, i * PROJ_GROUP:(i + 1) * PROJ_GROUP])

        lbl = lbl_ref[...]
        e = jnp.exp(lbl - jnp.max(lbl, axis=0, keepdims=True))
        lb = e[0:1] / jnp.sum(e, axis=0, keepdims=True)

        qna_ref[0] = (proj(0) * (NA_HEAD_DIM ** -0.5)).astype(BF16)
        k_t = proj(1).astype(BF16).T
        for j in range(knat_ref.shape[1]):
            knat_ref[0, j] = k_t[:, j * NA_TOK:(j + 1) * NA_TOK]
        vna_ref[0] = proj(2).astype(BF16)
        qhg_ref[0] = _silu(proj(3)).astype(BF16)
        lb_f = lb[:, :HG_WIDTH]
        lb_b = lb[:, HG_WIDTH:]
        gfw_ref[0] = jnp.log(lb_f + (1.0 - lb_f) * _sigmoid(proj(4)))
        gbw_ref[0] = jnp.log(lb_b + (1.0 - lb_b) * _sigmoid(proj(5)))
        vhg_ref[0] = proj(6).astype(BF16)
        gate_ref[0] = _silu(proj(7)).astype(BF16)

    first = (pl.program_id(0) == 0) & (pl.program_id(1) == 0)

    @pl.when(first)
    def _():
        jobs = [(w_hbm.at[:, pl.ds(c, PROJ_GROUP)], w_ref, (slice(None), pl.ds(c, PROJ_GROUP)))
                for c in range(0, w_ref.shape[1], PROJ_GROUP)]
        ring = _CastRing(stage_ref, sem, jobs)
        body(ring.consume)
        assert ring.next == len(jobs)

    @pl.when(jnp.logical_not(first))
    def _():
        body(lambda: None)


def _in_proj(x, sc, sh, nw, w, lbl, tm):
    B, T, _ = x.shape
    tok = lambda b, i: (b, i, 0)
    assert T % tm == 0 and tm % NA_TOK == 0 and w.shape[1] == 8 * PROJ_GROUP
    out_bf = jax.ShapeDtypeStruct((B, T, PROJ_GROUP), BF16)
    out_f = jax.ShapeDtypeStruct((B, T, PROJ_GROUP), F32)
    ospec = pl.BlockSpec((1, tm, PROJ_GROUP), tok)
    return pl.pallas_call(
        _in_proj_kernel,
        grid=(B, T // tm),
        in_specs=[
            pl.BlockSpec((1, tm, D_MODEL), tok),
            pl.BlockSpec((1, 1, D_MODEL), lambda b, i: (b, 0, 0)),
            pl.BlockSpec((1, 1, D_MODEL), lambda b, i: (b, 0, 0)),
            pl.BlockSpec((1, D_MODEL), lambda b, i: (0, 0)),
            pl.BlockSpec(memory_space=pl.ANY),
            pl.BlockSpec(lbl.shape, lambda b, i: (0, 0)),
        ],
        out_specs=[ospec, pl.BlockSpec((1, tm // NA_TOK, NA_WIDTH, NA_TOK), lambda b, i: (b, i, 0, 0))]
        + [ospec] * 6,
        out_shape=[out_bf, jax.ShapeDtypeStruct((B, T // NA_TOK, NA_WIDTH, NA_TOK), BF16),
                   out_bf, out_bf, out_f, out_f, out_bf, out_bf],
        scratch_shapes=[pltpu.VMEM(w.shape, BF16), pltpu.VMEM((2, w.shape[0], PROJ_GROUP), F32),
                        pltpu.SemaphoreType.DMA((2,))],
        compiler_params=pltpu.CompilerParams(
            dimension_semantics=("arbitrary", "arbitrary"), vmem_limit_bytes=VMEM_LIMIT),
        name="in_proj",
    )(x, sc, sh, nw, w, lbl)


def _ctx_state_kernel(gfw_ref, gbw_ref, v_ref, sfw_ref, sbw_ref):
    L = gfw_ref.shape[1]
    r = lax.broadcasted_iota(jnp.int32, (L, L), 0)
    c = lax.broadcasted_iota(jnp.int32, (L, L), 1)
    upper = jnp.where(c > r, 1.0, 0.0).astype(BF16)
    lower = jnp.where(c < r, 1.0, 0.0).astype(BF16)
    v = v_ref[0]
    for g_ref, tri, s_ref in ((gfw_ref, upper, sfw_ref), (gbw_ref, lower, sbw_ref)):
        g = g_ref[0]
        kw = ((1.0 - jnp.exp(g)) * jnp.exp(_dot_exact_lhs(tri, g))).astype(BF16)
        for h in range(HG_HEADS):
            sl = slice(h * HG_DK, (h + 1) * HG_DK)
            s_ref[0, h] = _dot_tn(kw[:, sl], v[:, sl])


def _ctx_state(g_cfw, g_cbw, v_c):
    B, L, _ = g_cfw.shape
    tok = pl.BlockSpec((1, L, HG_WIDTH), lambda b: (b, 0, 0))
    st = pl.BlockSpec((1, HG_HEADS, HG_DK, HG_DK), lambda b: (b, 0, 0, 0))
    sshape = jax.ShapeDtypeStruct((B, HG_HEADS, HG_DK, HG_DK), F32)
    return pl.pallas_call(
        _ctx_state_kernel,
        grid=(B,),
        in_specs=[tok, tok, tok],
        out_specs=[st, st],
        out_shape=[sshape, sshape],
        compiler_params=pltpu.CompilerParams(vmem_limit_bytes=VMEM_LIMIT),
        name="ctx_state",
    )(g_cfw, g_cbw, v_c)


SUBLANES = 8
HG_OUTPUT_LAG = 1
HG_DECAY_LEAD = 1


def _hg_masks():
    C = HG_CHUNK
    t = np.arange(C)[:, None]
    s = np.arange(C)[None, :]
    masks = []
    for j in range(N_LEVELS):
        hs = 1 << j
        same = (t // (2 * hs)) == (s // (2 * hs))
        masks.append(same & ((t % (2 * hs)) >= hs) & ((s % (2 * hs)) < hs))
    masks.append(t == s)
    masks = np.stack(masks).astype(np.float32)
    return np.stack([masks, masks[:, ::-1, ::-1]])


class _HgDirection:
    def __init__(self, q_ref, v_ref, g_ref, st_ref, o_ref, mask_ref, rev):
        C = HG_CHUNK
        self.q_ref, self.v_ref, self.st_ref, self.o_ref, self.rev = q_ref, v_ref, st_ref, o_ref, rev
        d = 1 if rev else 0
        self.g = g = g_ref[0] * LOG2E
        pos = lax.broadcasted_iota(jnp.int32, (C, HG_DK), 0) % SUBLANES
        self.scan_keep = {s: jnp.where(pos < SUBLANES - s if rev else pos >= s, 1.0, 0.0)
                          for s in (1 << i for i in range(SUBLANES.bit_length() - 1))}
        self.f = jnp.exp2(g)
        self.k = 1.0 - self.f
        self.level_mask = [mask_ref[d, j] for j in range(N_LEVELS + 1)]
        self.lane = lax.broadcasted_iota(jnp.int32, (SUBLANES, C), 1)
        t = lax.broadcasted_iota(jnp.int32, (C, HG_DK), 0)
        self.query_rows = {j: ((t % (2 << j)) < (1 << j)) if rev else ((t % (2 << j)) >= (1 << j))
                           for j in range(1, N_LEVELS) if (1 << j) < SUBLANES}
        n_pairs = q_ref.shape[1] // (2 * C)
        self.pairs = list(range(n_pairs - 1, -1, -1) if rev else range(n_pairs))
        self.st = st_ref[...]

    def decays(self, p):
        C, S, L = HG_CHUNK, SUBLANES, 2 * HG_DK
        rev = self.rev
        pos_s = lax.broadcasted_iota(jnp.int32, (S, L), 0)
        row = lambda a, r: jnp.broadcast_to(a[r:r + 1, :], (S, L))
        group = lambda a, i: a[i * S:(i + 1) * S]

        halves = []
        for ch in (2 * p, 2 * p + 1):
            c = self.g[ch * C:(ch + 1) * C]
            for s, keep in self.scan_keep.items():
                c = c + keep * pltpu.roll(c, C - s if rev else s, axis=0)
            halves.append(c)
        c = jnp.concatenate(halves, axis=1)

        def split_in_group(hs):
            blk = 2 * hs
            query = (pos_s % blk) < hs if rev else (pos_s % blk) >= hs
            side = jnp.where(query, 1.0, -1.0)
            out = []
            for i in range(C // S):
                mid = row(c, i * S + (hs if rev else hs - 1))
                for a in range(1, S // blk):
                    mid = jnp.where(pos_s >= a * blk, row(c, i * S + a * blk + (hs if rev else hs - 1)), mid)
                out.append((group(c, i) - mid) * side)
            return jnp.concatenate(out, axis=0)

        parts = [split_in_group(1 << j) for j in range(1, N_LEVELS) if 2 << j <= S]

        b_groups, carried, through = [None] * (C // S), None, {}
        for i in (range(C // S - 1, -1, -1) if rev else range(C // S)):
            b_groups[i] = group(c, i) if carried is None else group(c, i) + carried
            total = row(c, i * S + (0 if rev else S - 1))
            carried = total if carried is None else carried + total
            through[i] = carried
        b = jnp.concatenate(b_groups, axis=0)

        def b_row(m):
            assert m % S == (0 if rev else S - 1)
            return through[m // S]

        for j in range(1, N_LEVELS):
            hs = 1 << j
            if 2 * hs <= S:
                continue
            groups = []
            for r in range(0, C, S):
                start = r // (2 * hs) * (2 * hs)
                b_mid = b_row(start + (hs if rev else hs - 1))
                query_side = (r - start >= hs) != rev
                groups.append(b[r:r + S] - b_mid if query_side else b_mid - b[r:r + S])
            parts.append(jnp.concatenate(groups, axis=0))
        parts.append(b)
        parts.append(jnp.concatenate([b_row(0 if rev else C - 1)] * (C // S), axis=0) - b)
        return [jnp.exp2(x) for x in parts]

    def scores(self, p, e2):
        C = HG_CHUNK
        out = []
        for c in ((2 * p + 1, 2 * p) if self.rev else (2 * p, 2 * p + 1)):
            rows = slice(c * C, (c + 1) * C)
            lanes = slice((c - 2 * p) * HG_DK, (c - 2 * p + 1) * HG_DK)
            qc = self.q_ref[0, rows, :].astype(F32)
            kc = self.k[rows]
            k_next = pltpu.roll(kc, C - 1 if self.rev else 1, axis=0)
            a = (jnp.sum(qc * kc, axis=1, keepdims=True) * self.level_mask[N_LEVELS]
                 + jnp.sum(qc * self.f[rows] * k_next, axis=1, keepdims=True) * self.level_mask[0])
            for j in range(1, N_LEVELS):
                hs = 1 << j
                if hs < SUBLANES:
                    z = (jnp.where(self.query_rows[j], qc, kc) * e2[j - 1][:, lanes]).astype(BF16)
                    a = a + _dot_nt(z, z) * self.level_mask[j]
                    continue
                is_query = lambda r: ((r % (2 * hs)) >= hs) != self.rev
                starts = range(0, C, SUBLANES)
                z = jnp.concatenate([(qc if is_query(r) else kc)[r:r + SUBLANES] for r in starts], axis=0)
                z = (z * e2[j - 1][:, lanes]).astype(BF16)
                q_starts = [r for r in starts if is_query(r)]
                x = _dot_nt(jnp.concatenate([z[r:r + SUBLANES] for r in q_starts], axis=0), z)
                pieces = []
                for r in starts:
                    if not is_query(r):
                        pieces.append(a[r:r + SUBLANES])
                        continue
                    key_lo = r // (2 * hs) * (2 * hs) + (hs if self.rev else 0)
                    on_keys = (self.lane >= key_lo) & (self.lane < key_lo + hs)
                    i = q_starts.index(r) * SUBLANES
                    pieces.append(jnp.where(on_keys, x[i:i + SUBLANES], a[r:r + SUBLANES]))
                a = jnp.concatenate(pieces, axis=0)
            e_b = e2[N_LEVELS - 1][:, lanes]
            total = e_b[0:1] if self.rev else e_b[C - 1:C]
            q_dec = jnp.concatenate([(qc * e_b).astype(BF16), a.astype(BF16)], axis=1)
            out.append((rows, q_dec, (kc * e2[N_LEVELS][:, lanes]).astype(BF16), total))
        return out

    def outputs(self, chunk_terms):
        for rows, q_dec, k_dec, total in chunk_terms:
            vc = self.v_ref[0, rows, :]
            self.o_ref[0, rows, :] = _dot(q_dec, jnp.concatenate([self.st.astype(BF16), vc], axis=0))
            total_col = jnp.transpose(jnp.broadcast_to(total, (SUBLANES, HG_DK)))[:, 0:1]
            self.st = total_col * self.st + _dot_tn(k_dec, vc)

    def finish(self):
        self.st_ref[...] = self.st


def _hgrn_kernel(qf_ref, vf_ref, gf_ref, qb_ref, vb_ref, gb_ref, s0f_ref, s0b_ref, mask_ref,
                 of_ref, ob_ref, stf, stb):
    @pl.when(pl.program_id(2) == 0)
    def _():
        stf[...] = s0f_ref[0, 0]
        stb[...] = s0b_ref[0, 0]

    fw = _HgDirection(qf_ref, vf_ref, gf_ref, stf, of_ref, mask_ref, False)
    bw = _HgDirection(qb_ref, vb_ref, gb_ref, stb, ob_ref, mask_ref, True)
    units = [(dirn, p) for pf, pb in zip(fw.pairs, bw.pairs) for dirn, p in ((fw, pf), (bw, pb))]
    ready = [dirn.decays(p) for dirn, p in units[:HG_DECAY_LEAD]]
    pending = []
    for i, (dirn, p) in enumerate(units):
        if i + HG_DECAY_LEAD < len(units):
            ahead, p_ahead = units[i + HG_DECAY_LEAD]
            ready.append(ahead.decays(p_ahead))
        pending.append((dirn, dirn.scores(p, ready.pop(0))))
        if len(pending) > HG_OUTPUT_LAG:
            done, terms = pending.pop(0)
            done.outputs(terms)
    for done, terms in pending:
        done.outputs(terms)
    fw.finish()
    bw.finish()


def _hgrn(q_hg, v_hg, g_fw, g_bw, s_fw, s_bw, tb):
    B, T, _ = q_hg.shape
    nb = T // tb
    fwd = pl.BlockSpec((1, tb, HG_DK), lambda b, h, i: (b, i, h))
    bwd = pl.BlockSpec((1, tb, HG_DK), lambda b, h, i: (b, nb - 1 - i, h))
    st = pl.BlockSpec((1, 1, HG_DK, HG_DK), lambda b, h, i: (b, h, 0, 0))
    masks = jnp.asarray(_hg_masks())
    oshape = jax.ShapeDtypeStruct((B, T, HG_WIDTH), F32)
    return pl.pallas_call(
        _hgrn_kernel,
        grid=(B, HG_HEADS, nb),
        in_specs=[fwd, fwd, fwd, bwd, bwd, bwd, st, st,
                  pl.BlockSpec(masks.shape, lambda b, h, i: (0, 0, 0, 0))],
        out_specs=[fwd, bwd],
        out_shape=[oshape, oshape],
        scratch_shapes=[pltpu.VMEM((HG_DK, HG_DK), F32), pltpu.VMEM((HG_DK, HG_DK), F32)],
        compiler_params=pltpu.CompilerParams(
            dimension_semantics=("parallel", "parallel", "arbitrary"), vmem_limit_bytes=VMEM_LIMIT),
        name="hgrn",
    )(q_hg, v_hg, g_fw, q_hg, v_hg, g_bw, s_fw, s_bw, masks)


def _na_bias_kernel(taps_ref, valid_ref, o_ref):
    W = GRID_W
    for i in range(taps_ref.shape[2]):
        taps = jnp.broadcast_to(taps_ref[0, 0, i:i + 1, :], (W, taps_ref.shape[3]))
        shifted = pltpu.roll(taps, 0, axis=1, stride=1, stride_axis=0)
        o_ref[0, 0, i * W:(i + 1) * W, :] = jnp.where(valid_ref[0, i % NA_STEP_ROWS] > 0.0, shifted, NEG)


def _na_plan(rows):
    assert rows % NA_STEP_ROWS == 0 and rows >= NA_WIN_ROWS and (rows - NA_WIN_ROWS) % NA_STEP_ROWS == 0
    variants, starts = [], []
    for i in range(rows // NA_STEP_ROWS):
        w0 = int(np.clip(NA_STEP_ROWS * i - NA_KR // 2, 0, rows - NA_WIN_ROWS))
        geom = tuple((int(np.clip(r - NA_KR // 2, 0, rows - NA_KR)) - w0, r - w0)
                     for r in range(NA_STEP_ROWS * i, NA_STEP_ROWS * (i + 1)))
        if not variants or variants[-1] != geom:
            assert geom not in variants
            variants.append(geom)
            starts.append(i)
    return variants, starts


def _na_bias(rpb, rows):
    W = GRID_W
    n_dr = 2 * NA_KR - 1
    variants, _ = _na_plan(rows)
    dr = np.array([[[j - qrow + NA_KR - 1 if first <= j < first + NA_KR else n_dr for j in range(NA_WIN_ROWS)]
                    for first, qrow in geom] for geom in variants])
    cols = np.arange(W)
    c0 = np.clip(cols - NA_KC // 2, 0, W - NA_KC)
    col_ok = (cols[None, :] >= c0[:, None]) & (cols[None, :] < c0[:, None] + NA_KC)
    valid = (dr < n_dr)[:, :, None, :, None] & col_ok[None, None, :, None, :]
    valid = valid.reshape(len(variants), NA_STEP_ROWS, W, NA_WIN_ROWS * W).astype(np.float32)

    rpb_ext = jnp.zeros((NA_HEADS, n_dr + 1, 2 * NA_KC - 1), F32).at[:, :n_dr].set(rpb.astype(F32))
    picked = jnp.take(rpb_ext, jnp.asarray(dr.reshape(-1), dtype=jnp.int32), axis=1)
    picked = picked.reshape((NA_HEADS,) + dr.shape + (2 * NA_KC - 1,))
    ahead = picked[..., NA_KC - 1:]
    behind = jnp.roll(picked, -1, axis=3)[..., :NA_KC - 1]
    gap = jnp.zeros(picked.shape[:-1] + (W - 2 * NA_KC + 1,), F32)
    taps = jnp.concatenate([ahead, gap, behind], axis=-1)
    taps = taps.reshape(NA_HEADS // NA_GROUP, NA_GROUP, len(variants), NA_STEP_ROWS, NA_WIN_ROWS * W)
    taps = taps.transpose(2, 0, 1, 3, 4).reshape(len(variants), NA_HEADS // NA_GROUP, NA_GROUP * NA_STEP_ROWS,
                                                 NA_WIN_ROWS * W)
    out_block = (1, 1, NA_GROUP * NA_TOK, NA_WIN_ROWS * W)
    return pl.pallas_call(
        _na_bias_kernel,
        grid=(len(variants), NA_HEADS // NA_GROUP),
        in_specs=[pl.BlockSpec((1, 1) + taps.shape[2:], lambda v, g: (v, g, 0, 0)),
                  pl.BlockSpec((1,) + valid.shape[1:], lambda v, g: (v, 0, 0, 0))],
        out_specs=pl.BlockSpec(out_block, lambda v, g: (v, g, 0, 0)),
        out_shape=jax.ShapeDtypeStruct((len(variants), NA_HEADS // NA_GROUP) + out_block[2:], F32),
        name="na_bias",
    )(taps, jnp.asarray(valid))


def _natten_kernel(q_ref, kt_ref, v_ref, kct_ref, vc_ref, *rest):
    bias_refs, o_ref = rest[:-1], rest[-1]
    W = GRID_W
    rows = v_ref.shape[1] // W
    n_kblk = NA_WIN_ROWS // NA_STEP_ROWS
    rb = lax.broadcasted_iota(jnp.int32, (NA_GROUP * NA_TOK, NA_GW), 0) // NA_TOK
    cb = lax.broadcasted_iota(jnp.int32, (NA_GROUP * NA_TOK, NA_GW), 1) // NA_HEAD_DIM
    diag = rb == cb
    groups = [slice(grp * NA_GW, (grp + 1) * NA_GW) for grp in range(NA_HEADS // NA_GROUP)]

    def logits(u, grp):
        gs = groups[grp]
        pair = len(bias_refs) * pl.program_id(1) + u
        w0 = jnp.clip(NA_STEP_ROWS * pair - NA_KR // 2, 0, rows - NA_WIN_ROWS)
        blk0 = w0 // NA_STEP_ROWS
        v_rows = pl.ds(pl.multiple_of(w0 * W, NA_TOK), NA_WIN_ROWS * W)
        qg = q_ref[0, u * NA_TOK:(u + 1) * NA_TOK, gs]
        kt = jnp.concatenate([kt_ref[0, blk0 + j, gs, :] for j in range(n_kblk)], axis=1)
        kct = jnp.concatenate([kct_ref[0, j, gs, :] for j in range(kct_ref.shape[1])], axis=1)
        qbd = jnp.where(diag, jnp.concatenate([qg] * NA_GROUP, axis=0), jnp.zeros_like(qg[:1]))
        return u, gs, v_rows, _dot(qbd, kt) + bias_refs[u][0, grp], _dot(qbd, kct)

    def attend(u, gs, v_rows, s_win, s_ctx):
        vw = v_ref[0, v_rows, gs]
        m = jnp.maximum(jnp.max(s_win, axis=-1, keepdims=True), jnp.max(s_ctx, axis=-1, keepdims=True))
        p_win = jnp.exp(s_win - m)
        p_ctx = jnp.exp(s_ctx - m)
        denom = jnp.sum(p_win, axis=-1, keepdims=True) + jnp.sum(p_ctx, axis=-1, keepdims=True)
        of = (_dot(p_win.astype(BF16), vw) + _dot(p_ctx.astype(BF16), vc_ref[0, :, gs])) / denom
        of = jnp.where(diag, of, 0.0)
        og = of[0:NA_TOK]
        for h in range(1, NA_GROUP):
            og = og + of[h * NA_TOK:(h + 1) * NA_TOK]
        o_ref[0, u * NA_TOK:(u + 1) * NA_TOK, gs] = og.astype(o_ref.dtype)

    order = [(u, grp) for u in range(len(bias_refs)) for grp in range(len(groups))]
    ready = [logits(*unit) for unit in order[:NA_LOGIT_LEAD]]
    for i in range(len(order)):
        if i + NA_LOGIT_LEAD < len(order):
            ready.append(logits(*order[i + NA_LOGIT_LEAD]))
        attend(*ready.pop(0))


def _natten(q, kt, v, kct, vc, bias):
    B, T, _ = q.shape
    rows = T // GRID_W
    _, starts = _na_plan(rows)
    n_pairs = rows // NA_STEP_ROWS
    assert n_pairs % NA_STEP_PAIRS == 0
    tok = pl.BlockSpec((1, NA_STEP_PAIRS * NA_TOK, NA_WIDTH), lambda b, i: (b, i, 0))
    whole = lambda a: pl.BlockSpec((1,) + a.shape[1:], lambda b, i: (b,) + (0,) * (a.ndim - 1))

    def bias_spec(u):
        def index_map(b, i):
            pair = NA_STEP_PAIRS * i + u
            return (sum((pair >= s).astype(jnp.int32) for s in starts[1:]), 0, 0, 0)
        return pl.BlockSpec((1,) + bias.shape[1:], index_map)

    return pl.pallas_call(
        _natten_kernel,
        grid=(B, n_pairs // NA_STEP_PAIRS),
        in_specs=[tok, whole(kt), whole(v), whole(kct), whole(vc)] + [bias_spec(u) for u in range(NA_STEP_PAIRS)],
        out_specs=tok,
        out_shape=jax.ShapeDtypeStruct((B, T, NA_WIDTH), BF16),
        compiler_params=pltpu.CompilerParams(
            dimension_semantics=("parallel", "arbitrary"), vmem_limit_bytes=VMEM_LIMIT),
        name="natten",
    )(q, kt, v, kct, vc, *([bias] * NA_STEP_PAIRS))


class _CastRing:
    def __init__(self, stage_ref, sem, jobs):
        self.stage_ref, self.sem, self.jobs, self.next = stage_ref, sem, jobs, 0
        for k in range(min(2, len(jobs))):
            self._copy(k).start()

    def _copy(self, k):
        return pltpu.make_async_copy(self.jobs[k][0], self.stage_ref.at[k % 2], self.sem.at[k % 2])

    def consume(self):
        k = self.next
        _, dst_ref, dst = self.jobs[k]
        self._copy(k).wait()
        dst_ref[dst] = self.stage_ref[k % 2].astype(BF16)
        if k + 2 < len(self.jobs):
            self._copy(k + 2).start()
        self.next += 1


def _post_mix_kernel(yna_ref, of_ref, ob_ref, gate_ref, hgw_ref, wo_hbm, x_ref, gtm_ref, nmix_ref,
                     scf_ref, shf_ref, gtf_ref, npre_ref, npost_ref, w1_hbm, w2_hbm, o_ref,
                     act_ref, wo_ref, w1_ref, w2_ref, stage_cols, stage_rows, sem_cols, sem_rows):
    d_ff = w2_ref.shape[0]

    def body(need_wo, need_slab, after_slab):
        o = of_ref[0] + ob_ref[0]
        gate = gate_ref[0].astype(F32)
        hgw = hgw_ref[...]
        parts = []
        for h in range(HG_HEADS):
            sl = slice(h * HG_DK, (h + 1) * HG_DK)
            parts.append((_rms(o[:, sl], hgw) * gate[:, sl]).astype(BF16))
        y_hg = jnp.concatenate(parts, axis=-1)
        need_wo()
        y = _dot(yna_ref[0], wo_ref[:NA_WIDTH, :]) + _dot(y_hg, wo_ref[NA_WIDTH:, :])
        x1 = x_ref[0] + gtm_ref[0] * _rms(y, nmix_ref[...])
        hb = (_rms(x1, npre_ref[...]) * (1.0 + scf_ref[0]) + shf_ref[0]).astype(BF16)
        for j in range(0, d_ff, FFN_COLS):
            need_slab()
            gate_j = _dot(hb, w1_ref[:, j:j + FFN_COLS])
            up_j = _dot(hb, w1_ref[:, d_ff + j:d_ff + j + FFN_COLS])
            act_ref[:, j:j + FFN_COLS] = (_silu(gate_j) * up_j).astype(BF16)
            after_slab()
        z = _dot(act_ref[...], w2_ref[...])
        o_ref[0] = x1 + gtf_ref[0] * _rms(z, npost_ref[...])

    first = (pl.program_id(0) == 0) & (pl.program_id(1) == 0)

    @pl.when(first)
    def _():
        rows, cols = stage_rows.shape[1], stage_cols.shape[2]
        by_rows = lambda hbm, ref: [(hbm.at[pl.ds(r, rows), :], ref, (pl.ds(r, rows), slice(None)))
                                    for r in range(0, ref.shape[0], rows)]
        col_job = lambda c: (w1_hbm.at[:, pl.ds(c, cols)], w1_ref, (slice(None), pl.ds(c, cols)))
        wo_jobs, w2_jobs = by_rows(wo_hbm, wo_ref), by_rows(w2_hbm, w2_ref)
        row_ring = _CastRing(stage_rows, sem_rows, wo_jobs + w2_jobs)
        col_ring = _CastRing(stage_cols, sem_cols,
                             [col_job(c) for j in range(0, d_ff, FFN_COLS) for c in (j, d_ff + j)])

        def need_wo():
            for _ in wo_jobs:
                row_ring.consume()

        def need_slab():
            col_ring.consume()
            col_ring.consume()

        body(need_wo, need_slab, row_ring.consume)
        assert row_ring.next == len(row_ring.jobs) and col_ring.next == len(col_ring.jobs)

    @pl.when(jnp.logical_not(first))
    def _():
        nothing = lambda: None
        body(nothing, nothing, nothing)


def _post_mix(y_na, o_fw, o_bw, gate, hgw, wo, x, gt_m, nmix, sc_f, sh_f, gt_f, npre, npost, w1, w2, tm):
    B, T, _ = x.shape
    d_ff = w2.shape[0]
    stage_rows = FFN_COLS
    assert d_ff % FFN_COLS == 0 and wo.shape[0] % stage_rows == 0 and wo.shape[1] == w2.shape[1]
    hbm = pl.BlockSpec(memory_space=pl.ANY)
    tok = lambda n: pl.BlockSpec((1, tm, n), lambda b, i: (b, i, 0))
    mod = pl.BlockSpec((1, 1, D_MODEL), lambda b, i: (b, 0, 0))
    const = lambda shape: pl.BlockSpec(shape, lambda b, i: (0,) * len(shape), pipeline_mode=pl.Buffered(1))
    return pl.pallas_call(
        _post_mix_kernel,
        grid=(B, T // tm),
        in_specs=[tok(NA_WIDTH), tok(HG_WIDTH), tok(HG_WIDTH), tok(HG_WIDTH), const((1, HG_DK)),
                  hbm, tok(D_MODEL), mod, const((1, D_MODEL)),
                  mod, mod, mod, const((1, D_MODEL)), const((1, D_MODEL)),
                  hbm, hbm],
        out_specs=tok(D_MODEL),
        out_shape=jax.ShapeDtypeStruct(x.shape, F32),
        scratch_shapes=[pltpu.VMEM((tm, d_ff), BF16),
                        pltpu.VMEM(wo.shape, BF16), pltpu.VMEM(w1.shape, BF16), pltpu.VMEM(w2.shape, BF16),
                        pltpu.VMEM((2, w1.shape[0], FFN_COLS), F32), pltpu.VMEM((2, stage_rows, w2.shape[1]), F32),
                        pltpu.SemaphoreType.DMA((2,)), pltpu.SemaphoreType.DMA((2,))],
        compiler_params=pltpu.CompilerParams(
            dimension_semantics=("arbitrary", "arbitrary"), vmem_limit_bytes=VMEM_LIMIT),
        name="post_mix",
    )(y_na, o_fw, o_bw, gate, hgw, wo, x, gt_m, nmix, sc_f, sh_f, gt_f, npre, npost, w1, w2)


def kernel(x, c, ctx, c_ctx, w_ada, b_ada, norm_mix_pre, norm_mix_post, norm_ffn_pre, norm_ffn_post,
           w_in, na_rpb, hg_lb_logits, hg_norm_w, w_out, w_ffn_in, w_ffn_out):
    B, T, D = x.shape
    assert w_ada.shape[0] == 1, "single-layer stack"
    assert D == D_MODEL and B < SUBLANES and T % GRID_W == 0 and T % HG_BLOCK == 0
    assert ctx.shape[1] % NA_TOK == 0
    rows = T // GRID_W

    cv = jnp.zeros((SUBLANES, D), F32).at[:B].set(c).at[B].set(c_ctx)
    mod = _ada(cv, w_ada[0], b_ada[0][None, :])
    sh_m, sc_m, gt_m, sh_f, sc_f, gt_f = [mod[:, i * D:(i + 1) * D] for i in range(N_MOD)]
    lat = lambda m: m[:B, None, :]
    cx = lambda m: jnp.broadcast_to(m[B][None, None, :], (B, 1, D))

    w_in_bf = w_in[0]
    lbl = hg_lb_logits.reshape(hg_lb_logits.shape[0], 2 * HG_WIDTH)
    nw_pre = norm_mix_pre[0][None, :]

    q_na, kt_na, v_na, q_hg, g_fw, g_bw, v_hg, gate = _in_proj(x, lat(sc_m), lat(sh_m), nw_pre, w_in_bf, lbl,
                                                                IN_PROJ_ROWS)
    _, kt_c, v_c, _, g_cfw, g_cbw, vhg_c, _ = _in_proj(ctx, cx(sc_m), cx(sh_m), nw_pre, w_in_bf, lbl,
                                                       ctx.shape[1])

    s_fw, s_bw = _ctx_state(g_cfw, g_cbw, vhg_c)
    o_fw, o_bw = _hgrn(q_hg, v_hg, g_fw, g_bw, s_fw, s_bw, HG_BLOCK)

    y_na = _natten(q_na, kt_na, v_na, kt_c, v_c, _na_bias(na_rpb[0], rows))

    return _post_mix(y_na, o_fw, o_bw, gate, hg_norm_w[0][None, :], w_out[0], x, lat(gt_m),
                     norm_mix_post[0][None, :], lat(sc_f), lat(sh_f), lat(gt_f), norm_ffn_pre[0][None, :],
                     norm_ffn_post[0][None, :], w_ffn_in[0], w_ffn_out[0], POST_MIX_ROWS)
```

```python
import jax
import jax.numpy as jnp
import numpy as np
from jax import lax
from jax.experimental import pallas as pl
from jax.experimental.pallas import tpu as pltpu

D_MODEL = 1024
GRID_W = 64
NA_HEADS = 8
NA_HEAD_DIM = 64
NA_WIDTH = NA_HEADS * NA_HEAD_DIM
NA_KR = 8
NA_KC = 16
HG_HEADS = 4
HG_DK = 128
HG_WIDTH = HG_HEADS * HG_DK
HG_CHUNK = 64
N_LEVELS = 6
PROJ_GROUP = NA_WIDTH
assert HG_WIDTH == PROJ_GROUP
N_MOD = 6
EPS = 1e-6
NEG = -1e30
LOG2E = 1.4426950408889634

NA_GROUP = 4
NA_GW = NA_GROUP * NA_HEAD_DIM
NA_STEP_ROWS = 2
NA_TOK = NA_STEP_ROWS * GRID_W
NA_WIN_ROWS = NA_KR + NA_STEP_ROWS
NA_STEP_PAIRS = 2
NA_LOGIT_LEAD = 3

F32 = jnp.float32
BF16 = jnp.bfloat16

VMEM_LIMIT = 56 * 1024 * 1024
IN_PROJ_ROWS = 1024
POST_MIX_ROWS = 512
HG_BLOCK = 4096
ADA_COLS = 1536
FFN_COLS = 256
PM_RING = 3


def _sigmoid(x):
    return 0.5 * jnp.tanh(0.5 * x) + 0.5


def _silu(x):
    half = 0.5 * x
    return half * jnp.tanh(half) + half


def _dot(a, b):
    return jnp.dot(a, b, preferred_element_type=F32)


def _dot_nt(a, b):
    return lax.dot_general(a, b, (((1,), (1,)), ((), ())), preferred_element_type=F32)


def _dot_tn(a, b):
    return lax.dot_general(a, b, (((0,), (0,)), ((), ())), preferred_element_type=F32)


def _split3(x):
    x1 = x.astype(BF16)
    r1 = x - x1.astype(F32)
    x2 = r1.astype(BF16)
    r2 = r1 - x2.astype(F32)
    return x1, x2, r2.astype(BF16)


def _dot_exact_lhs(t, x):
    x1, x2, x3 = _split3(x)
    return _dot(t, x1) + _dot(t, x2) + _dot(t, x3)


def _rms(x, w):
    return x * lax.rsqrt(jnp.mean(x * x, axis=-1, keepdims=True) + EPS) * w


def _ada_kernel(cv_ref, w_ref, b_ref, o_ref):
    s = _silu(cv_ref[...])
    s1, s2, s3 = _split3(s)
    w1, w2, w3 = _split3(w_ref[...])
    acc = _dot(s1, w1) + (_dot(s1, w2) + _dot(s2, w1)) + (_dot(s1, w3) + _dot(s2, w2) + _dot(s3, w1))
    o_ref[...] = acc + b_ref[...]


def _ada(cv, w_ada, b_ada):
    m, n = cv.shape[0], w_ada.shape[1]
    tn = ADA_COLS
    assert n % tn == 0
    return pl.pallas_call(
        _ada_kernel,
        grid=(n // tn,),
        in_specs=[
            pl.BlockSpec((m, D_MODEL), lambda j: (0, 0)),
            pl.BlockSpec((D_MODEL, tn), lambda j: (0, j)),
            pl.BlockSpec((1, tn), lambda j: (0, j)),
        ],
        out_specs=pl.BlockSpec((m, tn), lambda j: (0, j)),
        out_shape=jax.ShapeDtypeStruct((m, n), F32),
        compiler_params=pltpu.CompilerParams(vmem_limit_bytes=VMEM_LIMIT),
        name="ada",
    )(cv, w_ada, b_ada)


def _in_proj_kernel(x_ref, sc_ref, sh_ref, nw_ref, w_ref, lbl_ref,
                    qna_ref, knat_ref, vna_ref, qhg_ref, gfw_ref, gbw_ref, vhg_ref, gate_ref):
    x = x_ref[0]
    h = _rms(x, nw_ref[...]) * (1.0 + sc_ref[0]) + sh_ref[0]
    hb = h.astype(BF16)

    def proj(i):
        return _dot(hb, w_ref[:, i * PROJ_GROUP:(i + 1) * PROJ_GROUP])

    lbl = lbl_ref[...]
    e = jnp.exp(lbl - jnp.max(lbl, axis=0, keepdims=True))
    lb = e[0:1] / jnp.sum(e, axis=0, keepdims=True)

    qna_ref[0] = (proj(0) * (NA_HEAD_DIM ** -0.5)).astype(BF16)
    k_t = proj(1).astype(BF16).T
    for j in range(knat_ref.shape[1]):
        knat_ref[0, j] = k_t[:, j * NA_TOK:(j + 1) * NA_TOK]
    vna_ref[0] = proj(2).astype(BF16)
    qhg_ref[0] = _silu(proj(3)).astype(BF16)
    lb_f = lb[:, :HG_WIDTH]
    lb_b = lb[:, HG_WIDTH:]
    gfw_ref[0] = jnp.log(lb_f + (1.0 - lb_f) * _sigmoid(proj(4)))
    gbw_ref[0] = jnp.log(lb_b + (1.0 - lb_b) * _sigmoid(proj(5)))
    vhg_ref[0] = proj(6).astype(BF16)
    gate_ref[0] = _silu(proj(7)).astype(BF16)


def _in_proj(x, sc, sh, nw, w_bf, lbl, tm):
    B, T, _ = x.shape
    tok = lambda b, i: (b, i, 0)
    assert T % tm == 0 and tm % NA_TOK == 0 and w_bf.shape[1] == 8 * PROJ_GROUP
    out_bf = jax.ShapeDtypeStruct((B, T, PROJ_GROUP), BF16)
    out_f = jax.ShapeDtypeStruct((B, T, PROJ_GROUP), F32)
    ospec = pl.BlockSpec((1, tm, PROJ_GROUP), tok)
    return pl.pallas_call(
        _in_proj_kernel,
        grid=(B, T // tm),
        in_specs=[
            pl.BlockSpec((1, tm, D_MODEL), tok),
            pl.BlockSpec((1, 1, D_MODEL), lambda b, i: (b, 0, 0)),
            pl.BlockSpec((1, 1, D_MODEL), lambda b, i: (b, 0, 0)),
            pl.BlockSpec((1, D_MODEL), lambda b, i: (0, 0)),
            pl.BlockSpec(w_bf.shape, lambda b, i: (0, 0)),
            pl.BlockSpec(lbl.shape, lambda b, i: (0, 0)),
        ],
        out_specs=[ospec, pl.BlockSpec((1, tm // NA_TOK, NA_WIDTH, NA_TOK), lambda b, i: (b, i, 0, 0))]
        + [ospec] * 6,
        out_shape=[out_bf, jax.ShapeDtypeStruct((B, T // NA_TOK, NA_WIDTH, NA_TOK), BF16),
                   out_bf, out_bf, out_f, out_f, out_bf, out_bf],
        compiler_params=pltpu.CompilerParams(
            dimension_semantics=("parallel", "parallel"), vmem_limit_bytes=VMEM_LIMIT),
        name="in_proj",
    )(x, sc, sh, nw, w_bf, lbl)


def _ctx_state_kernel(gfw_ref, gbw_ref, v_ref, sfw_ref, sbw_ref):
    L = gfw_ref.shape[1]
    r = lax.broadcasted_iota(jnp.int32, (L, L), 0)
    c = lax.broadcasted_iota(jnp.int32, (L, L), 1)
    upper = jnp.where(c > r, 1.0, 0.0).astype(BF16)
    lower = jnp.where(c < r, 1.0, 0.0).astype(BF16)
    v = v_ref[0]
    for g_ref, tri, s_ref in ((gfw_ref, upper, sfw_ref), (gbw_ref, lower, sbw_ref)):
        g = g_ref[0]
        kw = ((1.0 - jnp.exp(g)) * jnp.exp(_dot_exact_lhs(tri, g))).astype(BF16)
        for h in range(HG_HEADS):
            sl = slice(h * HG_DK, (h + 1) * HG_DK)
            s_ref[0, h] = _dot_tn(kw[:, sl], v[:, sl])


def _ctx_state(g_cfw, g_cbw, v_c):
    B, L, _ = g_cfw.shape
    tok = pl.BlockSpec((1, L, HG_WIDTH), lambda b: (b, 0, 0))
    st = pl.BlockSpec((1, HG_HEADS, HG_DK, HG_DK), lambda b: (b, 0, 0, 0))
    sshape = jax.ShapeDtypeStruct((B, HG_HEADS, HG_DK, HG_DK), F32)
    return pl.pallas_call(
        _ctx_state_kernel,
        grid=(B,),
        in_specs=[tok, tok, tok],
        out_specs=[st, st],
        out_shape=[sshape, sshape],
        compiler_params=pltpu.CompilerParams(vmem_limit_bytes=VMEM_LIMIT),
        name="ctx_state",
    )(g_cfw, g_cbw, v_c)


SUBLANES = 8
HG_OUTPUT_LAG = 1
HG_DECAY_LEAD = 1


def _hg_masks():
    C = HG_CHUNK
    t = np.arange(C)[:, None]
    s = np.arange(C)[None, :]
    masks = []
    for j in range(N_LEVELS):
        hs = 1 << j
        same = (t // (2 * hs)) == (s // (2 * hs))
        masks.append(same & ((t % (2 * hs)) >= hs) & ((s % (2 * hs)) < hs))
    masks.append(t == s)
    masks = np.stack(masks).astype(np.float32)
    return np.stack([masks, masks[:, ::-1, ::-1]])


class _HgDirection:
    def __init__(self, q_ref, v_ref, g_ref, st_ref, o_ref, mask_ref, rev):
        C = HG_CHUNK
        self.q_ref, self.v_ref, self.st_ref, self.o_ref, self.rev = q_ref, v_ref, st_ref, o_ref, rev
        d = 1 if rev else 0
        self.g = g = g_ref[0] * LOG2E
        pos = lax.broadcasted_iota(jnp.int32, (C, HG_DK), 0) % SUBLANES
        self.scan_keep = {s: jnp.where(pos < SUBLANES - s if rev else pos >= s, 1.0, 0.0)
                          for s in (1 << i for i in range(SUBLANES.bit_length() - 1))}
        self.f = jnp.exp2(g)
        self.k = 1.0 - self.f
        self.level_mask = [mask_ref[d, j] for j in range(N_LEVELS + 1)]
        self.lane = lax.broadcasted_iota(jnp.int32, (SUBLANES, C), 1)
        t = lax.broadcasted_iota(jnp.int32, (C, HG_DK), 0)
        self.query_rows = {j: ((t % (2 << j)) < (1 << j)) if rev else ((t % (2 << j)) >= (1 << j))
                           for j in range(1, N_LEVELS) if (1 << j) < SUBLANES}
        n_pairs = q_ref.shape[1] // (2 * C)
        self.pairs = list(range(n_pairs - 1, -1, -1) if rev else range(n_pairs))
        self.st = st_ref[...]

    def decays(self, p):
        C, S, L = HG_CHUNK, SUBLANES, 2 * HG_DK
        rev = self.rev
        pos_s = lax.broadcasted_iota(jnp.int32, (S, L), 0)
        row = lambda a, r: jnp.broadcast_to(a[r:r + 1, :], (S, L))
        group = lambda a, i: a[i * S:(i + 1) * S]

        halves = []
        for ch in (2 * p, 2 * p + 1):
            c = self.g[ch * C:(ch + 1) * C]
            for s, keep in self.scan_keep.items():
                c = c + keep * pltpu.roll(c, C - s if rev else s, axis=0)
            halves.append(c)
        c = jnp.concatenate(halves, axis=1)

        def split_in_group(hs):
            blk = 2 * hs
            query = (pos_s % blk) < hs if rev else (pos_s % blk) >= hs
            side = jnp.where(query, 1.0, -1.0)
            out = []
            for i in range(C // S):
                mid = row(c, i * S + (hs if rev else hs - 1))
                for a in range(1, S // blk):
                    mid = jnp.where(pos_s >= a * blk, row(c, i * S + a * blk + (hs if rev else hs - 1)), mid)
                out.append((group(c, i) - mid) * side)
            return jnp.concatenate(out, axis=0)

        parts = [split_in_group(1 << j) for j in range(1, N_LEVELS) if 2 << j <= S]

        b_groups, carried, through = [None] * (C // S), None, {}
        for i in (range(C // S - 1, -1, -1) if rev else range(C // S)):
            b_groups[i] = group(c, i) if carried is None else group(c, i) + carried
            total = row(c, i * S + (0 if rev else S - 1))
            carried = total if carried is None else carried + total
            through[i] = carried
        b = jnp.concatenate(b_groups, axis=0)

        def b_row(m):
            assert m % S == (0 if rev else S - 1)
            return through[m // S]

        for j in range(1, N_LEVELS):
            hs = 1 << j
            if 2 * hs <= S:
                continue
            groups = []
            for r in range(0, C, S):
                start = r // (2 * hs) * (2 * hs)
                b_mid = b_row(start + (hs if rev else hs - 1))
                query_side = (r - start >= hs) != rev
                groups.append(b[r:r + S] - b_mid if query_side else b_mid - b[r:r + S])
            parts.append(jnp.concatenate(groups, axis=0))
        parts.append(b)
        parts.append(jnp.concatenate([b_row(0 if rev else C - 1)] * (C // S), axis=0) - b)
        return [jnp.exp2(x) for x in parts]

    def scores(self, p, e2):
        C = HG_CHUNK
        out = []
        for c in ((2 * p + 1, 2 * p) if self.rev else (2 * p, 2 * p + 1)):
            rows = slice(c * C, (c + 1) * C)
            lanes = slice((c - 2 * p) * HG_DK, (c - 2 * p + 1) * HG_DK)
            qc = self.q_ref[0, rows, :].astype(F32)
            kc = self.k[rows]
            k_next = pltpu.roll(kc, C - 1 if self.rev else 1, axis=0)
            a = (jnp.sum(qc * kc, axis=1, keepdims=True) * self.level_mask[N_LEVELS]
                 + jnp.sum(qc * self.f[rows] * k_next, axis=1, keepdims=True) * self.level_mask[0])
            for j in range(1, N_LEVELS):
                hs = 1 << j
                if hs < SUBLANES:
                    z = (jnp.where(self.query_rows[j], qc, kc) * e2[j - 1][:, lanes]).astype(BF16)
                    a = a + _dot_nt(z, z) * self.level_mask[j]
                    continue
                is_query = lambda r: ((r % (2 * hs)) >= hs) != self.rev
                starts = range(0, C, SUBLANES)
                z = jnp.concatenate([(qc if is_query(r) else kc)[r:r + SUBLANES] for r in starts], axis=0)
                z = (z * e2[j - 1][:, lanes]).astype(BF16)
                q_starts = [r for r in starts if is_query(r)]
                x = _dot_nt(jnp.concatenate([z[r:r + SUBLANES] for r in q_starts], axis=0), z)
                pieces = []
                for r in starts:
                    if not is_query(r):
                        pieces.append(a[r:r + SUBLANES])
                        continue
                    key_lo = r // (2 * hs) * (2 * hs) + (hs if self.rev else 0)
                    on_keys = (self.lane >= key_lo) & (self.lane < key_lo + hs)
                    i = q_starts.index(r) * SUBLANES
                    pieces.append(jnp.where(on_keys, x[i:i + SUBLANES], a[r:r + SUBLANES]))
                a = jnp.concatenate(pieces, axis=0)
            e_b = e2[N_LEVELS - 1][:, lanes]
            total = e_b[0:1] if self.rev else e_b[C - 1:C]
            q_dec = jnp.concatenate([(qc * e_b).astype(BF16), a.astype(BF16)], axis=1)
            out.append((rows, q_dec, (kc * e2[N_LEVELS][:, lanes]).astype(BF16), total))
        return out

    def outputs(self, chunk_terms):
        for rows, q_dec, k_dec, total in chunk_terms:
            vc = self.v_ref[0, rows, :]
            self.o_ref[0, rows, :] = _dot(q_dec, jnp.concatenate([self.st.astype(BF16), vc], axis=0))
            total_col = jnp.transpose(jnp.broadcast_to(total, (SUBLANES, HG_DK)))[:, 0:1]
            self.st = total_col * self.st + _dot_tn(k_dec, vc)

    def finish(self):
        self.st_ref[...] = self.st


def _hgrn_kernel(qf_ref, vf_ref, gf_ref, qb_ref, vb_ref, gb_ref, s0f_ref, s0b_ref, mask_ref,
                 of_ref, ob_ref, stf, stb):
    @pl.when(pl.program_id(2) == 0)
    def _():
        stf[...] = s0f_ref[0, 0]
        stb[...] = s0b_ref[0, 0]

    fw = _HgDirection(qf_ref, vf_ref, gf_ref, stf, of_ref, mask_ref, False)
    bw = _HgDirection(qb_ref, vb_ref, gb_ref, stb, ob_ref, mask_ref, True)
    units = [(dirn, p) for pf, pb in zip(fw.pairs, bw.pairs) for dirn, p in ((fw, pf), (bw, pb))]
    ready = [dirn.decays(p) for dirn, p in units[:HG_DECAY_LEAD]]
    pending = []
    for i, (dirn, p) in enumerate(units):
        if i + HG_DECAY_LEAD < len(units):
            ahead, p_ahead = units[i + HG_DECAY_LEAD]
            ready.append(ahead.decays(p_ahead))
        pending.append((dirn, dirn.scores(p, ready.pop(0))))
        if len(pending) > HG_OUTPUT_LAG:
            done, terms = pending.pop(0)
            done.outputs(terms)
    for done, terms in pending:
        done.outputs(terms)
    fw.finish()
    bw.finish()


def _hgrn(q_hg, v_hg, g_fw, g_bw, s_fw, s_bw, tb):
    B, T, _ = q_hg.shape
    nb = T // tb
    fwd = pl.BlockSpec((1, tb, HG_DK), lambda b, h, i: (b, i, h))
    bwd = pl.BlockSpec((1, tb, HG_DK), lambda b, h, i: (b, nb - 1 - i, h))
    st = pl.BlockSpec((1, 1, HG_DK, HG_DK), lambda b, h, i: (b, h, 0, 0))
    masks = jnp.asarray(_hg_masks())
    oshape = jax.ShapeDtypeStruct((B, T, HG_WIDTH), F32)
    return pl.pallas_call(
        _hgrn_kernel,
        grid=(B, HG_HEADS, nb),
        in_specs=[fwd, fwd, fwd, bwd, bwd, bwd, st, st,
                  pl.BlockSpec(masks.shape, lambda b, h, i: (0, 0, 0, 0))],
        out_specs=[fwd, bwd],
        out_shape=[oshape, oshape],
        scratch_shapes=[pltpu.VMEM((HG_DK, HG_DK), F32), pltpu.VMEM((HG_DK, HG_DK), F32)],
        compiler_params=pltpu.CompilerParams(
            dimension_semantics=("parallel", "parallel", "arbitrary"), vmem_limit_bytes=VMEM_LIMIT),
        name="hgrn",
    )(q_hg, v_hg, g_fw, q_hg, v_hg, g_bw, s_fw, s_bw, masks)


def _na_bias_kernel(taps_ref, valid_ref, o_ref):
    W = GRID_W
    for i in range(taps_ref.shape[2]):
        taps = jnp.broadcast_to(taps_ref[0, 0, i:i + 1, :], (W, taps_ref.shape[3]))
        shifted = pltpu.roll(taps, 0, axis=1, stride=1, stride_axis=0)
        o_ref[0, 0, i * W:(i + 1) * W, :] = jnp.where(valid_ref[0, i % NA_STEP_ROWS] > 0.0, shifted, NEG)


def _na_plan(rows):
    assert rows % NA_STEP_ROWS == 0 and rows >= NA_WIN_ROWS and (rows - NA_WIN_ROWS) % NA_STEP_ROWS == 0
    variants, starts = [], []
    for i in range(rows // NA_STEP_ROWS):
        w0 = int(np.clip(NA_STEP_ROWS * i - NA_KR // 2, 0, rows - NA_WIN_ROWS))
        geom = tuple((int(np.clip(r - NA_KR // 2, 0, rows - NA_KR)) - w0, r - w0)
                     for r in range(NA_STEP_ROWS * i, NA_STEP_ROWS * (i + 1)))
        if not variants or variants[-1] != geom:
            assert geom not in variants
            variants.append(geom)
            starts.append(i)
    return variants, starts


def _na_bias(rpb, rows):
    W = GRID_W
    n_dr = 2 * NA_KR - 1
    variants, _ = _na_plan(rows)
    dr = np.array([[[j - qrow + NA_KR - 1 if first <= j < first + NA_KR else n_dr for j in range(NA_WIN_ROWS)]
                    for first, qrow in geom] for geom in variants])
    cols = np.arange(W)
    c0 = np.clip(cols - NA_KC // 2, 0, W - NA_KC)
    col_ok = (cols[None, :] >= c0[:, None]) & (cols[None, :] < c0[:, None] + NA_KC)
    valid = (dr < n_dr)[:, :, None, :, None] & col_ok[None, None, :, None, :]
    valid = valid.reshape(len(variants), NA_STEP_ROWS, W, NA_WIN_ROWS * W).astype(np.float32)

    rpb_ext = jnp.zeros((NA_HEADS, n_dr + 1, 2 * NA_KC - 1), F32).at[:, :n_dr].set(rpb.astype(F32))
    picked = jnp.take(rpb_ext, jnp.asarray(dr.reshape(-1), dtype=jnp.int32), axis=1)
    picked = picked.reshape((NA_HEADS,) + dr.shape + (2 * NA_KC - 1,))
    ahead = picked[..., NA_KC - 1:]
    behind = jnp.roll(picked, -1, axis=3)[..., :NA_KC - 1]
    gap = jnp.zeros(picked.shape[:-1] + (W - 2 * NA_KC + 1,), F32)
    taps = jnp.concatenate([ahead, gap, behind], axis=-1)
    taps = taps.reshape(NA_HEADS // NA_GROUP, NA_GROUP, len(variants), NA_STEP_ROWS, NA_WIN_ROWS * W)
    taps = taps.transpose(2, 0, 1, 3, 4).reshape(len(variants), NA_HEADS // NA_GROUP, NA_GROUP * NA_STEP_ROWS,
                                                 NA_WIN_ROWS * W)
    out_block = (1, 1, NA_GROUP * NA_TOK, NA_WIN_ROWS * W)
    return pl.pallas_call(
        _na_bias_kernel,
        grid=(len(variants), NA_HEADS // NA_GROUP),
        in_specs=[pl.BlockSpec((1, 1) + taps.shape[2:], lambda v, g: (v, g, 0, 0)),
                  pl.BlockSpec((1,) + valid.shape[1:], lambda v, g: (v, 0, 0, 0))],
        out_specs=pl.BlockSpec(out_block, lambda v, g: (v, g, 0, 0)),
        out_shape=jax.ShapeDtypeStruct((len(variants), NA_HEADS // NA_GROUP) + out_block[2:], F32),
        name="na_bias",
    )(taps, jnp.asarray(valid))


def _natten_kernel(q_ref, kt_ref, v_ref, kct_ref, vc_ref, *rest):
    bias_refs, o_ref = rest[:-1], rest[-1]
    W = GRID_W
    rows = v_ref.shape[1] // W
    n_kblk = NA_WIN_ROWS // NA_STEP_ROWS
    rb = lax.broadcasted_iota(jnp.int32, (NA_GROUP * NA_TOK, NA_GW), 0) // NA_TOK
    cb = lax.broadcasted_iota(jnp.int32, (NA_GROUP * NA_TOK, NA_GW), 1) // NA_HEAD_DIM
    diag = rb == cb
    groups = [slice(grp * NA_GW, (grp + 1) * NA_GW) for grp in range(NA_HEADS // NA_GROUP)]

    def logits(u, grp):
        gs = groups[grp]
        pair = len(bias_refs) * pl.program_id(1) + u
        w0 = jnp.clip(NA_STEP_ROWS * pair - NA_KR // 2, 0, rows - NA_WIN_ROWS)
        blk0 = w0 // NA_STEP_ROWS
        v_rows = pl.ds(pl.multiple_of(w0 * W, NA_TOK), NA_WIN_ROWS * W)
        qg = q_ref[0, u * NA_TOK:(u + 1) * NA_TOK, gs]
        kt = jnp.concatenate([kt_ref[0, blk0 + j, gs, :] for j in range(n_kblk)], axis=1)
        kct = jnp.concatenate([kct_ref[0, j, gs, :] for j in range(kct_ref.shape[1])], axis=1)
        qbd = jnp.where(diag, jnp.concatenate([qg] * NA_GROUP, axis=0), jnp.zeros_like(qg[:1]))
        return u, gs, v_rows, _dot(qbd, kt) + bias_refs[u][0, grp], _dot(qbd, kct)

    def attend(u, gs, v_rows, s_win, s_ctx):
        vw = v_ref[0, v_rows, gs]
        m = jnp.maximum(jnp.max(s_win, axis=-1, keepdims=True), jnp.max(s_ctx, axis=-1, keepdims=True))
        p_win = jnp.exp(s_win - m)
        p_ctx = jnp.exp(s_ctx - m)
        denom = jnp.sum(p_win, axis=-1, keepdims=True) + jnp.sum(p_ctx, axis=-1, keepdims=True)
        of = (_dot(p_win.astype(BF16), vw) + _dot(p_ctx.astype(BF16), vc_ref[0, :, gs])) / denom
        of = jnp.where(diag, of, 0.0)
        og = of[0:NA_TOK]
        for h in range(1, NA_GROUP):
            og = og + of[h * NA_TOK:(h + 1) * NA_TOK]
        o_ref[0, u * NA_TOK:(u + 1) * NA_TOK, gs] = og.astype(o_ref.dtype)

    order = [(u, grp) for u in range(len(bias_refs)) for grp in range(len(groups))]
    ready = [logits(*unit) for unit in order[:NA_LOGIT_LEAD]]
    for i in range(len(order)):
        if i + NA_LOGIT_LEAD < len(order):
            ready.append(logits(*order[i + NA_LOGIT_LEAD]))
        attend(*ready.pop(0))


def _natten(q, kt, v, kct, vc, bias):
    B, T, _ = q.shape
    rows = T // GRID_W
    _, starts = _na_plan(rows)
    n_pairs = rows // NA_STEP_ROWS
    assert n_pairs % NA_STEP_PAIRS == 0
    tok = pl.BlockSpec((1, NA_STEP_PAIRS * NA_TOK, NA_WIDTH), lambda b, i: (b, i, 0))
    whole = lambda a: pl.BlockSpec((1,) + a.shape[1:], lambda b, i: (b,) + (0,) * (a.ndim - 1))

    def bias_spec(u):
        def index_map(b, i):
            pair = NA_STEP_PAIRS * i + u
            return (sum((pair >= s).astype(jnp.int32) for s in starts[1:]), 0, 0, 0)
        return pl.BlockSpec((1,) + bias.shape[1:], index_map)

    return pl.pallas_call(
        _natten_kernel,
        grid=(B, n_pairs // NA_STEP_PAIRS),
        in_specs=[tok, whole(kt), whole(v), whole(kct), whole(vc)] + [bias_spec(u) for u in range(NA_STEP_PAIRS)],
        out_specs=tok,
        out_shape=jax.ShapeDtypeStruct((B, T, NA_WIDTH), BF16),
        compiler_params=pltpu.CompilerParams(
            dimension_semantics=("parallel", "arbitrary"), vmem_limit_bytes=VMEM_LIMIT),
        name="natten",
    )(q, kt, v, kct, vc, *([bias] * NA_STEP_PAIRS))


class _CastRing:
    def __init__(self, stage_ref, sem, jobs):
        self.stage_ref, self.sem, self.jobs, self.next = stage_ref, sem, jobs, 0
        self.depth = stage_ref.shape[0]
        for k in range(min(self.depth, len(jobs))):
            self._copy(k).start()

    def _copy(self, k):
        slot = k % self.depth
        return pltpu.make_async_copy(self.jobs[k][0], self.stage_ref.at[slot], self.sem.at[slot])

    def consume(self):
        k = self.next
        _, dst_ref, dst = self.jobs[k]
        self._copy(k).wait()
        dst_ref[dst] = self.stage_ref[k % self.depth].astype(BF16)
        if k + self.depth < len(self.jobs):
            self._copy(k + self.depth).start()
        self.next += 1


def _post_mix_kernel(yna_ref, of_ref, ob_ref, gate_ref, hgw_ref, wo_hbm, x_ref, gtm_ref, nmix_ref,
                     scf_ref, shf_ref, gtf_ref, npre_ref, npost_ref, w1_hbm, w2_hbm, o_ref,
                     act_ref, wo_ref, w1_ref, w2_ref, stage_cols, stage_rows, sem_cols, sem_rows):
    d_ff = w2_ref.shape[0]

    def body(need_wo, need_slab, after_slab):
        o = of_ref[0] + ob_ref[0]
        gate = gate_ref[0].astype(F32)
        hgw = hgw_ref[...]
        parts = []
        for h in range(HG_HEADS):
            sl = slice(h * HG_DK, (h + 1) * HG_DK)
            parts.append((_rms(o[:, sl], hgw) * gate[:, sl]).astype(BF16))
        y_hg = jnp.concatenate(parts, axis=-1)
        need_wo()
        y = _dot(yna_ref[0], wo_ref[:NA_WIDTH, :]) + _dot(y_hg, wo_ref[NA_WIDTH:, :])
        x1 = x_ref[0] + gtm_ref[0] * _rms(y, nmix_ref[...])
        hb = (_rms(x1, npre_ref[...]) * (1.0 + scf_ref[0]) + shf_ref[0]).astype(BF16)
        for j in range(0, d_ff, FFN_COLS):
            need_slab()
            gate_j = _dot(hb, w1_ref[:, j:j + FFN_COLS])
            up_j = _dot(hb, w1_ref[:, d_ff + j:d_ff + j + FFN_COLS])
            act_ref[:, j:j + FFN_COLS] = (_silu(gate_j) * up_j).astype(BF16)
            after_slab()
        z = _dot(act_ref[...], w2_ref[...])
        o_ref[0] = x1 + gtf_ref[0] * _rms(z, npost_ref[...])

    first = (pl.program_id(0) == 0) & (pl.program_id(1) == 0)

    @pl.when(first)
    def _():
        rows, cols = stage_rows.shape[1], stage_cols.shape[2]
        by_rows = lambda hbm, ref: [(hbm.at[pl.ds(r, rows), :], ref, (pl.ds(r, rows), slice(None)))
                                    for r in range(0, ref.shape[0], rows)]
        col_job = lambda c: (w1_hbm.at[:, pl.ds(c, cols)], w1_ref, (slice(None), pl.ds(c, cols)))
        wo_jobs, w2_jobs = by_rows(wo_hbm, wo_ref), by_rows(w2_hbm, w2_ref)
        row_ring = _CastRing(stage_rows, sem_rows, wo_jobs + w2_jobs)
        col_ring = _CastRing(stage_cols, sem_cols,
                             [col_job(c) for j in range(0, d_ff, FFN_COLS) for c in (j, d_ff + j)])

        def need_wo():
            for _ in wo_jobs:
                row_ring.consume()

        def need_slab():
            col_ring.consume()
            col_ring.consume()

        body(need_wo, need_slab, row_ring.consume)
        assert row_ring.next == len(row_ring.jobs) and col_ring.next == len(col_ring.jobs)

    @pl.when(jnp.logical_not(first))
    def _():
        nothing = lambda: None
        body(nothing, nothing, nothing)


def _post_mix(y_na, o_fw, o_bw, gate, hgw, wo, x, gt_m, nmix, sc_f, sh_f, gt_f, npre, npost, w1, w2, tm):
    B, T, _ = x.shape
    d_ff = w2.shape[0]
    stage_rows = FFN_COLS
    assert d_ff % FFN_COLS == 0 and wo.shape[0] % stage_rows == 0 and wo.shape[1] == w2.shape[1]
    hbm = pl.BlockSpec(memory_space=pl.ANY)
    tok = lambda n: pl.BlockSpec((1, tm, n), lambda b, i: (b, i, 0))
    mod = pl.BlockSpec((1, 1, D_MODEL), lambda b, i: (b, 0, 0))
    const = lambda shape: pl.BlockSpec(shape, lambda b, i: (0,) * len(shape), pipeline_mode=pl.Buffered(1))
    return pl.pallas_call(
        _post_mix_kernel,
        grid=(B, T // tm),
        in_specs=[tok(NA_WIDTH), tok(HG_WIDTH), tok(HG_WIDTH), tok(HG_WIDTH), const((1, HG_DK)),
                  hbm, tok(D_MODEL), mod, const((1, D_MODEL)),
                  mod, mod, mod, const((1, D_MODEL)), const((1, D_MODEL)),
                  hbm, hbm],
        out_specs=tok(D_MODEL),
        out_shape=jax.ShapeDtypeStruct(x.shape, F32),
        scratch_shapes=[pltpu.VMEM((tm, d_ff), BF16),
                        pltpu.VMEM(wo.shape, BF16), pltpu.VMEM(w1.shape, BF16), pltpu.VMEM(w2.shape, BF16),
                        pltpu.VMEM((PM_RING, w1.shape[0], FFN_COLS), F32),
                        pltpu.VMEM((PM_RING, stage_rows, w2.shape[1]), F32),
                        pltpu.SemaphoreType.DMA((PM_RING,)), pltpu.SemaphoreType.DMA((PM_RING,))],
        compiler_params=pltpu.CompilerParams(
            dimension_semantics=("arbitrary", "arbitrary"), vmem_limit_bytes=VMEM_LIMIT),
        name="post_mix",
    )(y_na, o_fw, o_bw, gate, hgw, wo, x, gt_m, nmix, sc_f, sh_f, gt_f, npre, npost, w1, w2)


def kernel(x, c, ctx, c_ctx, w_ada, b_ada, norm_mix_pre, norm_mix_post, norm_ffn_pre, norm_ffn_post,
           w_in, na_rpb, hg_lb_logits, hg_norm_w, w_out, w_ffn_in, w_ffn_out):
    B, T, D = x.shape
    assert w_ada.shape[0] == 1, "single-layer stack"
    assert D == D_MODEL and B < SUBLANES and T % GRID_W == 0 and T % HG_BLOCK == 0
    assert ctx.shape[1] % NA_TOK == 0
    rows = T // GRID_W

    cv = jnp.zeros((SUBLANES, D), F32).at[:B].set(c).at[B].set(c_ctx)
    mod = _ada(cv, w_ada[0], b_ada[0][None, :])
    sh_m, sc_m, gt_m, sh_f, sc_f, gt_f = [mod[:, i * D:(i + 1) * D] for i in range(N_MOD)]
    lat = lambda m: m[:B, None, :]
    cx = lambda m: jnp.broadcast_to(m[B][None, None, :], (B, 1, D))

    w_in_bf = w_in[0].astype(BF16)
    lbl = hg_lb_logits.reshape(hg_lb_logits.shape[0], 2 * HG_WIDTH)
    nw_pre = norm_mix_pre[0][None, :]

    q_na, kt_na, v_na, q_hg, g_fw, g_bw, v_hg, gate = _in_proj(x, lat(sc_m), lat(sh_m), nw_pre, w_in_bf, lbl,
                                                                IN_PROJ_ROWS)
    _, kt_c, v_c, _, g_cfw, g_cbw, vhg_c, _ = _in_proj(ctx, cx(sc_m), cx(sh_m), nw_pre, w_in_bf, lbl,
                                                       ctx.shape[1])

    s_fw, s_bw = _ctx_state(g_cfw, g_cbw, vhg_c)
    o_fw, o_bw = _hgrn(q_hg, v_hg, g_fw, g_bw, s_fw, s_bw, HG_BLOCK)

    y_na = _natten(q_na, kt_na, v_na, kt_c, v_c, _na_bias(na_rpb[0], rows))

    return _post_mix(y_na, o_fw, o_bw, gate, hg_norm_w[0][None, :], w_out[0], x, lat(gt_m),
                     norm_mix_post[0][None, :], lat(sc_f), lat(sh_f), lat(gt_f), norm_ffn_pre[0][None, :],
                     norm_ffn_post[0][None, :], w_ffn_in[0], w_ffn_out[0], POST_MIX_ROWS)
```
